```python
import jax, jax.numpy as jnp
from jax import lax
import numpy as np

D_MODEL = 1024
BATCH = 2
SEQ = 8192
DEPTH = 1
DEC_BATCH = 16
DEC_SEQ = 32
PAST_LEN = 4096

CHUNK = 64
D_CONV = D_MODEL // 2
D_POOL = D_MODEL // 2
CONV_WIDTH = 31
CONV_HIST = CONV_WIDTH - 1
POOL_WINDOWS = (2, 4, 8, 16)
N_POOL_GROUPS = len(POOL_WINDOWS)
POOL_GROUP = D_POOL // N_POOL_GROUPS
POOL_HIST = max(POOL_WINDOWS) - 1
D_IN = 2 * D_CONV + D_POOL
D_FF = ((8 * D_MODEL // 3 + 127) // 128) * 128
ALPHA = (2.0 * DEPTH) ** 0.25
BETA = (8.0 * DEPTH) ** -0.25
LN_EPS = 1e-5

kernel_name = "streaming_conv_pool_hybrid_step"


def layer_norm(x, g, b):
    xf = x.astype(jnp.float32)
    mu = jnp.mean(xf, axis=-1, keepdims=True)
    var = jnp.mean(jnp.square(xf - mu), axis=-1, keepdims=True)
    y = (xf - mu) * lax.rsqrt(var + LN_EPS)
    return (y * g.astype(jnp.float32) + b.astype(jnp.float32)).astype(x.dtype)


def swiglu(x, w_g, w_u, w_d):
    return (jax.nn.silu(x @ w_g) * (x @ w_u)) @ w_d


def causal_depthwise_conv(ext, w, b):
    c = ext.shape[-1]
    out = lax.conv_general_dilated(ext, w[:, None, :].astype(ext.dtype), window_strides=(1,),
                                   padding="VALID", dimension_numbers=("NWC", "WIO", "NWC"),
                                   feature_group_count=c)
    return out + b


def multiscale_pool(ext, pos0):
    L = ext.shape[1] - POOL_HIST
    xf = ext.astype(jnp.float32)
    cs = jnp.concatenate([jnp.zeros_like(xf[:, :1]), jnp.cumsum(xf, axis=1)], axis=1)
    end = cs[:, POOL_HIST + 1:]
    pos = pos0 + jnp.arange(L, dtype=jnp.int32)
    outs = []
    for gi, w in enumerate(POOL_WINDOWS):
        sl = slice(gi * POOL_GROUP, (gi + 1) * POOL_GROUP)
        start = cs[:, POOL_HIST + 1 - w: POOL_HIST + 1 - w + L, sl]
        cnt = jnp.minimum(pos + 1, w).astype(jnp.float32)[None, :, None]
        outs.append((end[..., sl] - start) / cnt)
    mean = jnp.concatenate(outs, axis=-1)
    return mean - xf[:, POOL_HIST:]


def encoder_layer(x, conv_prev, pool_prev, pos0,
                  w_ffn1_gate, w_ffn1_up, w_ffn1_down, ln1_g, ln1_b,
                  w_in, w_gate, b_gate, conv_w, conv_b, conv_ln_g, conv_ln_b, w_conv_proj,
                  w_pool_group, pool_scale, w_pool_proj, w_out, ln2_g, ln2_b,
                  w_ffn2_gate, w_ffn2_up, w_ffn2_down, ln3_g, ln3_b):
    h = layer_norm(ALPHA * x + 0.5 * swiglu(x, w_ffn1_gate, w_ffn1_up, w_ffn1_down), ln1_g, ln1_b)

    u = h @ w_in
    conv_a = u[..., :D_CONV]
    conv_b_ = u[..., D_CONV:2 * D_CONV]
    pool_in = u[..., 2 * D_CONV:]

    glu = conv_a * jax.nn.sigmoid(conv_b_)
    conv_ext = jnp.concatenate([conv_prev.astype(glu.dtype), glu], axis=1)
    dc = causal_depthwise_conv(conv_ext, conv_w, conv_b)
    branch_conv = jax.nn.silu(layer_norm(dc, conv_ln_g, conv_ln_b)) @ w_conv_proj

    pool_ext = jnp.concatenate([pool_prev.astype(pool_in.dtype), pool_in], axis=1)
    pooled = multiscale_pool(pool_ext, pos0).astype(pool_in.dtype)
    Bsz, L = pooled.shape[0], pooled.shape[1]
    pg = pooled.reshape(Bsz, L, N_POOL_GROUPS, POOL_GROUP)
    pg = jnp.einsum("blgc,gcd->blgd", pg, w_pool_group).reshape(Bsz, L, D_POOL)
    branch_pool = (pg * pool_scale) @ w_pool_proj

    gates = jax.nn.sigmoid(h @ w_gate + b_gate)
    g_conv = gates[..., :D_MODEL]
    g_pool = gates[..., D_MODEL:]
    mixed = (g_conv * branch_conv + g_pool * branch_pool) @ w_out
    h2 = layer_norm(ALPHA * h + mixed, ln2_g, ln2_b)

    y = layer_norm(ALPHA * h2 + 0.5 * swiglu(h2, w_ffn2_gate, w_ffn2_up, w_ffn2_down), ln3_g, ln3_b)
    return y, conv_ext[:, -CONV_HIST:], pool_ext[:, -POOL_HIST:]


def setup_inputs(seed: int = 0) -> dict:
    key = jax.random.key(seed)
    ks = iter(jax.random.split(key, 40))

    def nrm(shape, scale):
        return jax.random.normal(next(ks), shape, jnp.float32) * scale

    def gain(shape):
        return 1.0 + nrm(shape, 0.02)

    L = DEPTH
    return {
        "x_prompt": nrm((BATCH, SEQ, D_MODEL), 1.0),
        "x_sample": nrm((DEC_BATCH, DEC_SEQ, D_MODEL), 1.0),
        "state_conv": nrm((L, DEC_BATCH, CONV_HIST, D_CONV), 0.5),
        "state_pool": nrm((L, DEC_BATCH, POOL_HIST, D_POOL), 1.0),
        "w_ffn1_gate": nrm((L, D_MODEL, D_FF), D_MODEL ** -0.5),
        "w_ffn1_up": nrm((L, D_MODEL, D_FF), D_MODEL ** -0.5),
        "w_ffn1_down": nrm((L, D_FF, D_MODEL), BETA * D_FF ** -0.5),
        "ln1_g": gain((L, D_MODEL)),
        "ln1_b": nrm((L, D_MODEL), 0.02),
        "w_in": nrm((L, D_MODEL, D_IN), D_MODEL ** -0.5),
        "w_gate": nrm((L, D_MODEL, 2 * D_MODEL), D_MODEL ** -0.5),
        "b_gate": nrm((L, 2 * D_MODEL), 0.02),
        "conv_w": nrm((L, CONV_WIDTH, D_CONV), CONV_WIDTH ** -0.5),
        "conv_b": nrm((L, D_CONV), 0.02),
        "conv_ln_g": gain((L, D_CONV)),
        "conv_ln_b": nrm((L, D_CONV), 0.02),
        "w_conv_proj": nrm((L, D_CONV, D_MODEL), D_CONV ** -0.5),
        "w_pool_group": nrm((L, N_POOL_GROUPS, POOL_GROUP, POOL_GROUP), POOL_GROUP ** -0.5),
        "pool_scale": 1.0 + nrm((L, D_POOL), 0.1),
        "w_pool_proj": nrm((L, D_POOL, D_MODEL), D_POOL ** -0.5),
        "w_out": nrm((L, D_MODEL, D_MODEL), BETA * D_MODEL ** -0.5),
        "ln2_g": gain((L, D_MODEL)),
        "ln2_b": nrm((L, D_MODEL), 0.02),
        "w_ffn2_gate": nrm((L, D_MODEL, D_FF), D_MODEL ** -0.5),
        "w_ffn2_up": nrm((L, D_MODEL, D_FF), D_MODEL ** -0.5),
        "w_ffn2_down": nrm((L, D_FF, D_MODEL), BETA * D_FF ** -0.5),
        "ln3_g": gain((L, D_MODEL)),
        "ln3_b": nrm((L, D_MODEL), 0.02),
    }


def reference(x_prompt, x_sample, state_conv, state_pool,
              w_ffn1_gate, w_ffn1_up, w_ffn1_down, ln1_g, ln1_b,
              w_in, w_gate, b_gate, conv_w, conv_b, conv_ln_g, conv_ln_b, w_conv_proj,
              w_pool_group, pool_scale, w_pool_proj, w_out, ln2_g, ln2_b,
              w_ffn2_gate, w_ffn2_up, w_ffn2_down, ln3_g, ln3_b):
    hp = x_prompt
    hs = x_sample
    conv_p_list, conv_s_list, pool_p_list, pool_s_list = [], [], [], []
    for l in range(DEPTH):
        weights = (w_ffn1_gate[l], w_ffn1_up[l], w_ffn1_down[l], ln1_g[l], ln1_b[l],
                   w_in[l], w_gate[l], b_gate[l], conv_w[l], conv_b[l], conv_ln_g[l], conv_ln_b[l],
                   w_conv_proj[l], w_pool_group[l], pool_scale[l], w_pool_proj[l], w_out[l],
                   ln2_g[l], ln2_b[l], w_ffn2_gate[l], w_ffn2_up[l], w_ffn2_down[l], ln3_g[l], ln3_b[l])
        zero_conv = jnp.zeros((hp.shape[0], CONV_HIST, D_CONV), hp.dtype)
        zero_pool = jnp.zeros((hp.shape[0], POOL_HIST, D_POOL), hp.dtype)
        hp, cp, pp = encoder_layer(hp, zero_conv, zero_pool, 0, *weights)
        hs, cs_, ps_ = encoder_layer(hs, state_conv[l], state_pool[l], PAST_LEN, *weights)
        conv_p_list.append(cp)
        conv_s_list.append(cs_)
        pool_p_list.append(pp)
        pool_s_list.append(ps_)
    new_conv_prompt = jnp.stack(conv_p_list, axis=0)
    new_conv_sample = jnp.stack(conv_s_list, axis=0)
    new_pool_prompt = jnp.stack(pool_p_list, axis=0)
    new_pool_sample = jnp.stack(pool_s_list, axis=0)
    return (hp, hs, new_conv_prompt, new_conv_sample, new_pool_prompt, new_pool_sample)
```

```python
import functools

import jax
import jax.numpy as jnp
from jax import lax
from jax.experimental import pallas as pl
from jax.experimental.pallas import tpu as pltpu

CONV_WIDTH = 31
CONV_HIST = CONV_WIDTH - 1
POOL_WINDOWS = (2, 4, 8, 16)
POOL_HIST = max(POOL_WINDOWS) - 1
LN_EPS = 1e-5
PAST_LEN = 4096

V7X_SUBLANES = 8
V7X_LANES = 128
V7X_VMEM_BYTES = 64 * 1024 * 1024

TOKEN_TILE = 512
STATE_SEQS_PER_STEP = 8
CONV_ROWS = 32
CONV_PAD = 32
POOL_PAD = 16

BF16 = jnp.bfloat16
F32 = jnp.float32


def _layer_norm(z, g, b):
    mu = jnp.mean(z, axis=-1, keepdims=True)
    zc = z - mu
    var = jnp.mean(zc * zc, axis=-1, keepdims=True)
    return zc * lax.rsqrt(var + LN_EPS) * g + b


def _dot(a, b):
    return jnp.dot(a, b, preferred_element_type=F32)


def _ffn_ln_kernel(xp_ref, xs_ref, wg_ref, wu_ref, wd_ref, g_ref, b_ref, op_ref, os_ref, *, n_prompt_tiles, alpha):
    i = pl.program_id(0)
    is_prompt = i < n_prompt_tiles
    x = jnp.where(is_prompt, xp_ref[...], xs_ref[...])
    xb = x.astype(BF16)
    gate = _dot(xb, wg_ref[...])
    up = _dot(xb, wu_ref[...])
    act = (gate * jax.nn.sigmoid(gate) * up).astype(BF16)
    ffn = _dot(act, wd_ref[...])
    y = _layer_norm(alpha * x + 0.5 * ffn, g_ref[...], b_ref[...])

    @pl.when(is_prompt)
    def _():
        op_ref[...] = y

    @pl.when(jnp.logical_not(is_prompt))
    def _():
        os_ref[...] = y


def _resident(shape):
    zeros = (0,) * len(shape)
    return pl.BlockSpec(shape, lambda *_: zeros, pipeline_mode=pl.Buffered(1))


def _nbytes(shape, dtype):
    n = 1
    for s in shape:
        n *= s
    return n * jnp.dtype(dtype).itemsize


def _vmem_limit(estimate_bytes):
    assert estimate_bytes <= V7X_VMEM_BYTES, estimate_bytes
    return int(estimate_bytes)


def _ffn_ln(xp, xs, wg, wu, wd, g, b, alpha):
    n_p, d = xp.shape
    n_s = xs.shape[0]
    d_ff = wg.shape[1]
    t = TOKEN_TILE
    assert n_p % t == 0 and n_s % t == 0
    npt, nst = n_p // t, n_s // t
    tile = (t, d)
    vmem = (2 * _nbytes((d, d_ff), BF16) + _nbytes((d_ff, d), BF16)
            + 4 * 2 * _nbytes(tile, F32)
            + 2 * _nbytes((t, d_ff), F32) + _nbytes((t, d_ff), BF16)
            + 4 * _nbytes(tile, F32))
    kern = functools.partial(_ffn_ln_kernel, n_prompt_tiles=npt, alpha=alpha)
    return pl.pallas_call(
        kern,
        grid=(npt + nst,),
        in_specs=[
            pl.BlockSpec(tile, lambda i: (jnp.minimum(i, npt - 1), 0)),
            pl.BlockSpec(tile, lambda i: (jnp.maximum(i - npt, 0), 0)),
            _resident(wg.shape), _resident(wu.shape), _resident(wd.shape),
            _resident(g.shape), _resident(b.shape),
        ],
        out_specs=[
            pl.BlockSpec(tile, lambda i: (jnp.minimum(i, npt - 1), 0)),
            pl.BlockSpec(tile, lambda i: (jnp.maximum(i - npt, 0), 0)),
        ],
        out_shape=[jax.ShapeDtypeStruct(xp.shape, F32), jax.ShapeDtypeStruct(xs.shape, F32)],
        compiler_params=pltpu.CompilerParams(dimension_semantics=("arbitrary",), vmem_limit_bytes=_vmem_limit(vmem)),
        name="ffn_ln",
    )(xp, xs, wg, wu, wd, g, b)


def _mixer_kernel(*refs, n_seq, seq_len, pos0, has_state, alpha):
    if has_state:
        h_ref, cst_ref, pst_ref = refs[:3]
        refs = refs[3:]
    else:
        h_ref = refs[0]
        refs = refs[1:]
    (w_in_ref, w_gate_ref, b_gate_ref, cw_ref, cb_ref, clg_ref, clb_ref, w_cp_ref, w_pg_ref, psc_ref, w_pp_ref,
     w_out_ref, l2g_ref, l2b_ref, h2_ref, nc_ref, np_ref, cbuf, pext, wbc, act_c) = refs
    S, L = n_seq, seq_len
    N = S * L
    d_model = h_ref.shape[-1]
    d_conv = cbuf.shape[-1]
    d_pool = pext.shape[-1]
    pool_group = d_pool // len(POOL_WINDOWS)
    t_idx = pl.program_id(1)
    n_t = pl.num_programs(1)

    h = h_ref[...].reshape(N, d_model)
    hb = h.astype(BF16)
    u = _dot(hb, w_in_ref[...])
    glu = u[:, :d_conv] * jax.nn.sigmoid(u[:, d_conv:2 * d_conv])
    pool_in = u[:, 2 * d_conv:]

    if has_state:
        for b in range(V7X_SUBLANES):
            lo = CONV_PAD - CONV_HIST + b
            cbuf[b, :, lo:lo + CONV_HIST, :] = cst_ref[0]
        pext[:, POOL_PAD - POOL_HIST:POOL_PAD, :] = pst_ref[0]
    else:
        @pl.when(t_idx == 0)
        def _():
            cbuf[...] = jnp.zeros(cbuf.shape, F32)
            pext[:, :POOL_PAD, :] = jnp.zeros((S, POOL_PAD, d_pool), F32)

        @pl.when(t_idx > 0)
        def _():
            keep = CONV_PAD + V7X_SUBLANES
            cbuf[:, :, :keep, :] = cbuf[:, :, L:L + keep, :]
            pext[:, :POOL_PAD, :] = pext[:, L:L + POOL_PAD, :]

    @pl.when(t_idx == 0)
    def _():
        for k in range(CONV_WIDTH):
            wbc[k] = jnp.broadcast_to(cw_ref[k:k + 1, :], (V7X_SUBLANES, d_conv))

    glu3 = glu.reshape(S, L, d_conv)
    for b in range(V7X_SUBLANES):
        cbuf[b, :, CONV_PAD + b:CONV_PAD + b + L, :] = glu3
    pext[:, POOL_PAD:, :] = pool_in.reshape(S, L, d_pool)

    groups = CONV_ROWS // V7X_SUBLANES
    cbias = jnp.broadcast_to(cb_ref[...], (V7X_SUBLANES, d_conv))
    clg = clg_ref[...]
    clb = clb_ref[...]

    def conv_chunk(idx, carry):
        if S == 1:
            s, r0 = 0, pl.multiple_of(idx * CONV_ROWS, CONV_ROWS)
        else:
            s, r0 = idx, 0
        acc = [cbias] * groups
        for j in range(CONV_WIDTH):
            a, b = divmod(j, V7X_SUBLANES)
            w = wbc[CONV_WIDTH - 1 - j]
            base = r0 + CONV_PAD - V7X_SUBLANES * a
            for gidx in range(groups):
                x = cbuf[b, s, pl.ds(base + V7X_SUBLANES * gidx, V7X_SUBLANES), :]
                acc[gidx] = acc[gidx] + x * w
        dc = jnp.concatenate(acc, axis=0)
        y = _layer_norm(dc, clg, clb)
        y = y * jax.nn.sigmoid(y)
        act_c[pl.ds(pl.multiple_of(s * L + r0, CONV_ROWS), CONV_ROWS), :] = y.astype(BF16)
        return carry

    lax.fori_loop(0, N // CONV_ROWS, conv_chunk, 0)
    branch_conv = _dot(act_c[...], w_cp_ref[...])

    row = lax.broadcasted_iota(jnp.int32, (L, pool_group), 0)
    pos1 = row + (pos0 + 1) + t_idx * L
    pooled_groups = []
    for gi, w in enumerate(POOL_WINDOWS):
        lanes = slice(gi * pool_group, (gi + 1) * pool_group)
        frame = pext[:, POOL_PAD:POOL_PAD + L, lanes]
        total = frame
        for j in range(1, w):
            total = total + pext[:, POOL_PAD - j:POOL_PAD - j + L, lanes]
        cnt = jnp.minimum(pos1, w).astype(F32)
        pooled = total / cnt[None] - frame
        pg = _dot(pooled.reshape(N, pool_group).astype(BF16), w_pg_ref[gi])
        pooled_groups.append(pg * psc_ref[:, lanes])
    pg_all = jnp.concatenate(pooled_groups, axis=-1).astype(BF16)
    branch_pool = _dot(pg_all, w_pp_ref[...])

    gates = jax.nn.sigmoid(_dot(hb, w_gate_ref[...]) + b_gate_ref[...])
    merged = gates[:, :d_model] * branch_conv + gates[:, d_model:] * branch_pool
    mixed = _dot(merged.astype(BF16), w_out_ref[...])
    h2 = _layer_norm(alpha * h + mixed, l2g_ref[...], l2b_ref[...])
    h2_ref[...] = h2.reshape(S, L, d_model)

    @pl.when(t_idx == n_t - 1)
    def _():
        nc_ref[0] = cbuf[0, :, CONV_PAD + L - CONV_HIST:CONV_PAD + L, :]
        np_ref[0] = pext[:, POOL_PAD + L - POOL_HIST:POOL_PAD + L, :]


def _mixer(h, state, weights, pos0, alpha):
    (w_in, w_gate, b_gate, conv_w, conv_b, cln_g, cln_b, w_cp, w_pg, pscale, w_pp, w_out, l2g, l2b) = weights
    bsz, lseq, d = h.shape
    d_conv = conv_w.shape[1]
    d_pool = w_pp.shape[0]
    has_state = state is not None
    if has_state:
        S, L = STATE_SEQS_PER_STEP, lseq
        grid = (bsz // S, 1)
        assert L == CONV_ROWS and bsz % S == 0
    else:
        S, L, grid = 1, TOKEN_TILE, (bsz, lseq // TOKEN_TILE)
        assert lseq % L == 0
    assert L >= CONV_HIST and L >= POOL_HIST and L % CONV_ROWS == 0
    N = S * L
    hblock = (S, L, d)
    hmap = lambda b, t: (b, t, 0)
    cstate_shape = (1, bsz, CONV_HIST, d_conv)
    pstate_shape = (1, bsz, POOL_HIST, d_pool)
    cstate_spec = pl.BlockSpec((1, S, CONV_HIST, d_conv), lambda b, t: (0, b, 0, 0))
    pstate_spec = pl.BlockSpec((1, S, POOL_HIST, d_pool), lambda b, t: (0, b, 0, 0))
    cbuf_shape = (V7X_SUBLANES, S, CONV_PAD + L + V7X_SUBLANES, d_conv)
    pext_shape = (S, POOL_PAD + L, d_pool)
    wbc_shape = (CONV_WIDTH, V7X_SUBLANES, d_conv)

    w_list = [w_in, w_gate, b_gate, conv_w, conv_b, cln_g, cln_b, w_cp, w_pg, pscale, w_pp, w_out, l2g, l2b]
    args = [h] + (list(state) if has_state else []) + w_list
    in_specs = ([pl.BlockSpec(hblock, hmap)] + ([cstate_spec, pstate_spec] if has_state else [])
                + [_resident(w.shape) for w in w_list])
    vmem = (sum(_nbytes(w.shape, w.dtype) for w in w_list)
            + 2 * 2 * _nbytes(hblock, F32)
            + 2 * 2 * (_nbytes((S, CONV_HIST, d_conv), F32) + _nbytes((S, POOL_HIST, d_pool), F32))
            + _nbytes(cbuf_shape, F32) + _nbytes(pext_shape, F32) + _nbytes(wbc_shape, F32) + _nbytes((N, d_conv), BF16)
            + _nbytes((N, w_in.shape[1]), F32) + _nbytes((N, 2 * d), F32)
            + 6 * _nbytes((N, d), F32))
    kern = functools.partial(_mixer_kernel, n_seq=S, seq_len=L, pos0=pos0, has_state=has_state, alpha=alpha)
    return pl.pallas_call(
        kern,
        grid=grid,
        in_specs=in_specs,
        out_specs=[pl.BlockSpec(hblock, hmap), cstate_spec, pstate_spec],
        out_shape=[jax.ShapeDtypeStruct(h.shape, F32), jax.ShapeDtypeStruct(cstate_shape, F32),
                   jax.ShapeDtypeStruct(pstate_shape, F32)],
        scratch_shapes=[pltpu.VMEM(cbuf_shape, F32), pltpu.VMEM(pext_shape, F32), pltpu.VMEM(wbc_shape, F32),
                        pltpu.VMEM((N, d_conv), BF16)],
        compiler_params=pltpu.CompilerParams(dimension_semantics=("arbitrary", "arbitrary"),
                                             vmem_limit_bytes=_vmem_limit(vmem)),
        name="mixer_state" if has_state else "mixer_prompt",
    )(*args)


def kernel(x_prompt, x_sample, state_conv, state_pool, w_ffn1_gate, w_ffn1_up, w_ffn1_down, ln1_g, ln1_b, w_in, w_gate, b_gate, conv_w, conv_b, conv_ln_g, conv_ln_b, w_conv_proj, w_pool_group, pool_scale, w_pool_proj, w_out, ln2_g, ln2_b, w_ffn2_gate, w_ffn2_up, w_ffn2_down, ln3_g, ln3_b):
    depth = w_in.shape[0]
    bsz, seq, d = x_prompt.shape
    dbsz, dseq, _ = x_sample.shape
    alpha = (2.0 * depth) ** 0.25
    row = lambda v: v.reshape(1, -1)

    hp = x_prompt.reshape(bsz * seq, d)
    hs = x_sample.reshape(dbsz * dseq, d)
    conv_p, conv_s, pool_p, pool_s = [], [], [], []
    for l in range(depth):
        hp, hs = _ffn_ln(hp, hs, w_ffn1_gate[l].astype(BF16), w_ffn1_up[l].astype(BF16), w_ffn1_down[l].astype(BF16),
                         row(ln1_g[l]), row(ln1_b[l]), alpha)
        mix_w = (w_in[l].astype(BF16), w_gate[l].astype(BF16), row(b_gate[l]), conv_w[l], row(conv_b[l]),
                 row(conv_ln_g[l]), row(conv_ln_b[l]), w_conv_proj[l].astype(BF16), w_pool_group[l].astype(BF16),
                 row(pool_scale[l]), w_pool_proj[l].astype(BF16), w_out[l].astype(BF16), row(ln2_g[l]), row(ln2_b[l]))
        hp3, cp, pp = _mixer(hp.reshape(bsz, seq, d), None, mix_w, 0, alpha)
        hs3, cs, ps = _mixer(hs.reshape(dbsz, dseq, d), (state_conv[l:l + 1], state_pool[l:l + 1]), mix_w, PAST_LEN, alpha)
        hp, hs = _ffn_ln(hp3.reshape(bsz * seq, d), hs3.reshape(dbsz * dseq, d), w_ffn2_gate[l].astype(BF16),
                         w_ffn2_up[l].astype(BF16), w_ffn2_down[l].astype(BF16), row(ln3_g[l]), row(ln3_b[l]), alpha)
        conv_p.append(cp)
        conv_s.append(cs)
        pool_p.append(pp)
        pool_s.append(ps)
    cat = lambda xs: xs[0] if len(xs) == 1 else jnp.concatenate(xs, axis=0)
    return (hp.reshape(bsz, seq, d), hs.reshape(dbsz, dseq, d), cat(conv_p), cat(conv_s), cat(pool_p), cat(pool_s))
```

```python
import functools

import jax
import jax.numpy as jnp
from jax import lax
from jax.experimental import pallas as pl
from jax.experimental.pallas import tpu as pltpu

CONV_WIDTH = 31
CONV_HIST = CONV_WIDTH - 1
POOL_WINDOWS = (2, 4, 8, 16)
POOL_HIST = max(POOL_WINDOWS) - 1
LN_EPS = 1e-5
PAST_LEN = 4096

V7X_SUBLANES = 8
V7X_LANES = 128
V7X_VMEM_BYTES = 64 * 1024 * 1024

TOKEN_TILE = 512
STATE_SEQS_PER_STEP = 8
CONV_ROWS = 64
CONV_PAD = 32
POOL_PAD = 16

BF16 = jnp.bfloat16
F32 = jnp.float32


def _layer_norm(z, g, b):
    mu = jnp.mean(z, axis=-1, keepdims=True)
    zc = z - mu
    var = jnp.mean(zc * zc, axis=-1, keepdims=True)
    return zc * lax.rsqrt(var + LN_EPS) * g + b


def _dot(a, b):
    return jnp.dot(a, b, preferred_element_type=F32)


def _ffn_ln_kernel(xp_ref, xs_ref, wg_ref, wu_ref, wd_ref, g_ref, b_ref, op_ref, os_ref, *, n_prompt_tiles, alpha):
    i = pl.program_id(0)
    is_prompt = i < n_prompt_tiles
    x = jnp.where(is_prompt, xp_ref[...], xs_ref[...])
    xb = x.astype(BF16)
    gate = _dot(xb, wg_ref[...])
    up = _dot(xb, wu_ref[...])
    act = (gate * jax.nn.sigmoid(gate) * up).astype(BF16)
    ffn = _dot(act, wd_ref[...])
    y = _layer_norm(alpha * x + 0.5 * ffn, g_ref[...], b_ref[...])

    @pl.when(is_prompt)
    def _():
        op_ref[...] = y

    @pl.when(jnp.logical_not(is_prompt))
    def _():
        os_ref[...] = y


def _resident(shape):
    zeros = (0,) * len(shape)
    return pl.BlockSpec(shape, lambda *_: zeros, pipeline_mode=pl.Buffered(1))


def _nbytes(shape, dtype):
    n = 1
    for s in shape:
        n *= s
    return n * jnp.dtype(dtype).itemsize


def _vmem_limit(estimate_bytes):
    assert estimate_bytes <= V7X_VMEM_BYTES, estimate_bytes
    return int(estimate_bytes)


def _ffn_ln(xp, xs, wg, wu, wd, g, b, alpha):
    n_p, d = xp.shape
    n_s = xs.shape[0]
    d_ff = wg.shape[1]
    t = TOKEN_TILE
    assert n_p % t == 0 and n_s % t == 0
    npt, nst = n_p // t, n_s // t
    tile = (t, d)
    vmem = (2 * _nbytes((d, d_ff), BF16) + _nbytes((d_ff, d), BF16)
            + 4 * 2 * _nbytes(tile, F32)
            + 2 * _nbytes((t, d_ff), F32) + _nbytes((t, d_ff), BF16)
            + 4 * _nbytes(tile, F32))
    kern = functools.partial(_ffn_ln_kernel, n_prompt_tiles=npt, alpha=alpha)
    return pl.pallas_call(
        kern,
        grid=(npt + nst,),
        in_specs=[
            pl.BlockSpec(tile, lambda i: (jnp.minimum(i, npt - 1), 0)),
            pl.BlockSpec(tile, lambda i: (jnp.maximum(i - npt, 0), 0)),
            _resident(wg.shape), _resident(wu.shape), _resident(wd.shape),
            _resident(g.shape), _resident(b.shape),
        ],
        out_specs=[
            pl.BlockSpec(tile, lambda i: (jnp.minimum(i, npt - 1), 0)),
            pl.BlockSpec(tile, lambda i: (jnp.maximum(i - npt, 0), 0)),
        ],
        out_shape=[jax.ShapeDtypeStruct(xp.shape, F32), jax.ShapeDtypeStruct(xs.shape, F32)],
        compiler_params=pltpu.CompilerParams(dimension_semantics=("arbitrary",), vmem_limit_bytes=_vmem_limit(vmem)),
        name="ffn_ln",
    )(xp, xs, wg, wu, wd, g, b)


def _mixer_kernel(*refs, n_seq, seq_len, pos0, has_state, alpha):
    if has_state:
        h_ref, cst_ref, pst_ref = refs[:3]
        refs = refs[3:]
    else:
        h_ref = refs[0]
        refs = refs[1:]
    (w_in_ref, w_gate_ref, b_gate_ref, cw_ref, cb_ref, clg_ref, clb_ref, w_cp_ref, w_pg_ref, psc_ref, w_pp_ref,
     w_out_ref, l2g_ref, l2b_ref, h2_ref, nc_ref, np_ref, cbuf, pext, wbc, dconv, hb_ref, gates_ref) = refs
    S, L = n_seq, seq_len
    N = S * L
    d_model = h_ref.shape[-1]
    d_conv = cbuf.shape[-1]
    d_pool = pext.shape[-1]
    pool_group = d_pool // len(POOL_WINDOWS)
    t_idx = pl.program_id(1)
    n_t = pl.num_programs(1)

    h = h_ref[...].reshape(N, d_model)
    hb = h.astype(BF16)
    u = _dot(hb, w_in_ref[...])
    glu = u[:, :d_conv] * jax.nn.sigmoid(u[:, d_conv:2 * d_conv])
    pool_in = u[:, 2 * d_conv:]

    if has_state:
        for b in range(V7X_SUBLANES):
            lo = CONV_PAD - CONV_HIST + b
            cbuf[b, :, lo:lo + CONV_HIST, :] = cst_ref[0]
        pext[:, POOL_PAD - POOL_HIST:POOL_PAD, :] = pst_ref[0]
    else:
        @pl.when(t_idx == 0)
        def _():
            cbuf[...] = jnp.zeros(cbuf.shape, F32)
            pext[:, :POOL_PAD, :] = jnp.zeros((S, POOL_PAD, d_pool), F32)

        @pl.when(t_idx > 0)
        def _():
            keep = CONV_PAD + V7X_SUBLANES
            cbuf[:, :, :keep, :] = cbuf[:, :, L:L + keep, :]
            pext[:, :POOL_PAD, :] = pext[:, L:L + POOL_PAD, :]

    @pl.when(t_idx == 0)
    def _():
        for k in range(CONV_WIDTH):
            wbc[k] = jnp.broadcast_to(cw_ref[k:k + 1, :], (V7X_SUBLANES, d_conv))

    glu3 = glu.reshape(S, L, d_conv)
    for b in range(V7X_SUBLANES):
        cbuf[b, :, CONV_PAD + b:CONV_PAD + b + L, :] = glu3
    pext[:, POOL_PAD:, :] = pool_in.reshape(S, L, d_pool)

    conv_rows = min(CONV_ROWS, L)
    groups = conv_rows // V7X_SUBLANES

    def conv_chunk(idx):
        if S == 1:
            s, r0 = 0, pl.multiple_of(idx * conv_rows, conv_rows)
        else:
            s, r0 = idx, 0
        for c in range(d_conv // V7X_LANES):
            lanes = slice(c * V7X_LANES, (c + 1) * V7X_LANES)
            acc = [jnp.broadcast_to(cb_ref[:, lanes], (V7X_SUBLANES, V7X_LANES))] * groups
            for b in range(V7X_SUBLANES):
                a_max = (CONV_HIST - b) // V7X_SUBLANES
                rows = {g: cbuf[b, s, pl.ds(r0 + CONV_PAD + V7X_SUBLANES * g, V7X_SUBLANES), lanes]
                        for g in range(-a_max, groups)}
                for a in range(a_max + 1):
                    w = wbc[CONV_HIST - (V7X_SUBLANES * a + b), :, lanes]
                    for m in range(groups):
                        acc[m] = acc[m] + rows[m - a] * w
            row0 = pl.multiple_of(s * L + r0, conv_rows)
            dconv[pl.ds(row0, conv_rows), lanes] = jnp.concatenate(acc, axis=0)

    hb_ref[...] = hb

    def conv_and_gate(i, carry):
        conv_chunk(i)
        gates_ref[i] = jax.nn.sigmoid(_dot(hb_ref[...], w_gate_ref[i]) + b_gate_ref[i])
        return carry

    assert N // conv_rows == gates_ref.shape[0] and (S == 1 or conv_rows == L)
    lax.fori_loop(0, N // conv_rows, conv_and_gate, 0)
    conv_act = _layer_norm(dconv[...], clg_ref[...], clb_ref[...])
    conv_act = conv_act * jax.nn.sigmoid(conv_act)
    branch_conv = _dot(conv_act.astype(BF16), w_cp_ref[...])

    row = lax.broadcasted_iota(jnp.int32, (L, pool_group), 0)
    pos1 = row + (pos0 + 1) + t_idx * L
    pooled_groups = []
    for gi, w in enumerate(POOL_WINDOWS):
        lanes = slice(gi * pool_group, (gi + 1) * pool_group)
        frame = pext[:, POOL_PAD:POOL_PAD + L, lanes]
        total = frame
        for j in range(1, w):
            total = total + pext[:, POOL_PAD - j:POOL_PAD - j + L, lanes]
        cnt = jnp.minimum(pos1, w).astype(F32)
        pooled = total / cnt[None] - frame
        pg = _dot(pooled.reshape(N, pool_group).astype(BF16), w_pg_ref[gi])
        pooled_groups.append(pg * psc_ref[:, lanes])
    pg_all = jnp.concatenate(pooled_groups, axis=-1).astype(BF16)
    branch_pool = _dot(pg_all, w_pp_ref[...])

    half = gates_ref.shape[0] // 2
    gcols = gates_ref.shape[-1]
    merged = jnp.concatenate(
        [gates_ref[p] * branch_conv[:, p * gcols:(p + 1) * gcols]
         + gates_ref[half + p] * branch_pool[:, p * gcols:(p + 1) * gcols] for p in range(half)], axis=-1)
    mixed = _dot(merged.astype(BF16), w_out_ref[...])
    h2 = _layer_norm(alpha * h + mixed, l2g_ref[...], l2b_ref[...])
    h2_ref[...] = h2.reshape(S, L, d_model)

    @pl.when(t_idx == n_t - 1)
    def _():
        nc_ref[0] = cbuf[0, :, CONV_PAD + L - CONV_HIST:CONV_PAD + L, :]
        np_ref[0] = pext[:, POOL_PAD + L - POOL_HIST:POOL_PAD + L, :]


def _mixer(h, state, weights, pos0, alpha):
    (w_in, w_gate, b_gate, conv_w, conv_b, cln_g, cln_b, w_cp, w_pg, pscale, w_pp, w_out, l2g, l2b) = weights
    bsz, lseq, d = h.shape
    d_conv = conv_w.shape[1]
    d_pool = w_pp.shape[0]
    has_state = state is not None
    if has_state:
        S, L = STATE_SEQS_PER_STEP, lseq
        grid = (bsz // S, 1)
        assert bsz % S == 0
    else:
        S, L, grid = 1, TOKEN_TILE, (bsz, lseq // TOKEN_TILE)
        assert lseq % L == 0
    assert L >= CONV_HIST and L >= POOL_HIST and L % min(CONV_ROWS, L) == 0
    N = S * L
    hblock = (S, L, d)
    hmap = lambda b, t: (b, t, 0)
    cstate_shape = (1, bsz, CONV_HIST, d_conv)
    pstate_shape = (1, bsz, POOL_HIST, d_pool)
    cstate_spec = pl.BlockSpec((1, S, CONV_HIST, d_conv), lambda b, t: (0, b, 0, 0))
    pstate_spec = pl.BlockSpec((1, S, POOL_HIST, d_pool), lambda b, t: (0, b, 0, 0))
    cbuf_shape = (V7X_SUBLANES, S, CONV_PAD + L + V7X_SUBLANES, d_conv)
    pext_shape = (S, POOL_PAD + L, d_pool)
    wbc_shape = (CONV_WIDTH, V7X_SUBLANES, d_conv)
    n_pieces = N // min(CONV_ROWS, L)
    gates_shape = (n_pieces, N, 2 * d // n_pieces)
    w_gate = w_gate.reshape(d, n_pieces, gates_shape[2]).transpose(1, 0, 2)
    b_gate = b_gate.reshape(n_pieces, 1, gates_shape[2])

    w_list = [w_in, w_gate, b_gate, conv_w, conv_b, cln_g, cln_b, w_cp, w_pg, pscale, w_pp, w_out, l2g, l2b]
    args = [h] + (list(state) if has_state else []) + w_list
    in_specs = ([pl.BlockSpec(hblock, hmap)] + ([cstate_spec, pstate_spec] if has_state else [])
                + [_resident(w.shape) for w in w_list])
    vmem = (sum(_nbytes(w.shape, w.dtype) for w in w_list)
            + 2 * 2 * _nbytes(hblock, F32)
            + 2 * 2 * (_nbytes((S, CONV_HIST, d_conv), F32) + _nbytes((S, POOL_HIST, d_pool), F32))
            + _nbytes(cbuf_shape, F32) + _nbytes(pext_shape, F32) + _nbytes(wbc_shape, F32)
            + _nbytes((N, d_conv), F32) + _nbytes((N, d), BF16) + _nbytes(gates_shape, F32)
            + _nbytes((N, w_in.shape[1]), F32)
            + 6 * _nbytes((N, d), F32))
    kern = functools.partial(_mixer_kernel, n_seq=S, seq_len=L, pos0=pos0, has_state=has_state, alpha=alpha)
    return pl.pallas_call(
        kern,
        grid=grid,
        in_specs=in_specs,
        out_specs=[pl.BlockSpec(hblock, hmap), cstate_spec, pstate_spec],
        out_shape=[jax.ShapeDtypeStruct(h.shape, F32), jax.ShapeDtypeStruct(cstate_shape, F32),
                   jax.ShapeDtypeStruct(pstate_shape, F32)],
        scratch_shapes=[pltpu.VMEM(cbuf_shape, F32), pltpu.VMEM(pext_shape, F32), pltpu.VMEM(wbc_shape, F32),
                        pltpu.VMEM((N, d_conv), F32), pltpu.VMEM((N, d), BF16), pltpu.VMEM(gates_shape, F32)],
        compiler_params=pltpu.CompilerParams(dimension_semantics=("arbitrary", "arbitrary"),
                                             vmem_limit_bytes=_vmem_limit(vmem)),
        name="mixer_state" if has_state else "mixer_prompt",
    )(*args)


def kernel(x_prompt, x_sample, state_conv, state_pool, w_ffn1_gate, w_ffn1_up, w_ffn1_down, ln1_g, ln1_b, w_in, w_gate, b_gate, conv_w, conv_b, conv_ln_g, conv_ln_b, w_conv_proj, w_pool_group, pool_scale, w_pool_proj, w_out, ln2_g, ln2_b, w_ffn2_gate, w_ffn2_up, w_ffn2_down, ln3_g, ln3_b):
    depth = w_in.shape[0]
    bsz, seq, d = x_prompt.shape
    dbsz, dseq, _ = x_sample.shape
    alpha = (2.0 * depth) ** 0.25
    row = lambda v: v.reshape(1, -1)

    hp = x_prompt.reshape(bsz * seq, d)
    hs = x_sample.reshape(dbsz * dseq, d)
    conv_p, conv_s, pool_p, pool_s = [], [], [], []
    for l in range(depth):
        hp, hs = _ffn_ln(hp, hs, w_ffn1_gate[l].astype(BF16), w_ffn1_up[l].astype(BF16), w_ffn1_down[l].astype(BF16),
                         row(ln1_g[l]), row(ln1_b[l]), alpha)
        mix_w = (w_in[l].astype(BF16), w_gate[l].astype(BF16), row(b_gate[l]), conv_w[l], row(conv_b[l]),
                 row(conv_ln_g[l]), row(conv_ln_b[l]), w_conv_proj[l].astype(BF16), w_pool_group[l].astype(BF16),
                 row(pool_scale[l]), w_pool_proj[l].astype(BF16), w_out[l].astype(BF16), row(ln2_g[l]), row(ln2_b[l]))
        hp3, cp, pp = _mixer(hp.reshape(bsz, seq, d), None, mix_w, 0, alpha)
        hs3, cs, ps = _mixer(hs.reshape(dbsz, dseq, d), (state_conv[l:l + 1], state_pool[l:l + 1]), mix_w, PAST_LEN, alpha)
        hp, hs = _ffn_ln(hp3.reshape(bsz * seq, d), hs3.reshape(dbsz * dseq, d), w_ffn2_gate[l].astype(BF16),
                         w_ffn2_up[l].astype(BF16), w_ffn2_down[l].astype(BF16), row(ln3_g[l]), row(ln3_b[l]), alpha)
        conv_p.append(cp)
        conv_s.append(cs)
        pool_p.append(pp)
        pool_s.append(ps)
    cat = lambda xs: xs[0] if len(xs) == 1 else jnp.concatenate(xs, axis=0)
    return (hp.reshape(bsz, seq, d), hs.reshape(dbsz, dseq, d), cat(conv_p), cat(conv_s), cat(pool_p), cat(pool_s))
```

```python
import functools

import jax
import jax.numpy as jnp
from jax import lax
from jax.experimental import pallas as pl
from jax.experimental.pallas import tpu as pltpu

CONV_WIDTH = 31
CONV_HIST = CONV_WIDTH - 1
POOL_WINDOWS = (2, 4, 8, 16)
POOL_HIST = max(POOL_WINDOWS) - 1
LN_EPS = 1e-5
PAST_LEN = 4096

V7X_SUBLANES = 8
V7X_LANES = 128
V7X_MXU_COLS = 256
V7X_VMEM_BYTES = 64 * 1024 * 1024

TOKEN_TILE = 512
STATE_SEQS_PER_STEP = 8
CONV_ROWS = 64
CHUNKS_PER_TRIP = 2
CONV_PAD = 32
POOL_PAD = 16

BF16 = jnp.bfloat16
F32 = jnp.float32


def _layer_norm(z, g, b):
    mu = jnp.mean(z, axis=-1, keepdims=True)
    zc = z - mu
    var = jnp.mean(zc * zc, axis=-1, keepdims=True)
    return zc * lax.rsqrt(var + LN_EPS) * g + b


def _dot(a, b):
    return jnp.dot(a, b, preferred_element_type=F32)


def _resident(shape):
    zeros = (0,) * len(shape)
    return pl.BlockSpec(shape, lambda *_: zeros, pipeline_mode=pl.Buffered(1))


def _nbytes(shape, dtype):
    n = 1
    for s in shape:
        n *= s
    return n * jnp.dtype(dtype).itemsize


def _vmem_limit(estimate_bytes):
    assert estimate_bytes <= V7X_VMEM_BYTES, estimate_bytes
    return int(estimate_bytes)


def _ffn_ln_kernel(xp_ref, xs_ref, wg_ref, wu_ref, wd_ref, g_ref, b_ref, op_ref, os_ref, *, n_prompt_tiles, alpha):
    i = pl.program_id(0)
    is_prompt = i < n_prompt_tiles
    x = jnp.where(is_prompt, xp_ref[...], xs_ref[...])
    xb = x.astype(BF16)
    gate = _dot(xb, wg_ref[...])
    up = _dot(xb, wu_ref[...])
    act = (gate * jax.nn.sigmoid(gate) * up).astype(BF16)
    ffn = _dot(act, wd_ref[...])
    y = _layer_norm(alpha * x + 0.5 * ffn, g_ref[...], b_ref[...])

    @pl.when(is_prompt)
    def _():
        op_ref[...] = y

    @pl.when(jnp.logical_not(is_prompt))
    def _():
        os_ref[...] = y


def _ffn_ln(xp, xs, wg, wu, wd, g, b, alpha):
    n_p, d = xp.shape
    n_s = xs.shape[0]
    d_ff = wg.shape[1]
    t = TOKEN_TILE
    assert n_p % t == 0 and n_s % t == 0
    npt, nst = n_p // t, n_s // t
    tile = (t, d)
    vmem = (2 * _nbytes((d, d_ff), BF16) + _nbytes((d_ff, d), BF16)
            + 4 * 2 * _nbytes(tile, F32)
            + 2 * _nbytes((t, d_ff), F32) + _nbytes((t, d_ff), BF16)
            + 4 * _nbytes(tile, F32))
    kern = functools.partial(_ffn_ln_kernel, n_prompt_tiles=npt, alpha=alpha)
    return pl.pallas_call(
        kern,
        grid=(npt + nst,),
        in_specs=[
            pl.BlockSpec(tile, lambda i: (jnp.minimum(i, npt - 1), 0)),
            pl.BlockSpec(tile, lambda i: (jnp.maximum(i - npt, 0), 0)),
            _resident(wg.shape), _resident(wu.shape), _resident(wd.shape),
            _resident(g.shape), _resident(b.shape),
        ],
        out_specs=[
            pl.BlockSpec(tile, lambda i: (jnp.minimum(i, npt - 1), 0)),
            pl.BlockSpec(tile, lambda i: (jnp.maximum(i - npt, 0), 0)),
        ],
        out_shape=[jax.ShapeDtypeStruct(xp.shape, F32), jax.ShapeDtypeStruct(xs.shape, F32)],
        compiler_params=pltpu.CompilerParams(dimension_semantics=("arbitrary",), vmem_limit_bytes=_vmem_limit(vmem)),
        name="ffn_ln",
    )(xp, xs, wg, wu, wd, g, b)


def _store_delayed(cbuf, glu3):
    L = glu3.shape[1]
    for b in range(V7X_SUBLANES):
        cbuf[b, :, CONV_PAD + b:CONV_PAD + b + L, :] = glu3


def _broadcast_taps(wbc, cw_ref):
    for k in range(CONV_WIDTH):
        wbc[k] = jnp.broadcast_to(cw_ref[k:k + 1, :], wbc.shape[1:])


def _conv_chunk(cbuf, wbc, cb_ref, dconv, s, r0, out_row0, conv_rows):
    groups = conv_rows // V7X_SUBLANES
    for c in range(cbuf.shape[-1] // V7X_LANES):
        lanes = slice(c * V7X_LANES, (c + 1) * V7X_LANES)
        acc = [jnp.broadcast_to(cb_ref[:, lanes], (V7X_SUBLANES, V7X_LANES))] * groups
        for b in range(V7X_SUBLANES):
            a_max = (CONV_HIST - b) // V7X_SUBLANES
            rows = {g: cbuf[b, s, pl.ds(r0 + CONV_PAD + V7X_SUBLANES * g, V7X_SUBLANES), lanes]
                    for g in range(-a_max, groups)}
            for a in range(a_max + 1):
                w = wbc[CONV_HIST - (V7X_SUBLANES * a + b), :, lanes]
                for m in range(groups):
                    acc[m] = acc[m] + rows[m - a] * w
        dconv[pl.ds(out_row0, conv_rows), lanes] = jnp.concatenate(acc, axis=0)


def _pooled(pext, L, pos1):
    pool_group = pext.shape[-1] // len(POOL_WINDOWS)
    out = []
    for gi, w in enumerate(POOL_WINDOWS):
        lanes = slice(gi * pool_group, (gi + 1) * pool_group)
        frame = pext[:, POOL_PAD:POOL_PAD + L, lanes]
        total = frame
        for j in range(1, w):
            total = total + pext[:, POOL_PAD - j:POOL_PAD - j + L, lanes]
        cnt = jnp.minimum(pos1, w).astype(F32)
        out.append(total / cnt[None] - frame)
    return jnp.concatenate(out, axis=-1)


def _merge(h, dconv, pooled_bf16, gates_ref, clg_ref, clb_ref, w_cp_ref, w_pg_ref, psc_ref, w_pp_ref, w_out_ref,
           l2g_ref, l2b_ref, alpha):
    conv_act = _layer_norm(dconv[...], clg_ref[...], clb_ref[...])
    conv_act = conv_act * jax.nn.sigmoid(conv_act)
    branch_conv = _dot(conv_act.astype(BF16), w_cp_ref[...])
    pool_group = w_pg_ref.shape[-1]
    groups = []
    for gi in range(len(POOL_WINDOWS)):
        lanes = slice(gi * pool_group, (gi + 1) * pool_group)
        groups.append(_dot(pooled_bf16[:, lanes], w_pg_ref[gi]) * psc_ref[:, lanes])
    branch_pool = _dot(jnp.concatenate(groups, axis=-1).astype(BF16), w_pp_ref[...])
    half = gates_ref.shape[0] // 2
    gcols = gates_ref.shape[-1]
    merged = jnp.concatenate(
        [gates_ref[p] * branch_conv[:, p * gcols:(p + 1) * gcols]
         + gates_ref[half + p] * branch_pool[:, p * gcols:(p + 1) * gcols] for p in range(half)], axis=-1)
    mixed = _dot(merged.astype(BF16), w_out_ref[...])
    return _layer_norm(alpha * h + mixed, l2g_ref[...], l2b_ref[...])


def _glu(u_ref):
    n = u_ref.shape[0] // 3
    glu = jnp.concatenate([u_ref[p] * jax.nn.sigmoid(u_ref[n + p]) for p in range(n)], axis=-1)
    pool_in = jnp.concatenate([u_ref[2 * n + p] for p in range(n)], axis=-1)
    return glu, pool_in


def _mixer_prompt_kernel(hn_ref, w_in_ref, w_gate_ref, b_gate_ref, cw_ref, cb_ref, clg_ref, clb_ref, w_cp_ref,
                         w_pg_ref, psc_ref, w_pp_ref, w_out_ref, l2g_ref, l2b_ref,
                         h2_ref, nc_ref, np_ref,
                         cbuf, pext, wbc, dconv, hbuf, hbbuf, gates_ref, u_ref, pooled_buf,
                         *, seq_tiles, alpha):
    k = pl.program_id(0)
    n_tiles = pl.num_programs(0) - 1
    L = hn_ref.shape[0]
    d_conv = cbuf.shape[-1]
    d_pool = pext.shape[-1]
    tile = jnp.minimum(k, n_tiles - 1)
    tile_in_seq = tile % seq_tiles

    @pl.when(k == 0)
    def _():
        cbuf[...] = jnp.zeros(cbuf.shape, F32)
        pext[...] = jnp.zeros(pext.shape, F32)
        pooled_buf[...] = jnp.zeros(pooled_buf.shape, BF16)
        hbuf[...] = jnp.zeros(hbuf.shape, F32)
        hbbuf[...] = jnp.zeros(hbbuf.shape, BF16)
        _broadcast_taps(wbc, cw_ref)

    new = k % 2
    old = 1 - new
    hbuf[new] = hn_ref[...]
    hbbuf[new] = hn_ref[...].astype(BF16)

    n_chunks = L // CONV_ROWS
    n_trips = n_chunks // CHUNKS_PER_TRIP
    n_in_trips = u_ref.shape[0] // CHUNKS_PER_TRIP
    assert n_chunks == gates_ref.shape[0] and n_chunks % CHUNKS_PER_TRIP == 0
    assert u_ref.shape[0] % CHUNKS_PER_TRIP == 0 and n_in_trips <= n_trips

    def trip(j, carry, with_input_piece):
        for c in range(CHUNKS_PER_TRIP):
            i = j * CHUNKS_PER_TRIP + c
            r0 = pl.multiple_of(i * CONV_ROWS, CONV_ROWS)
            _conv_chunk(cbuf, wbc, cb_ref, dconv, 0, r0, r0, CONV_ROWS)
            gates_ref[i] = jax.nn.sigmoid(_dot(hbbuf[old], w_gate_ref[i]) + b_gate_ref[i])
            if with_input_piece:
                u_ref[i] = _dot(hbbuf[new], w_in_ref[i])
        return carry

    lax.fori_loop(0, n_in_trips, functools.partial(trip, with_input_piece=True), 0)
    lax.fori_loop(n_in_trips, n_trips, functools.partial(trip, with_input_piece=False), 0)

    h2_ref[...] = _merge(hbuf[old], dconv, pooled_buf[old], gates_ref, clg_ref, clb_ref, w_cp_ref, w_pg_ref, psc_ref,
                         w_pp_ref, w_out_ref, l2g_ref, l2b_ref, alpha)

    keep = CONV_PAD + V7X_SUBLANES
    carried = tile_in_seq > 0
    cbuf[:, :, :keep, :] = jnp.where(carried, cbuf[:, :, L:L + keep, :], 0.0)
    pext[:, :POOL_PAD, :] = jnp.where(carried, pext[:, L:L + POOL_PAD, :], 0.0)
    glu, pool_in = _glu(u_ref)
    _store_delayed(cbuf, glu.reshape(1, L, d_conv))
    pext[:, POOL_PAD:, :] = pool_in.reshape(1, L, d_pool)
    row = lax.broadcasted_iota(jnp.int32, (L, d_pool // len(POOL_WINDOWS)), 0)
    pooled_buf[new] = _pooled(pext, L, row + 1 + tile_in_seq * L).reshape(L, d_pool).astype(BF16)
    nc_ref[0] = cbuf[0, :, CONV_PAD + L - CONV_HIST:CONV_PAD + L, :]
    np_ref[0] = pext[:, POOL_PAD + L - POOL_HIST:POOL_PAD + L, :]


def _mixer_prompt(h, weights, alpha):
    (w_in, w_gate, b_gate, conv_w, conv_b, cln_g, cln_b, w_cp, w_pg, pscale, w_pp, w_out, l2g, l2b) = weights
    bsz, lseq, d = h.shape
    d_conv = conv_w.shape[1]
    d_pool = w_pp.shape[0]
    L = TOKEN_TILE
    assert lseq % L == 0 and L % CONV_ROWS == 0 and L >= CONV_HIST and L >= POOL_HIST
    seq_tiles = lseq // L
    n_tiles = bsz * seq_tiles
    n_gate = L // CONV_ROWS
    n_in = w_in.shape[1] // V7X_MXU_COLS
    assert w_in.shape[1] % V7X_MXU_COLS == 0 and d_conv % V7X_MXU_COLS == 0 and d_pool == d_conv and n_in % 3 == 0
    gcols = 2 * d // n_gate
    w_gate = w_gate.reshape(d, n_gate, gcols).transpose(1, 0, 2)
    b_gate = b_gate.reshape(n_gate, 1, gcols)
    w_in = w_in.reshape(d, n_in, V7X_MXU_COLS).transpose(1, 0, 2)
    h2d = h.reshape(bsz * lseq, d)

    tile = (L, d)
    cstate_shape = (1, bsz, CONV_HIST, d_conv)
    pstate_shape = (1, bsz, POOL_HIST, d_pool)
    state_map = lambda k: (0, jnp.minimum(k, n_tiles - 1) // seq_tiles, 0, 0)
    cbuf_shape = (V7X_SUBLANES, 1, CONV_PAD + L + V7X_SUBLANES, d_conv)
    pext_shape = (1, POOL_PAD + L, d_pool)
    wbc_shape = (CONV_WIDTH, V7X_SUBLANES, d_conv)
    gates_shape = (n_gate, L, gcols)
    u_shape = (n_in, L, V7X_MXU_COLS)
    scratch = [(cbuf_shape, F32), (pext_shape, F32), (wbc_shape, F32), ((L, d_conv), F32), ((2,) + tile, F32),
               ((2,) + tile, BF16), (gates_shape, F32), (u_shape, F32), ((2, L, d_pool), BF16)]

    w_list = [w_in, w_gate, b_gate, conv_w, conv_b, cln_g, cln_b, w_cp, w_pg, pscale, w_pp, w_out, l2g, l2b]
    vmem = (sum(_nbytes(w.shape, w.dtype) for w in w_list)
            + 2 * 2 * _nbytes(tile, F32)
            + 2 * 2 * (_nbytes((1, CONV_HIST, d_conv), F32) + _nbytes((1, POOL_HIST, d_pool), F32))
            + sum(_nbytes(shape, dtype) for shape, dtype in scratch)
            + 6 * _nbytes(tile, F32))
    kern = functools.partial(_mixer_prompt_kernel, seq_tiles=seq_tiles, alpha=alpha)
    h2, conv_state, pool_state = pl.pallas_call(
        kern,
        grid=(n_tiles + 1,),
        in_specs=[pl.BlockSpec(tile, lambda k: (jnp.minimum(k, n_tiles - 1), 0))] + [_resident(w.shape) for w in w_list],
        out_specs=[pl.BlockSpec(tile, lambda k: (jnp.maximum(k - 1, 0), 0)),
                   pl.BlockSpec((1, 1, CONV_HIST, d_conv), state_map),
                   pl.BlockSpec((1, 1, POOL_HIST, d_pool), state_map)],
        out_shape=[jax.ShapeDtypeStruct(h2d.shape, F32), jax.ShapeDtypeStruct(cstate_shape, F32),
                   jax.ShapeDtypeStruct(pstate_shape, F32)],
        scratch_shapes=[pltpu.VMEM(shape, dtype) for shape, dtype in scratch],
        compiler_params=pltpu.CompilerParams(dimension_semantics=("arbitrary",), vmem_limit_bytes=_vmem_limit(vmem)),
        name="mixer_prompt",
    )(h2d, *w_list)
    return h2.reshape(bsz, lseq, d), conv_state, pool_state


def _mixer_state_kernel(h_ref, cst_ref, pst_ref, w_in_ref, w_gate_ref, b_gate_ref, cw_ref, cb_ref, clg_ref, clb_ref,
                        w_cp_ref, w_pg_ref, psc_ref, w_pp_ref, w_out_ref, l2g_ref, l2b_ref,
                        h2_ref, nc_ref, np_ref, cbuf, pext, wbc, dconv, hb_ref, gates_ref, u_ref, *, pos0, alpha):
    S, L, d_model = h_ref.shape
    N = S * L
    d_conv = cbuf.shape[-1]
    d_pool = pext.shape[-1]

    h = h_ref[...].reshape(N, d_model)
    hb_ref[...] = h.astype(BF16)
    for p in range(u_ref.shape[0]):
        u_ref[p] = _dot(hb_ref[...], w_in_ref[p])
    glu, pool_in = _glu(u_ref)

    for b in range(V7X_SUBLANES):
        lo = CONV_PAD - CONV_HIST + b
        cbuf[b, :, lo:lo + CONV_HIST, :] = cst_ref[0]
    pext[:, POOL_PAD - POOL_HIST:POOL_PAD, :] = pst_ref[0]
    _broadcast_taps(wbc, cw_ref)
    _store_delayed(cbuf, glu.reshape(S, L, d_conv))
    pext[:, POOL_PAD:, :] = pool_in.reshape(S, L, d_pool)

    assert S == gates_ref.shape[0]

    def trip(s, carry):
        _conv_chunk(cbuf, wbc, cb_ref, dconv, s, 0, pl.multiple_of(s * L, L), L)
        gates_ref[s] = jax.nn.sigmoid(_dot(hb_ref[...], w_gate_ref[s]) + b_gate_ref[s])
        return carry

    lax.fori_loop(0, S, trip, 0)

    row = lax.broadcasted_iota(jnp.int32, (L, d_pool // len(POOL_WINDOWS)), 0)
    pooled = _pooled(pext, L, row + (pos0 + 1)).reshape(N, d_pool).astype(BF16)
    h2 = _merge(h, dconv, pooled, gates_ref, clg_ref, clb_ref, w_cp_ref, w_pg_ref, psc_ref, w_pp_ref, w_out_ref,
                l2g_ref, l2b_ref, alpha)
    h2_ref[...] = h2.reshape(S, L, d_model)
    nc_ref[0] = cbuf[0, :, CONV_PAD + L - CONV_HIST:CONV_PAD + L, :]
    np_ref[0] = pext[:, POOL_PAD + L - POOL_HIST:POOL_PAD + L, :]


def _mixer_state(h, conv_state, pool_state, weights, pos0, alpha):
    (w_in, w_gate, b_gate, conv_w, conv_b, cln_g, cln_b, w_cp, w_pg, pscale, w_pp, w_out, l2g, l2b) = weights
    bsz, L, d = h.shape
    d_conv = conv_w.shape[1]
    d_pool = w_pp.shape[0]
    S = STATE_SEQS_PER_STEP
    N = S * L
    assert bsz % S == 0 and L % V7X_SUBLANES == 0 and L >= CONV_HIST and L >= POOL_HIST
    n_in = w_in.shape[1] // V7X_MXU_COLS
    gcols = 2 * d // S
    w_gate = w_gate.reshape(d, S, gcols).transpose(1, 0, 2)
    b_gate = b_gate.reshape(S, 1, gcols)
    w_in = w_in.reshape(d, n_in, V7X_MXU_COLS).transpose(1, 0, 2)

    hblock = (S, L, d)
    hmap = lambda i: (i, 0, 0)
    cstate_spec = pl.BlockSpec((1, S, CONV_HIST, d_conv), lambda i: (0, i, 0, 0))
    pstate_spec = pl.BlockSpec((1, S, POOL_HIST, d_pool), lambda i: (0, i, 0, 0))
    scratch = [((V7X_SUBLANES, S, CONV_PAD + L + V7X_SUBLANES, d_conv), F32), ((S, POOL_PAD + L, d_pool), F32),
               ((CONV_WIDTH, V7X_SUBLANES, d_conv), F32), ((N, d_conv), F32), ((N, d), BF16), ((S, N, gcols), F32),
               ((n_in, N, V7X_MXU_COLS), F32)]
    w_list = [w_in, w_gate, b_gate, conv_w, conv_b, cln_g, cln_b, w_cp, w_pg, pscale, w_pp, w_out, l2g, l2b]
    vmem = (sum(_nbytes(w.shape, w.dtype) for w in w_list)
            + 2 * 2 * _nbytes(hblock, F32)
            + 2 * 2 * (_nbytes((S, CONV_HIST, d_conv), F32) + _nbytes((S, POOL_HIST, d_pool), F32))
            + sum(_nbytes(shape, dtype) for shape, dtype in scratch)
            + 8 * _nbytes((N, d), F32))
    kern = functools.partial(_mixer_state_kernel, pos0=pos0, alpha=alpha)
    return pl.pallas_call(
        kern,
        grid=(bsz // S,),
        in_specs=[pl.BlockSpec(hblock, hmap), cstate_spec, pstate_spec] + [_resident(w.shape) for w in w_list],
        out_specs=[pl.BlockSpec(hblock, hmap), cstate_spec, pstate_spec],
        out_shape=[jax.ShapeDtypeStruct(h.shape, F32), jax.ShapeDtypeStruct(conv_state.shape, F32),
                   jax.ShapeDtypeStruct(pool_state.shape, F32)],
        scratch_shapes=[pltpu.VMEM(shape, dtype) for shape, dtype in scratch],
        compiler_params=pltpu.CompilerParams(dimension_semantics=("arbitrary",), vmem_limit_bytes=_vmem_limit(vmem)),
        name="mixer_state",
    )(h, conv_state, pool_state, *w_list)


def kernel(x_prompt, x_sample, state_conv, state_pool, w_ffn1_gate, w_ffn1_up, w_ffn1_down, ln1_g, ln1_b, w_in, w_gate, b_gate, conv_w, conv_b, conv_ln_g, conv_ln_b, w_conv_proj, w_pool_group, pool_scale, w_pool_proj, w_out, ln2_g, ln2_b, w_ffn2_gate, w_ffn2_up, w_ffn2_down, ln3_g, ln3_b):
    depth = w_in.shape[0]
    bsz, seq, d = x_prompt.shape
    dbsz, dseq, _ = x_sample.shape
    alpha = (2.0 * depth) ** 0.25
    row = lambda v: v.reshape(1, -1)

    hp = x_prompt.reshape(bsz * seq, d)
    hs = x_sample.reshape(dbsz * dseq, d)
    conv_p, conv_s, pool_p, pool_s = [], [], [], []
    for l in range(depth):
        hp, hs = _ffn_ln(hp, hs, w_ffn1_gate[l].astype(BF16), w_ffn1_up[l].astype(BF16), w_ffn1_down[l].astype(BF16),
                         row(ln1_g[l]), row(ln1_b[l]), alpha)
        mix_w = (w_in[l].astype(BF16), w_gate[l].astype(BF16), row(b_gate[l]), conv_w[l], row(conv_b[l]),
                 row(conv_ln_g[l]), row(conv_ln_b[l]), w_conv_proj[l].astype(BF16), w_pool_group[l].astype(BF16),
                 row(pool_scale[l]), w_pool_proj[l].astype(BF16), w_out[l].astype(BF16), row(ln2_g[l]), row(ln2_b[l]))
        hp3, cp, pp = _mixer_prompt(hp.reshape(bsz, seq, d), mix_w, alpha)
        hs3, cs, ps = _mixer_state(hs.reshape(dbsz, dseq, d), state_conv[l:l + 1], state_pool[l:l + 1], mix_w, PAST_LEN, alpha)
        hp, hs = _ffn_ln(hp3.reshape(bsz * seq, d), hs3.reshape(dbsz * dseq, d), w_ffn2_gate[l].astype(BF16),
                         w_ffn2_up[l].astype(BF16), w_ffn2_down[l].astype(BF16), row(ln3_g[l]), row(ln3_b[l]), alpha)
        conv_p.append(cp)
        conv_s.append(cs)
        pool_p.append(pp)
        pool_s.append(ps)
    cat = lambda xs: xs[0] if len(xs) == 1 else jnp.concatenate(xs, axis=0)
    return (hp.reshape(bsz, seq, d), hs.reshape(dbsz, dseq, d), cat(conv_p), cat(conv_s), cat(pool_p), cat(pool_s))
```

```python
import functools

import jax
import jax.numpy as jnp
from jax import lax
from jax.experimental import pallas as pl
from jax.experimental.pallas import tpu as pltpu

CONV_WIDTH = 31
CONV_HIST = CONV_WIDTH - 1
POOL_WINDOWS = (2, 4, 8, 16)
POOL_HIST = max(POOL_WINDOWS) - 1
LN_EPS = 1e-5
PAST_LEN = 4096

V7X_SUBLANES = 8
V7X_LANES = 128
V7X_MXU_COLS = 256
V7X_VMEM_BYTES = 64 * 1024 * 1024

TOKEN_TILE = 512
STATE_SEQS_PER_STEP = 8
CONV_ROWS = 64
CONV_PAD = 32
POOL_PAD = 16

BF16 = jnp.bfloat16
F32 = jnp.float32


def _layer_norm(z, g, b):
    mu = jnp.mean(z, axis=-1, keepdims=True)
    zc = z - mu
    var = jnp.mean(zc * zc, axis=-1, keepdims=True)
    return zc * lax.rsqrt(var + LN_EPS) * g + b


def _dot(a, b):
    return jnp.dot(a, b, preferred_element_type=F32)


def _resident(shape):
    zeros = (0,) * len(shape)
    return pl.BlockSpec(shape, lambda *_: zeros, pipeline_mode=pl.Buffered(1))


def _nbytes(shape, dtype):
    n = 1
    for s in shape:
        n *= s
    return n * jnp.dtype(dtype).itemsize


def _vmem_limit(estimate_bytes):
    assert estimate_bytes <= V7X_VMEM_BYTES, estimate_bytes
    return int(estimate_bytes)


def _ffn_ln_kernel(xp_ref, xs_ref, wg_ref, wu_ref, wd_ref, g_ref, b_ref, op_ref, os_ref, *, n_prompt_tiles, alpha):
    i = pl.program_id(0)
    is_prompt = i < n_prompt_tiles
    x = jnp.where(is_prompt, xp_ref[...], xs_ref[...])
    xb = x.astype(BF16)
    gate = _dot(xb, wg_ref[...])
    up = _dot(xb, wu_ref[...])
    act = (gate * jax.nn.sigmoid(gate) * up).astype(BF16)
    ffn = _dot(act, wd_ref[...])
    y = _layer_norm(alpha * x + 0.5 * ffn, g_ref[...], b_ref[...])

    @pl.when(is_prompt)
    def _():
        op_ref[...] = y

    @pl.when(jnp.logical_not(is_prompt))
    def _():
        os_ref[...] = y


def _ffn_ln(xp, xs, wg, wu, wd, g, b, alpha):
    n_p, d = xp.shape
    n_s = xs.shape[0]
    d_ff = wg.shape[1]
    t = TOKEN_TILE
    assert n_p % t == 0 and n_s % t == 0
    npt, nst = n_p // t, n_s // t
    tile = (t, d)
    vmem = (2 * _nbytes((d, d_ff), BF16) + _nbytes((d_ff, d), BF16)
            + 4 * 2 * _nbytes(tile, F32)
            + 2 * _nbytes((t, d_ff), F32) + _nbytes((t, d_ff), BF16)
            + 4 * _nbytes(tile, F32))
    kern = functools.partial(_ffn_ln_kernel, n_prompt_tiles=npt, alpha=alpha)
    return pl.pallas_call(
        kern,
        grid=(npt + nst,),
        in_specs=[
            pl.BlockSpec(tile, lambda i: (jnp.minimum(i, npt - 1), 0)),
            pl.BlockSpec(tile, lambda i: (jnp.maximum(i - npt, 0), 0)),
            _resident(wg.shape), _resident(wu.shape), _resident(wd.shape),
            _resident(g.shape), _resident(b.shape),
        ],
        out_specs=[
            pl.BlockSpec(tile, lambda i: (jnp.minimum(i, npt - 1), 0)),
            pl.BlockSpec(tile, lambda i: (jnp.maximum(i - npt, 0), 0)),
        ],
        out_shape=[jax.ShapeDtypeStruct(xp.shape, F32), jax.ShapeDtypeStruct(xs.shape, F32)],
        compiler_params=pltpu.CompilerParams(dimension_semantics=("arbitrary",), vmem_limit_bytes=_vmem_limit(vmem)),
        name="ffn_ln",
    )(xp, xs, wg, wu, wd, g, b)


def _store_delayed(cbuf, glu3):
    L = glu3.shape[1]
    for b in range(V7X_SUBLANES):
        cbuf[b, :, CONV_PAD + b:CONV_PAD + b + L, :] = glu3


def _broadcast_taps(wbc, cw_ref):
    for k in range(CONV_WIDTH):
        wbc[k] = jnp.broadcast_to(cw_ref[k:k + 1, :], wbc.shape[1:])


def _conv_chunk(cbuf, wbc, cb_ref, dconv, s, r0, out_row0, conv_rows):
    groups = conv_rows // V7X_SUBLANES
    for c in range(cbuf.shape[-1] // V7X_LANES):
        lanes = slice(c * V7X_LANES, (c + 1) * V7X_LANES)
        acc = [jnp.broadcast_to(cb_ref[:, lanes], (V7X_SUBLANES, V7X_LANES))] * groups
        for b in range(V7X_SUBLANES):
            a_max = (CONV_HIST - b) // V7X_SUBLANES
            rows = {g: cbuf[b, s, pl.ds(r0 + CONV_PAD + V7X_SUBLANES * g, V7X_SUBLANES), lanes]
                    for g in range(-a_max, groups)}
            for a in range(a_max + 1):
                w = wbc[CONV_HIST - (V7X_SUBLANES * a + b), :, lanes]
                for m in range(groups):
                    acc[m] = acc[m] + rows[m - a] * w
        dconv[pl.ds(out_row0, conv_rows), lanes] = jnp.concatenate(acc, axis=0)


def _pooled(pext, L, pos1):
    pool_group = pext.shape[-1] // len(POOL_WINDOWS)
    out = []
    for gi, w in enumerate(POOL_WINDOWS):
        lanes = slice(gi * pool_group, (gi + 1) * pool_group)
        frame = pext[:, POOL_PAD:POOL_PAD + L, lanes]
        total = frame
        for j in range(1, w):
            total = total + pext[:, POOL_PAD - j:POOL_PAD - j + L, lanes]
        cnt = jnp.minimum(pos1, w).astype(F32)
        out.append(total / cnt[None] - frame)
    return jnp.concatenate(out, axis=-1)


def _merge(h, dconv, pooled_bf16, gate_piece, n_pieces, clg_ref, clb_ref, w_cp_ref, w_pg_ref, psc_ref, w_pp_ref,
           w_out_ref, l2g_ref, l2b_ref, alpha):
    conv_act = _layer_norm(dconv[...], clg_ref[...], clb_ref[...])
    conv_act = conv_act * jax.nn.sigmoid(conv_act)
    branch_conv = _dot(conv_act.astype(BF16), w_cp_ref[...])
    pool_group = w_pg_ref.shape[-1]
    groups = []
    for gi in range(len(POOL_WINDOWS)):
        lanes = slice(gi * pool_group, (gi + 1) * pool_group)
        groups.append(_dot(pooled_bf16[:, lanes], w_pg_ref[gi]) * psc_ref[:, lanes])
    branch_pool = _dot(jnp.concatenate(groups, axis=-1).astype(BF16), w_pp_ref[...])
    half = n_pieces // 2
    gcols = branch_conv.shape[-1] // half
    merged = jnp.concatenate(
        [gate_piece(p) * branch_conv[:, p * gcols:(p + 1) * gcols]
         + gate_piece(half + p) * branch_pool[:, p * gcols:(p + 1) * gcols] for p in range(half)], axis=-1)
    mixed = _dot(merged.astype(BF16), w_out_ref[...])
    return _layer_norm(alpha * h + mixed, l2g_ref[...], l2b_ref[...])


def _mixer_prompt_kernel(h_ref, w_in_ref, w_gate_ref, b_gate_ref, cw_ref, cb_ref, clg_ref, clb_ref, w_cp_ref,
                         w_pg_ref, psc_ref, w_pp_ref, w_out_ref, l2g_ref, l2b_ref,
                         h2_ref, nc_ref, np_ref, cbuf, pext, wbc, dconv, *, alpha):
    t_idx = pl.program_id(1)
    L, d_model = h_ref.shape[1:]
    d_conv = cbuf.shape[-1]
    d_pool = pext.shape[-1]

    @pl.when(t_idx == 0)
    def _():
        cbuf[...] = jnp.zeros(cbuf.shape, F32)
        pext[:, :POOL_PAD, :] = jnp.zeros((1, POOL_PAD, d_pool), F32)
        _broadcast_taps(wbc, cw_ref)

    @pl.when(t_idx > 0)
    def _():
        keep = CONV_PAD + V7X_SUBLANES
        cbuf[:, :, :keep, :] = cbuf[:, :, L:L + keep, :]
        pext[:, :POOL_PAD, :] = pext[:, L:L + POOL_PAD, :]

    h = h_ref[0]
    hb = h.astype(BF16)
    u = _dot(hb, w_in_ref[...])
    gates = jax.nn.sigmoid(_dot(hb, w_gate_ref[...]) + b_gate_ref[...])
    glu = u[:, :d_conv] * jax.nn.sigmoid(u[:, d_conv:2 * d_conv])
    _store_delayed(cbuf, glu.reshape(1, L, d_conv))
    pext[:, POOL_PAD:, :] = u[:, 2 * d_conv:].reshape(1, L, d_pool)
    row = lax.broadcasted_iota(jnp.int32, (L, d_pool // len(POOL_WINDOWS)), 0)
    pooled = _pooled(pext, L, row + 1 + t_idx * L).reshape(L, d_pool).astype(BF16)

    def conv_trip(i, carry):
        r0 = pl.multiple_of(i * CONV_ROWS, CONV_ROWS)
        _conv_chunk(cbuf, wbc, cb_ref, dconv, 0, r0, r0, CONV_ROWS)
        return carry

    lax.fori_loop(0, L // CONV_ROWS, conv_trip, 0)

    n_pieces = 2 * d_model // V7X_MXU_COLS
    gate_piece = lambda p: gates[:, p * V7X_MXU_COLS:(p + 1) * V7X_MXU_COLS]
    h2_ref[0] = _merge(h, dconv, pooled, gate_piece, n_pieces, clg_ref, clb_ref, w_cp_ref, w_pg_ref, psc_ref, w_pp_ref,
                       w_out_ref, l2g_ref, l2b_ref, alpha)

    @pl.when(t_idx == pl.num_programs(1) - 1)
    def _():
        nc_ref[0] = cbuf[0, :, CONV_PAD + L - CONV_HIST:CONV_PAD + L, :]
        np_ref[0] = pext[:, POOL_PAD + L - POOL_HIST:POOL_PAD + L, :]


def _mixer_prompt(h, weights, alpha):
    (w_in, w_gate, b_gate, conv_w, conv_b, cln_g, cln_b, w_cp, w_pg, pscale, w_pp, w_out, l2g, l2b) = weights
    bsz, lseq, d = h.shape
    d_conv = conv_w.shape[1]
    d_pool = w_pp.shape[0]
    L = TOKEN_TILE
    assert lseq % L == 0 and L % CONV_ROWS == 0 and L >= CONV_HIST and L >= POOL_HIST
    hblock = (1, L, d)
    hmap = lambda b, t: (b, t, 0)
    state_map = lambda b, t: (0, b, 0, 0)
    scratch = [((V7X_SUBLANES, 1, CONV_PAD + L + V7X_SUBLANES, d_conv), F32), ((1, POOL_PAD + L, d_pool), F32),
               ((CONV_WIDTH, V7X_SUBLANES, d_conv), F32), ((L, d_conv), F32)]
    w_list = [w_in, w_gate, b_gate, conv_w, conv_b, cln_g, cln_b, w_cp, w_pg, pscale, w_pp, w_out, l2g, l2b]
    vmem = (sum(_nbytes(w.shape, w.dtype) for w in w_list)
            + 2 * 2 * _nbytes(hblock, F32)
            + 2 * 2 * (_nbytes((1, CONV_HIST, d_conv), F32) + _nbytes((1, POOL_HIST, d_pool), F32))
            + sum(_nbytes(shape, dtype) for shape, dtype in scratch)
            + _nbytes((L, w_in.shape[1]), F32) + _nbytes((L, 2 * d), F32)
            + 8 * _nbytes((L, d), F32))
    kern = functools.partial(_mixer_prompt_kernel, alpha=alpha)
    return pl.pallas_call(
        kern,
        grid=(bsz, lseq // L),
        in_specs=[pl.BlockSpec(hblock, hmap)] + [_resident(w.shape) for w in w_list],
        out_specs=[pl.BlockSpec(hblock, hmap), pl.BlockSpec((1, 1, CONV_HIST, d_conv), state_map),
                   pl.BlockSpec((1, 1, POOL_HIST, d_pool), state_map)],
        out_shape=[jax.ShapeDtypeStruct(h.shape, F32), jax.ShapeDtypeStruct((1, bsz, CONV_HIST, d_conv), F32),
                   jax.ShapeDtypeStruct((1, bsz, POOL_HIST, d_pool), F32)],
        scratch_shapes=[pltpu.VMEM(shape, dtype) for shape, dtype in scratch],
        compiler_params=pltpu.CompilerParams(dimension_semantics=("arbitrary", "arbitrary"),
                                             vmem_limit_bytes=_vmem_limit(vmem)),
        name="mixer_prompt",
    )(h, *w_list)


def _mixer_state_kernel(h_ref, cst_ref, pst_ref, w_in_ref, w_gate_ref, b_gate_ref, cw_ref, cb_ref, clg_ref, clb_ref,
                        w_cp_ref, w_pg_ref, psc_ref, w_pp_ref, w_out_ref, l2g_ref, l2b_ref,
                        h2_ref, nc_ref, np_ref, cbuf, pext, wbc, dconv, *, pos0, alpha):
    S, L, d_model = h_ref.shape
    N = S * L
    d_conv = cbuf.shape[-1]
    d_pool = pext.shape[-1]

    h = h_ref[...].reshape(N, d_model)
    hb = h.astype(BF16)
    u = _dot(hb, w_in_ref[...])
    gates = jax.nn.sigmoid(_dot(hb, w_gate_ref[...]) + b_gate_ref[...])
    glu = u[:, :d_conv] * jax.nn.sigmoid(u[:, d_conv:2 * d_conv])

    for b in range(V7X_SUBLANES):
        lo = CONV_PAD - CONV_HIST + b
        cbuf[b, :, lo:lo + CONV_HIST, :] = cst_ref[0]
    pext[:, POOL_PAD - POOL_HIST:POOL_PAD, :] = pst_ref[0]
    _broadcast_taps(wbc, cw_ref)
    _store_delayed(cbuf, glu.reshape(S, L, d_conv))
    pext[:, POOL_PAD:, :] = u[:, 2 * d_conv:].reshape(S, L, d_pool)
    row = lax.broadcasted_iota(jnp.int32, (L, d_pool // len(POOL_WINDOWS)), 0)
    pooled = _pooled(pext, L, row + (pos0 + 1)).reshape(N, d_pool).astype(BF16)

    def conv_trip(s, carry):
        _conv_chunk(cbuf, wbc, cb_ref, dconv, s, 0, pl.multiple_of(s * L, L), L)
        return carry

    lax.fori_loop(0, S, conv_trip, 0)

    n_pieces = 2 * d_model // V7X_MXU_COLS
    gate_piece = lambda p: gates[:, p * V7X_MXU_COLS:(p + 1) * V7X_MXU_COLS]
    h2 = _merge(h, dconv, pooled, gate_piece, n_pieces, clg_ref, clb_ref, w_cp_ref, w_pg_ref, psc_ref, w_pp_ref,
                w_out_ref, l2g_ref, l2b_ref, alpha)
    h2_ref[...] = h2.reshape(S, L, d_model)
    nc_ref[0] = cbuf[0, :, CONV_PAD + L - CONV_HIST:CONV_PAD + L, :]
    np_ref[0] = pext[:, POOL_PAD + L - POOL_HIST:POOL_PAD + L, :]


def _mixer_state(h, conv_state, pool_state, weights, pos0, alpha):
    (w_in, w_gate, b_gate, conv_w, conv_b, cln_g, cln_b, w_cp, w_pg, pscale, w_pp, w_out, l2g, l2b) = weights
    bsz, L, d = h.shape
    d_conv = conv_w.shape[1]
    d_pool = w_pp.shape[0]
    S = STATE_SEQS_PER_STEP
    N = S * L
    assert bsz % S == 0 and L % V7X_SUBLANES == 0 and L >= CONV_HIST and L >= POOL_HIST

    hblock = (S, L, d)
    hmap = lambda i: (i, 0, 0)
    cstate_spec = pl.BlockSpec((1, S, CONV_HIST, d_conv), lambda i: (0, i, 0, 0))
    pstate_spec = pl.BlockSpec((1, S, POOL_HIST, d_pool), lambda i: (0, i, 0, 0))
    scratch = [((V7X_SUBLANES, S, CONV_PAD + L + V7X_SUBLANES, d_conv), F32), ((S, POOL_PAD + L, d_pool), F32),
               ((CONV_WIDTH, V7X_SUBLANES, d_conv), F32), ((N, d_conv), F32)]
    w_list = [w_in, w_gate, b_gate, conv_w, conv_b, cln_g, cln_b, w_cp, w_pg, pscale, w_pp, w_out, l2g, l2b]
    vmem = (sum(_nbytes(w.shape, w.dtype) for w in w_list)
            + 2 * 2 * _nbytes(hblock, F32)
            + 2 * 2 * (_nbytes((S, CONV_HIST, d_conv), F32) + _nbytes((S, POOL_HIST, d_pool), F32))
            + sum(_nbytes(shape, dtype) for shape, dtype in scratch)
            + _nbytes((N, w_in.shape[1]), F32) + _nbytes((N, 2 * d), F32)
            + 8 * _nbytes((N, d), F32))
    kern = functools.partial(_mixer_state_kernel, pos0=pos0, alpha=alpha)
    return pl.pallas_call(
        kern,
        grid=(bsz // S,),
        in_specs=[pl.BlockSpec(hblock, hmap), cstate_spec, pstate_spec] + [_resident(w.shape) for w in w_list],
        out_specs=[pl.BlockSpec(hblock, hmap), cstate_spec, pstate_spec],
        out_shape=[jax.ShapeDtypeStruct(h.shape, F32), jax.ShapeDtypeStruct(conv_state.shape, F32),
                   jax.ShapeDtypeStruct(pool_state.shape, F32)],
        scratch_shapes=[pltpu.VMEM(shape, dtype) for shape, dtype in scratch],
        compiler_params=pltpu.CompilerParams(dimension_semantics=("arbitrary",), vmem_limit_bytes=_vmem_limit(vmem)),
        name="mixer_state",
    )(h, conv_state, pool_state, *w_list)


def kernel(x_prompt, x_sample, state_conv, state_pool, w_ffn1_gate, w_ffn1_up, w_ffn1_down, ln1_g, ln1_b, w_in, w_gate, b_gate, conv_w, conv_b, conv_ln_g, conv_ln_b, w_conv_proj, w_pool_group, pool_scale, w_pool_proj, w_out, ln2_g, ln2_b, w_ffn2_gate, w_ffn2_up, w_ffn2_down, ln3_g, ln3_b):
    depth = w_in.shape[0]
    bsz, seq, d = x_prompt.shape
    dbsz, dseq, _ = x_sample.shape
    alpha = (2.0 * depth) ** 0.25
    row = lambda v: v.reshape(1, -1)

    hp = x_prompt.reshape(bsz * seq, d)
    hs = x_sample.reshape(dbsz * dseq, d)
    conv_p, conv_s, pool_p, pool_s = [], [], [], []
    for l in range(depth):
        hp, hs = _ffn_ln(hp, hs, w_ffn1_gate[l].astype(BF16), w_ffn1_up[l].astype(BF16), w_ffn1_down[l].astype(BF16),
                         row(ln1_g[l]), row(ln1_b[l]), alpha)
        mix_w = (w_in[l].astype(BF16), w_gate[l].astype(BF16), row(b_gate[l]), conv_w[l], row(conv_b[l]),
                 row(conv_ln_g[l]), row(conv_ln_b[l]), w_conv_proj[l].astype(BF16), w_pool_group[l].astype(BF16),
                 row(pool_scale[l]), w_pool_proj[l].astype(BF16), w_out[l].astype(BF16), row(ln2_g[l]), row(ln2_b[l]))
        hp3, cp, pp = _mixer_prompt(hp.reshape(bsz, seq, d), mix_w, alpha)
        hs3, cs, ps = _mixer_state(hs.reshape(dbsz, dseq, d), state_conv[l:l + 1], state_pool[l:l + 1], mix_w, PAST_LEN, alpha)
        hp, hs = _ffn_ln(hp3.reshape(bsz * seq, d), hs3.reshape(dbsz * dseq, d), w_ffn2_gate[l].astype(BF16),
                         w_ffn2_up[l].astype(BF16), w_ffn2_down[l].astype(BF16), row(ln3_g[l]), row(ln3_b[l]), alpha)
        conv_p.append(cp)
        conv_s.append(cs)
        pool_p.append(pp)
        pool_s.append(ps)
    cat = lambda xs: xs[0] if len(xs) == 1 else jnp.concatenate(xs, axis=0)
    return (hp.reshape(bsz, seq, d), hs.reshape(dbsz, dseq, d), cat(conv_p), cat(conv_s), cat(pool_p), cat(pool_s))
```

```python
import functools

import jax
import jax.numpy as jnp
from jax import lax
from jax.experimental import pallas as pl
from jax.experimental.pallas import tpu as pltpu

CONV_WIDTH = 31
CONV_HIST = CONV_WIDTH - 1
POOL_WINDOWS = (2, 4, 8, 16)
POOL_HIST = max(POOL_WINDOWS) - 1
LN_EPS = 1e-5
PAST_LEN = 4096

V7X_SUBLANES = 8
V7X_LANES = 128
V7X_MXU_COLS = 256
V7X_VMEM_BYTES = 64 * 1024 * 1024

TOKEN_TILE = 512
STATE_SEQS_PER_STEP = 8
CONV_ROWS = 64
CONV_PAD = 32
POOL_PAD = 16
WEIGHT_CHUNK = 256
STAGE_SLOTS = 4

BF16 = jnp.bfloat16
F32 = jnp.float32


def _layer_norm(z, g, b):
    mu = jnp.mean(z, axis=-1, keepdims=True)
    zc = z - mu
    var = jnp.mean(zc * zc, axis=-1, keepdims=True)
    return zc * lax.rsqrt(var + LN_EPS) * g + b


def _dot(a, b):
    return jnp.dot(a, b, preferred_element_type=F32)


def _resident(shape):
    zeros = (0,) * len(shape)
    return pl.BlockSpec(shape, lambda *_: zeros, pipeline_mode=pl.Buffered(1))


def _nbytes(shape, dtype):
    n = 1
    for s in shape:
        n *= s
    return n * jnp.dtype(dtype).itemsize


def _vmem_limit(estimate_bytes):
    assert estimate_bytes <= V7X_VMEM_BYTES, estimate_bytes
    return int(estimate_bytes)


def _stage_shape(w_shape, axis):
    chunk = (WEIGHT_CHUNK, w_shape[1]) if axis == 0 else (w_shape[0], WEIGHT_CHUNK)
    return (STAGE_SLOTS,) + chunk


def _cast_weight(w_hbm, w_bf16, stage, sem, axis):
    assert w_hbm.shape[axis] % WEIGHT_CHUNK == 0 and stage.shape == _stage_shape(w_hbm.shape, axis)
    n = w_hbm.shape[axis] // WEIGHT_CHUNK
    piece = lambda i: pl.ds(i * WEIGHT_CHUNK, WEIGHT_CHUNK)

    def copy(i):
        src = w_hbm.at[piece(i), :] if axis == 0 else w_hbm.at[:, piece(i)]
        return pltpu.make_async_copy(src, stage.at[i % STAGE_SLOTS], sem.at[i % STAGE_SLOTS])

    ahead = STAGE_SLOTS - 1
    for i in range(min(ahead, n)):
        copy(i).start()
    for i in range(n):
        if i + ahead < n:
            copy(i + ahead).start()
        copy(i).wait()
        if axis == 0:
            w_bf16[piece(i), :] = stage[i % STAGE_SLOTS].astype(BF16)
        else:
            w_bf16[:, piece(i)] = stage[i % STAGE_SLOTS].astype(BF16)


def _ffn_ln_kernel(xp_ref, xs_ref, wg_hbm, wu_hbm, wd_hbm, g_ref, b_ref, op_ref, os_ref,
                   wg_ref, wu_ref, wd_ref, stage_cols, stage_rows, sem, *, n_prompt_tiles, alpha):
    i = pl.program_id(0)

    @pl.when(i == 0)
    def _():
        _cast_weight(wg_hbm, wg_ref, stage_cols, sem, 1)
        _cast_weight(wu_hbm, wu_ref, stage_cols, sem, 1)
        _cast_weight(wd_hbm, wd_ref, stage_rows, sem, 0)

    is_prompt = i < n_prompt_tiles
    x = jnp.where(is_prompt, xp_ref[...], xs_ref[...])
    xb = x.astype(BF16)
    gate = _dot(xb, wg_ref[...])
    up = _dot(xb, wu_ref[...])
    act = (gate * jax.nn.sigmoid(gate) * up).astype(BF16)
    ffn = _dot(act, wd_ref[...])
    y = _layer_norm(alpha * x + 0.5 * ffn, g_ref[...], b_ref[...])

    @pl.when(is_prompt)
    def _():
        op_ref[...] = y

    @pl.when(jnp.logical_not(is_prompt))
    def _():
        os_ref[...] = y


def _ffn_ln(xp, xs, wg, wu, wd, g, b, alpha):
    n_p, d = xp.shape
    n_s = xs.shape[0]
    d_ff = wg.shape[1]
    t = TOKEN_TILE
    assert n_p % t == 0 and n_s % t == 0
    npt, nst = n_p // t, n_s // t
    tile = (t, d)
    stage_cols, stage_rows = _stage_shape(wg.shape, 1), _stage_shape(wd.shape, 0)
    scratch = [(wg.shape, BF16), (wu.shape, BF16), (wd.shape, BF16), (stage_cols, F32), (stage_rows, F32)]
    vmem = (sum(_nbytes(shape, dtype) for shape, dtype in scratch)
            + 4 * 2 * _nbytes(tile, F32)
            + 2 * _nbytes((t, d_ff), F32) + _nbytes((t, d_ff), BF16)
            + 4 * _nbytes(tile, F32))
    kern = functools.partial(_ffn_ln_kernel, n_prompt_tiles=npt, alpha=alpha)
    return pl.pallas_call(
        kern,
        grid=(npt + nst,),
        in_specs=[
            pl.BlockSpec(tile, lambda i: (jnp.minimum(i, npt - 1), 0)),
            pl.BlockSpec(tile, lambda i: (jnp.maximum(i - npt, 0), 0)),
            pl.BlockSpec(memory_space=pl.ANY), pl.BlockSpec(memory_space=pl.ANY), pl.BlockSpec(memory_space=pl.ANY),
            _resident(g.shape), _resident(b.shape),
        ],
        out_specs=[
            pl.BlockSpec(tile, lambda i: (jnp.minimum(i, npt - 1), 0)),
            pl.BlockSpec(tile, lambda i: (jnp.maximum(i - npt, 0), 0)),
        ],
        out_shape=[jax.ShapeDtypeStruct(xp.shape, F32), jax.ShapeDtypeStruct(xs.shape, F32)],
        scratch_shapes=([pltpu.VMEM(shape, dtype) for shape, dtype in scratch]
                        + [pltpu.SemaphoreType.DMA((STAGE_SLOTS,))]),
        compiler_params=pltpu.CompilerParams(dimension_semantics=("arbitrary",), vmem_limit_bytes=_vmem_limit(vmem)),
        name="ffn_ln",
    )(xp, xs, wg, wu, wd, g, b)


def _store_delayed(cbuf, glu3):
    L = glu3.shape[1]
    for b in range(V7X_SUBLANES):
        cbuf[b, :, CONV_PAD + b:CONV_PAD + b + L, :] = glu3


def _broadcast_taps(wbc, cw_ref):
    for k in range(CONV_WIDTH):
        wbc[k] = jnp.broadcast_to(cw_ref[k:k + 1, :], wbc.shape[1:])


def _conv_chunk(cbuf, wbc, cb_ref, dconv, s, r0, out_row0, conv_rows):
    groups = conv_rows // V7X_SUBLANES
    for c in range(cbuf.shape[-1] // V7X_LANES):
        lanes = slice(c * V7X_LANES, (c + 1) * V7X_LANES)
        acc = [jnp.broadcast_to(cb_ref[:, lanes], (V7X_SUBLANES, V7X_LANES))] * groups
        for b in range(V7X_SUBLANES):
            a_max = (CONV_HIST - b) // V7X_SUBLANES
            rows = {g: cbuf[b, s, pl.ds(r0 + CONV_PAD + V7X_SUBLANES * g, V7X_SUBLANES), lanes]
                    for g in range(-a_max, groups)}
            for a in range(a_max + 1):
                w = wbc[CONV_HIST - (V7X_SUBLANES * a + b), :, lanes]
                for m in range(groups):
                    acc[m] = acc[m] + rows[m - a] * w
        dconv[pl.ds(out_row0, conv_rows), lanes] = jnp.concatenate(acc, axis=0)


def _pooled(pext, L, pos1):
    pool_group = pext.shape[-1] // len(POOL_WINDOWS)
    out = []
    for gi, w in enumerate(POOL_WINDOWS):
        lanes = slice(gi * pool_group, (gi + 1) * pool_group)
        frame = pext[:, POOL_PAD:POOL_PAD + L, lanes]
        total = frame
        for j in range(1, w):
            total = total + pext[:, POOL_PAD - j:POOL_PAD - j + L, lanes]
        cnt = jnp.minimum(pos1, w).astype(F32)
        out.append(total / cnt[None] - frame)
    return jnp.concatenate(out, axis=-1)


def _merge(h, dconv, pooled_bf16, gate_piece, n_pieces, clg_ref, clb_ref, w_cp_ref, w_pg_ref, psc_ref, w_pp_ref,
           w_out_ref, l2g_ref, l2b_ref, alpha):
    conv_act = _layer_norm(dconv[...], clg_ref[...], clb_ref[...])
    conv_act = conv_act * jax.nn.sigmoid(conv_act)
    branch_conv = _dot(conv_act.astype(BF16), w_cp_ref[...])
    pool_group = w_pg_ref.shape[-1]
    groups = []
    for gi in range(len(POOL_WINDOWS)):
        lanes = slice(gi * pool_group, (gi + 1) * pool_group)
        groups.append(_dot(pooled_bf16[:, lanes], w_pg_ref[gi]) * psc_ref[:, lanes])
    branch_pool = _dot(jnp.concatenate(groups, axis=-1).astype(BF16), w_pp_ref[...])
    half = n_pieces // 2
    gcols = branch_conv.shape[-1] // half
    merged = jnp.concatenate(
        [gate_piece(p) * branch_conv[:, p * gcols:(p + 1) * gcols]
         + gate_piece(half + p) * branch_pool[:, p * gcols:(p + 1) * gcols] for p in range(half)], axis=-1)
    mixed = _dot(merged.astype(BF16), w_out_ref[...])
    return _layer_norm(alpha * h + mixed, l2g_ref[...], l2b_ref[...])


def _mixer_prompt_kernel(h_ref, w_in_ref, w_gate_ref, b_gate_ref, cw_ref, cb_ref, clg_ref, clb_ref, w_cp_ref,
                         w_pg_ref, psc_ref, w_pp_ref, w_out_ref, l2g_ref, l2b_ref,
                         h2_ref, nc_ref, np_ref, cbuf, pext, wbc, dconv, *, alpha):
    t_idx = pl.program_id(1)
    L, d_model = h_ref.shape[1:]
    d_conv = cbuf.shape[-1]
    d_pool = pext.shape[-1]

    @pl.when(t_idx == 0)
    def _():
        cbuf[...] = jnp.zeros(cbuf.shape, F32)
        pext[:, :POOL_PAD, :] = jnp.zeros((1, POOL_PAD, d_pool), F32)
        _broadcast_taps(wbc, cw_ref)

    @pl.when(t_idx > 0)
    def _():
        keep = CONV_PAD + V7X_SUBLANES
        cbuf[:, :, :keep, :] = cbuf[:, :, L:L + keep, :]
        pext[:, :POOL_PAD, :] = pext[:, L:L + POOL_PAD, :]

    h = h_ref[0]
    hb = h.astype(BF16)
    u = _dot(hb, w_in_ref[...])
    gates = jax.nn.sigmoid(_dot(hb, w_gate_ref[...]) + b_gate_ref[...])
    glu = u[:, :d_conv] * jax.nn.sigmoid(u[:, d_conv:2 * d_conv])
    _store_delayed(cbuf, glu.reshape(1, L, d_conv))
    pext[:, POOL_PAD:, :] = u[:, 2 * d_conv:].reshape(1, L, d_pool)
    row = lax.broadcasted_iota(jnp.int32, (L, d_pool // len(POOL_WINDOWS)), 0)
    pooled = _pooled(pext, L, row + 1 + t_idx * L).reshape(L, d_pool).astype(BF16)

    def conv_trip(i, carry):
        r0 = pl.multiple_of(i * CONV_ROWS, CONV_ROWS)
        _conv_chunk(cbuf, wbc, cb_ref, dconv, 0, r0, r0, CONV_ROWS)
        return carry

    lax.fori_loop(0, L // CONV_ROWS, conv_trip, 0)

    n_pieces = 2 * d_model // V7X_MXU_COLS
    gate_piece = lambda p: gates[:, p * V7X_MXU_COLS:(p + 1) * V7X_MXU_COLS]
    h2_ref[0] = _merge(h, dconv, pooled, gate_piece, n_pieces, clg_ref, clb_ref, w_cp_ref, w_pg_ref, psc_ref, w_pp_ref,
                       w_out_ref, l2g_ref, l2b_ref, alpha)

    @pl.when(t_idx == pl.num_programs(1) - 1)
    def _():
        nc_ref[0] = cbuf[0, :, CONV_PAD + L - CONV_HIST:CONV_PAD + L, :]
        np_ref[0] = pext[:, POOL_PAD + L - POOL_HIST:POOL_PAD + L, :]


def _mixer_prompt(h, weights, alpha):
    (w_in, w_gate, b_gate, conv_w, conv_b, cln_g, cln_b, w_cp, w_pg, pscale, w_pp, w_out, l2g, l2b) = weights
    bsz, lseq, d = h.shape
    d_conv = conv_w.shape[1]
    d_pool = w_pp.shape[0]
    L = TOKEN_TILE
    assert lseq % L == 0 and L % CONV_ROWS == 0 and L >= CONV_HIST and L >= POOL_HIST
    hblock = (1, L, d)
    hmap = lambda b, t: (b, t, 0)
    state_map = lambda b, t: (0, b, 0, 0)
    scratch = [((V7X_SUBLANES, 1, CONV_PAD + L + V7X_SUBLANES, d_conv), F32), ((1, POOL_PAD + L, d_pool), F32),
               ((CONV_WIDTH, V7X_SUBLANES, d_conv), F32), ((L, d_conv), F32)]
    w_list = [w_in, w_gate, b_gate, conv_w, conv_b, cln_g, cln_b, w_cp, w_pg, pscale, w_pp, w_out, l2g, l2b]
    vmem = (sum(_nbytes(w.shape, w.dtype) for w in w_list)
            + 2 * 2 * _nbytes(hblock, F32)
            + 2 * 2 * (_nbytes((1, CONV_HIST, d_conv), F32) + _nbytes((1, POOL_HIST, d_pool), F32))
            + sum(_nbytes(shape, dtype) for shape, dtype in scratch)
            + _nbytes((L, w_in.shape[1]), F32) + _nbytes((L, 2 * d), F32)
            + 8 * _nbytes((L, d), F32))
    kern = functools.partial(_mixer_prompt_kernel, alpha=alpha)
    return pl.pallas_call(
        kern,
        grid=(bsz, lseq // L),
        in_specs=[pl.BlockSpec(hblock, hmap)] + [_resident(w.shape) for w in w_list],
        out_specs=[pl.BlockSpec(hblock, hmap), pl.BlockSpec((1, 1, CONV_HIST, d_conv), state_map),
                   pl.BlockSpec((1, 1, POOL_HIST, d_pool), state_map)],
        out_shape=[jax.ShapeDtypeStruct(h.shape, F32), jax.ShapeDtypeStruct((1, bsz, CONV_HIST, d_conv), F32),
                   jax.ShapeDtypeStruct((1, bsz, POOL_HIST, d_pool), F32)],
        scratch_shapes=[pltpu.VMEM(shape, dtype) for shape, dtype in scratch],
        compiler_params=pltpu.CompilerParams(dimension_semantics=("arbitrary", "arbitrary"),
                                             vmem_limit_bytes=_vmem_limit(vmem)),
        name="mixer_prompt",
    )(h, *w_list)


def _mixer_state_kernel(h_ref, cst_ref, pst_ref, w_in_ref, w_gate_ref, b_gate_ref, cw_ref, cb_ref, clg_ref, clb_ref,
                        w_cp_ref, w_pg_ref, psc_ref, w_pp_ref, w_out_ref, l2g_ref, l2b_ref,
                        h2_ref, nc_ref, np_ref, cbuf, pext, wbc, dconv, *, pos0, alpha):
    S, L, d_model = h_ref.shape
    N = S * L
    d_conv = cbuf.shape[-1]
    d_pool = pext.shape[-1]

    h = h_ref[...].reshape(N, d_model)
    hb = h.astype(BF16)
    u = _dot(hb, w_in_ref[...])
    gates = jax.nn.sigmoid(_dot(hb, w_gate_ref[...]) + b_gate_ref[...])
    glu = u[:, :d_conv] * jax.nn.sigmoid(u[:, d_conv:2 * d_conv])

    for b in range(V7X_SUBLANES):
        lo = CONV_PAD - CONV_HIST + b
        cbuf[b, :, lo:lo + CONV_HIST, :] = cst_ref[0]
    pext[:, POOL_PAD - POOL_HIST:POOL_PAD, :] = pst_ref[0]
    _broadcast_taps(wbc, cw_ref)
    _store_delayed(cbuf, glu.reshape(S, L, d_conv))
    pext[:, POOL_PAD:, :] = u[:, 2 * d_conv:].reshape(S, L, d_pool)
    row = lax.broadcasted_iota(jnp.int32, (L, d_pool // len(POOL_WINDOWS)), 0)
    pooled = _pooled(pext, L, row + (pos0 + 1)).reshape(N, d_pool).astype(BF16)

    def conv_trip(s, carry):
        _conv_chunk(cbuf, wbc, cb_ref, dconv, s, 0, pl.multiple_of(s * L, L), L)
        return carry

    lax.fori_loop(0, S, conv_trip, 0)

    n_pieces = 2 * d_model // V7X_MXU_COLS
    gate_piece = lambda p: gates[:, p * V7X_MXU_COLS:(p + 1) * V7X_MXU_COLS]
    h2 = _merge(h, dconv, pooled, gate_piece, n_pieces, clg_ref, clb_ref, w_cp_ref, w_pg_ref, psc_ref, w_pp_ref,
                w_out_ref, l2g_ref, l2b_ref, alpha)
    h2_ref[...] = h2.reshape(S, L, d_model)
    nc_ref[0] = cbuf[0, :, CONV_PAD + L - CONV_HIST:CONV_PAD + L, :]
    np_ref[0] = pext[:, POOL_PAD + L - POOL_HIST:POOL_PAD + L, :]


def _mixer_state(h, conv_state, pool_state, weights, pos0, alpha):
    (w_in, w_gate, b_gate, conv_w, conv_b, cln_g, cln_b, w_cp, w_pg, pscale, w_pp, w_out, l2g, l2b) = weights
    bsz, L, d = h.shape
    d_conv = conv_w.shape[1]
    d_pool = w_pp.shape[0]
    S = STATE_SEQS_PER_STEP
    N = S * L
    assert bsz % S == 0 and L % V7X_SUBLANES == 0 and L >= CONV_HIST and L >= POOL_HIST

    hblock = (S, L, d)
    hmap = lambda i: (i, 0, 0)
    cstate_spec = pl.BlockSpec((1, S, CONV_HIST, d_conv), lambda i: (0, i, 0, 0))
    pstate_spec = pl.BlockSpec((1, S, POOL_HIST, d_pool), lambda i: (0, i, 0, 0))
    scratch = [((V7X_SUBLANES, S, CONV_PAD + L + V7X_SUBLANES, d_conv), F32), ((S, POOL_PAD + L, d_pool), F32),
               ((CONV_WIDTH, V7X_SUBLANES, d_conv), F32), ((N, d_conv), F32)]
    w_list = [w_in, w_gate, b_gate, conv_w, conv_b, cln_g, cln_b, w_cp, w_pg, pscale, w_pp, w_out, l2g, l2b]
    vmem = (sum(_nbytes(w.shape, w.dtype) for w in w_list)
            + 2 * 2 * _nbytes(hblock, F32)
            + 2 * 2 * (_nbytes((S, CONV_HIST, d_conv), F32) + _nbytes((S, POOL_HIST, d_pool), F32))
            + sum(_nbytes(shape, dtype) for shape, dtype in scratch)
            + _nbytes((N, w_in.shape[1]), F32) + _nbytes((N, 2 * d), F32)
            + 8 * _nbytes((N, d), F32))
    kern = functools.partial(_mixer_state_kernel, pos0=pos0, alpha=alpha)
    return pl.pallas_call(
        kern,
        grid=(bsz // S,),
        in_specs=[pl.BlockSpec(hblock, hmap), cstate_spec, pstate_spec] + [_resident(w.shape) for w in w_list],
        out_specs=[pl.BlockSpec(hblock, hmap), cstate_spec, pstate_spec],
        out_shape=[jax.ShapeDtypeStruct(h.shape, F32), jax.ShapeDtypeStruct(conv_state.shape, F32),
                   jax.ShapeDtypeStruct(pool_state.shape, F32)],
        scratch_shapes=[pltpu.VMEM(shape, dtype) for shape, dtype in scratch],
        compiler_params=pltpu.CompilerParams(dimension_semantics=("arbitrary",), vmem_limit_bytes=_vmem_limit(vmem)),
        name="mixer_state",
    )(h, conv_state, pool_state, *w_list)


def kernel(x_prompt, x_sample, state_conv, state_pool, w_ffn1_gate, w_ffn1_up, w_ffn1_down, ln1_g, ln1_b, w_in, w_gate, b_gate, conv_w, conv_b, conv_ln_g, conv_ln_b, w_conv_proj, w_pool_group, pool_scale, w_pool_proj, w_out, ln2_g, ln2_b, w_ffn2_gate, w_ffn2_up, w_ffn2_down, ln3_g, ln3_b):
    depth = w_in.shape[0]
    bsz, seq, d = x_prompt.shape
    dbsz, dseq, _ = x_sample.shape
    alpha = (2.0 * depth) ** 0.25
    row = lambda v: v.reshape(1, -1)

    hp = x_prompt.reshape(bsz * seq, d)
    hs = x_sample.reshape(dbsz * dseq, d)
    conv_p, conv_s, pool_p, pool_s = [], [], [], []
    for l in range(depth):
        hp, hs = _ffn_ln(hp, hs, w_ffn1_gate[l], w_ffn1_up[l], w_ffn1_down[l], row(ln1_g[l]), row(ln1_b[l]), alpha)
        mix_w = (w_in[l].astype(BF16), w_gate[l].astype(BF16), row(b_gate[l]), conv_w[l], row(conv_b[l]),
                 row(conv_ln_g[l]), row(conv_ln_b[l]), w_conv_proj[l].astype(BF16), w_pool_group[l].astype(BF16),
                 row(pool_scale[l]), w_pool_proj[l].astype(BF16), w_out[l].astype(BF16), row(ln2_g[l]), row(ln2_b[l]))
        hp3, cp, pp = _mixer_prompt(hp.reshape(bsz, seq, d), mix_w, alpha)
        hs3, cs, ps = _mixer_state(hs.reshape(dbsz, dseq, d), state_conv[l:l + 1], state_pool[l:l + 1], mix_w, PAST_LEN, alpha)
        hp, hs = _ffn_ln(hp3.reshape(bsz * seq, d), hs3.reshape(dbsz * dseq, d), w_ffn2_gate[l], w_ffn2_up[l],
                         w_ffn2_down[l], row(ln3_g[l]), row(ln3_b[l]), alpha)
        conv_p.append(cp)
        conv_s.append(cs)
        pool_p.append(pp)
        pool_s.append(ps)
    cat = lambda xs: xs[0] if len(xs) == 1 else jnp.concatenate(xs, axis=0)
    return (hp.reshape(bsz, seq, d), hs.reshape(dbsz, dseq, d), cat(conv_p), cat(conv_s), cat(pool_p), cat(pool_s))
```

```python
import functools

import jax
import jax.numpy as jnp
from jax import lax
from jax.experimental import pallas as pl
from jax.experimental.pallas import tpu as pltpu

CONV_WIDTH = 31
CONV_HIST = CONV_WIDTH - 1
POOL_WINDOWS = (2, 4, 8, 16)
POOL_HIST = max(POOL_WINDOWS) - 1
LN_EPS = 1e-5
PAST_LEN = 4096

V7X_SUBLANES = 8
V7X_LANES = 128
V7X_MXU_COLS = 256
V7X_VMEM_BYTES = 64 * 1024 * 1024

TOKEN_TILE = 512
STATE_SEQS_PER_STEP = 8
CONV_ROWS = 64
CONV_PAD = 32
POOL_PAD = 16
WEIGHT_CHUNK = 256
STAGE_SLOTS = 4

BF16 = jnp.bfloat16
F32 = jnp.float32


def _layer_norm(z, g, b):
    mu = jnp.mean(z, axis=-1, keepdims=True)
    zc = z - mu
    var = jnp.mean(zc * zc, axis=-1, keepdims=True)
    return zc * lax.rsqrt(var + LN_EPS) * g + b


def _dot(a, b):
    return jnp.dot(a, b, preferred_element_type=F32)


def _resident(shape):
    zeros = (0,) * len(shape)
    return pl.BlockSpec(shape, lambda *_: zeros, pipeline_mode=pl.Buffered(1))


def _nbytes(shape, dtype):
    n = 1
    for s in shape:
        n *= s
    return n * jnp.dtype(dtype).itemsize


def _vmem_limit(estimate_bytes):
    assert estimate_bytes <= V7X_VMEM_BYTES, estimate_bytes
    return int(estimate_bytes)


def _stage_shape(w_shape, axis):
    chunk = (WEIGHT_CHUNK, w_shape[1]) if axis == 0 else (w_shape[0], WEIGHT_CHUNK)
    return (STAGE_SLOTS,) + chunk


def _cast_weight(w_hbm, w_bf16, stage, sem, axis):
    assert w_hbm.shape[axis] % WEIGHT_CHUNK == 0 and stage.shape == _stage_shape(w_hbm.shape, axis)
    n = w_hbm.shape[axis] // WEIGHT_CHUNK
    piece = lambda i: pl.ds(i * WEIGHT_CHUNK, WEIGHT_CHUNK)

    def copy(i):
        src = w_hbm.at[piece(i), :] if axis == 0 else w_hbm.at[:, piece(i)]
        return pltpu.make_async_copy(src, stage.at[i % STAGE_SLOTS], sem.at[i % STAGE_SLOTS])

    ahead = STAGE_SLOTS - 1
    for i in range(min(ahead, n)):
        copy(i).start()
    for i in range(n):
        if i + ahead < n:
            copy(i + ahead).start()
        copy(i).wait()
        if axis == 0:
            w_bf16[piece(i), :] = stage[i % STAGE_SLOTS].astype(BF16)
        else:
            w_bf16[:, piece(i)] = stage[i % STAGE_SLOTS].astype(BF16)


def _ffn_ln_kernel(xp_ref, xs_ref, wg_hbm, wu_hbm, wd_hbm, g_ref, b_ref, op_ref, os_ref,
                   wg_ref, wu_ref, wd_ref, stage_cols, stage_rows, sem, *, n_prompt_tiles, alpha):
    i = pl.program_id(0)

    @pl.when(i == 0)
    def _():
        _cast_weight(wg_hbm, wg_ref, stage_cols, sem, 1)
        _cast_weight(wu_hbm, wu_ref, stage_cols, sem, 1)
        _cast_weight(wd_hbm, wd_ref, stage_rows, sem, 0)

    is_prompt = i < n_prompt_tiles
    x = jnp.where(is_prompt, xp_ref[...], xs_ref[...])
    xb = x.astype(BF16)
    gate = _dot(xb, wg_ref[...])
    up = _dot(xb, wu_ref[...])
    act = (gate * jax.nn.sigmoid(gate) * up).astype(BF16)
    ffn = _dot(act, wd_ref[...])
    y = _layer_norm(alpha * x + 0.5 * ffn, g_ref[...], b_ref[...])

    @pl.when(is_prompt)
    def _():
        op_ref[...] = y

    @pl.when(jnp.logical_not(is_prompt))
    def _():
        os_ref[...] = y


def _ffn_ln(xp, xs, wg, wu, wd, g, b, alpha):
    n_p, d = xp.shape
    n_s = xs.shape[0]
    d_ff = wg.shape[1]
    t = TOKEN_TILE
    assert n_p % t == 0 and n_s % t == 0
    npt, nst = n_p // t, n_s // t
    tile = (t, d)
    stage_cols, stage_rows = _stage_shape(wg.shape, 1), _stage_shape(wd.shape, 0)
    scratch = [(wg.shape, BF16), (wu.shape, BF16), (wd.shape, BF16), (stage_cols, F32), (stage_rows, F32)]
    vmem = (sum(_nbytes(shape, dtype) for shape, dtype in scratch)
            + 4 * 2 * _nbytes(tile, F32)
            + 2 * _nbytes((t, d_ff), F32) + _nbytes((t, d_ff), BF16)
            + 4 * _nbytes(tile, F32))
    kern = functools.partial(_ffn_ln_kernel, n_prompt_tiles=npt, alpha=alpha)
    return pl.pallas_call(
        kern,
        grid=(npt + nst,),
        in_specs=[
            pl.BlockSpec(tile, lambda i: (jnp.minimum(i, npt - 1), 0)),
            pl.BlockSpec(tile, lambda i: (jnp.maximum(i - npt, 0), 0)),
            pl.BlockSpec(memory_space=pl.ANY), pl.BlockSpec(memory_space=pl.ANY), pl.BlockSpec(memory_space=pl.ANY),
            _resident(g.shape), _resident(b.shape),
        ],
        out_specs=[
            pl.BlockSpec(tile, lambda i: (jnp.minimum(i, npt - 1), 0)),
            pl.BlockSpec(tile, lambda i: (jnp.maximum(i - npt, 0), 0)),
        ],
        out_shape=[jax.ShapeDtypeStruct(xp.shape, F32), jax.ShapeDtypeStruct(xs.shape, F32)],
        scratch_shapes=([pltpu.VMEM(shape, dtype) for shape, dtype in scratch]
                        + [pltpu.SemaphoreType.DMA((STAGE_SLOTS,))]),
        compiler_params=pltpu.CompilerParams(dimension_semantics=("arbitrary",), vmem_limit_bytes=_vmem_limit(vmem)),
        name="ffn_ln",
    )(xp, xs, wg, wu, wd, g, b)


def _store_delayed(cbuf, glu3):
    L = glu3.shape[1]
    for b in range(V7X_SUBLANES):
        cbuf[b, :, CONV_PAD + b:CONV_PAD + b + L, :] = glu3


def _broadcast_taps(wbc, cw_ref):
    for k in range(CONV_WIDTH):
        wbc[k] = jnp.broadcast_to(cw_ref[k:k + 1, :], wbc.shape[1:])


def _conv_chunk(cbuf, wbc, cb_ref, dconv, s, r0, out_row0, conv_rows):
    groups = conv_rows // V7X_SUBLANES
    for c in range(cbuf.shape[-1] // V7X_LANES):
        lanes = slice(c * V7X_LANES, (c + 1) * V7X_LANES)
        acc = [jnp.broadcast_to(cb_ref[:, lanes], (V7X_SUBLANES, V7X_LANES))] * groups
        for b in range(V7X_SUBLANES):
            a_max = (CONV_HIST - b) // V7X_SUBLANES
            rows = {g: cbuf[b, s, pl.ds(r0 + CONV_PAD + V7X_SUBLANES * g, V7X_SUBLANES), lanes]
                    for g in range(-a_max, groups)}
            for a in range(a_max + 1):
                w = wbc[CONV_HIST - (V7X_SUBLANES * a + b), :, lanes]
                for m in range(groups):
                    acc[m] = acc[m] + rows[m - a] * w
        dconv[pl.ds(out_row0, conv_rows), lanes] = jnp.concatenate(acc, axis=0)


def _pooled(pext, L, pos1):
    pool_group = pext.shape[-1] // len(POOL_WINDOWS)
    out = []
    for gi, w in enumerate(POOL_WINDOWS):
        lanes = slice(gi * pool_group, (gi + 1) * pool_group)
        assert w & (w - 1) == 0 and w - 1 <= POOL_PAD
        total = pext[:, :, lanes]
        k = 1
        while k < w:
            total = total + jnp.concatenate([total[:, :k], total[:, :-k]], axis=1)
            k *= 2
        frame = pext[:, POOL_PAD:POOL_PAD + L, lanes]
        cnt = jnp.minimum(pos1, w).astype(F32)
        out.append(total[:, POOL_PAD:POOL_PAD + L] / cnt[None] - frame)
    return jnp.concatenate(out, axis=-1)


def _merge(h, dconv, pooled_bf16, gate_piece, n_pieces, clg_ref, clb_ref, w_cp_ref, w_pg_ref, psc_ref, w_pp_ref,
           w_out_ref, l2g_ref, l2b_ref, alpha):
    conv_act = _layer_norm(dconv[...], clg_ref[...], clb_ref[...])
    conv_act = conv_act * jax.nn.sigmoid(conv_act)
    branch_conv = _dot(conv_act.astype(BF16), w_cp_ref[...])
    pool_group = w_pg_ref.shape[-1]
    groups = []
    for gi in range(len(POOL_WINDOWS)):
        lanes = slice(gi * pool_group, (gi + 1) * pool_group)
        groups.append(_dot(pooled_bf16[:, lanes], w_pg_ref[gi]) * psc_ref[:, lanes])
    branch_pool = _dot(jnp.concatenate(groups, axis=-1).astype(BF16), w_pp_ref[...])
    half = n_pieces // 2
    gcols = branch_conv.shape[-1] // half
    merged = jnp.concatenate(
        [gate_piece(p) * branch_conv[:, p * gcols:(p + 1) * gcols]
         + gate_piece(half + p) * branch_pool[:, p * gcols:(p + 1) * gcols] for p in range(half)], axis=-1)
    mixed = _dot(merged.astype(BF16), w_out_ref[...])
    return _layer_norm(alpha * h + mixed, l2g_ref[...], l2b_ref[...])


def _mixer_prompt_kernel(h_ref, w_in_ref, w_gate_ref, b_gate_ref, cw_ref, cb_ref, clg_ref, clb_ref, w_cp_ref,
                         w_pg_ref, psc_ref, w_pp_ref, w_out_ref, l2g_ref, l2b_ref,
                         h2_ref, nc_ref, np_ref, cbuf, pext, wbc, dconv, *, alpha):
    t_idx = pl.program_id(1)
    L, d_model = h_ref.shape[1:]
    d_conv = cbuf.shape[-1]
    d_pool = pext.shape[-1]

    @pl.when(t_idx == 0)
    def _():
        cbuf[...] = jnp.zeros(cbuf.shape, F32)
        pext[:, :POOL_PAD, :] = jnp.zeros((1, POOL_PAD, d_pool), F32)
        _broadcast_taps(wbc, cw_ref)

    @pl.when(t_idx > 0)
    def _():
        keep = CONV_PAD + V7X_SUBLANES
        cbuf[:, :, :keep, :] = cbuf[:, :, L:L + keep, :]
        pext[:, :POOL_PAD, :] = pext[:, L:L + POOL_PAD, :]

    h = h_ref[0]
    hb = h.astype(BF16)
    u = _dot(hb, w_in_ref[...])
    gates = jax.nn.sigmoid(_dot(hb, w_gate_ref[...]) + b_gate_ref[...])
    glu = u[:, :d_conv] * jax.nn.sigmoid(u[:, d_conv:2 * d_conv])
    _store_delayed(cbuf, glu.reshape(1, L, d_conv))
    pext[:, POOL_PAD:, :] = u[:, 2 * d_conv:].reshape(1, L, d_pool)
    row = lax.broadcasted_iota(jnp.int32, (L, d_pool // len(POOL_WINDOWS)), 0)
    pooled = _pooled(pext, L, row + 1 + t_idx * L).reshape(L, d_pool).astype(BF16)

    def conv_trip(i, carry):
        r0 = pl.multiple_of(i * CONV_ROWS, CONV_ROWS)
        _conv_chunk(cbuf, wbc, cb_ref, dconv, 0, r0, r0, CONV_ROWS)
        return carry

    lax.fori_loop(0, L // CONV_ROWS, conv_trip, 0)

    n_pieces = 2 * d_model // V7X_MXU_COLS
    gate_piece = lambda p: gates[:, p * V7X_MXU_COLS:(p + 1) * V7X_MXU_COLS]
    h2_ref[0] = _merge(h, dconv, pooled, gate_piece, n_pieces, clg_ref, clb_ref, w_cp_ref, w_pg_ref, psc_ref, w_pp_ref,
                       w_out_ref, l2g_ref, l2b_ref, alpha)

    @pl.when(t_idx == pl.num_programs(1) - 1)
    def _():
        nc_ref[0] = cbuf[0, :, CONV_PAD + L - CONV_HIST:CONV_PAD + L, :]
        np_ref[0] = pext[:, POOL_PAD + L - POOL_HIST:POOL_PAD + L, :]


def _mixer_prompt(h, weights, alpha):
    (w_in, w_gate, b_gate, conv_w, conv_b, cln_g, cln_b, w_cp, w_pg, pscale, w_pp, w_out, l2g, l2b) = weights
    bsz, lseq, d = h.shape
    d_conv = conv_w.shape[1]
    d_pool = w_pp.shape[0]
    L = TOKEN_TILE
    assert lseq % L == 0 and L % CONV_ROWS == 0 and L >= CONV_HIST and L >= POOL_HIST
    hblock = (1, L, d)
    hmap = lambda b, t: (b, t, 0)
    state_map = lambda b, t: (0, b, 0, 0)
    scratch = [((V7X_SUBLANES, 1, CONV_PAD + L + V7X_SUBLANES, d_conv), F32), ((1, POOL_PAD + L, d_pool), F32),
               ((CONV_WIDTH, V7X_SUBLANES, d_conv), F32), ((L, d_conv), F32)]
    w_list = [w_in, w_gate, b_gate, conv_w, conv_b, cln_g, cln_b, w_cp, w_pg, pscale, w_pp, w_out, l2g, l2b]
    vmem = (sum(_nbytes(w.shape, w.dtype) for w in w_list)
            + 2 * 2 * _nbytes(hblock, F32)
            + 2 * 2 * (_nbytes((1, CONV_HIST, d_conv), F32) + _nbytes((1, POOL_HIST, d_pool), F32))
            + sum(_nbytes(shape, dtype) for shape, dtype in scratch)
            + _nbytes((L, w_in.shape[1]), F32) + _nbytes((L, 2 * d), F32)
            + 8 * _nbytes((L, d), F32))
    kern = functools.partial(_mixer_prompt_kernel, alpha=alpha)
    return pl.pallas_call(
        kern,
        grid=(bsz, lseq // L),
        in_specs=[pl.BlockSpec(hblock, hmap)] + [_resident(w.shape) for w in w_list],
        out_specs=[pl.BlockSpec(hblock, hmap), pl.BlockSpec((1, 1, CONV_HIST, d_conv), state_map),
                   pl.BlockSpec((1, 1, POOL_HIST, d_pool), state_map)],
        out_shape=[jax.ShapeDtypeStruct(h.shape, F32), jax.ShapeDtypeStruct((1, bsz, CONV_HIST, d_conv), F32),
                   jax.ShapeDtypeStruct((1, bsz, POOL_HIST, d_pool), F32)],
        scratch_shapes=[pltpu.VMEM(shape, dtype) for shape, dtype in scratch],
        compiler_params=pltpu.CompilerParams(dimension_semantics=("arbitrary", "arbitrary"),
                                             vmem_limit_bytes=_vmem_limit(vmem)),
        name="mixer_prompt",
    )(h, *w_list)


def _mixer_state_kernel(h_ref, cst_ref, pst_ref, w_in_ref, w_gate_ref, b_gate_ref, cw_ref, cb_ref, clg_ref, clb_ref,
                        w_cp_ref, w_pg_ref, psc_ref, w_pp_ref, w_out_ref, l2g_ref, l2b_ref,
                        h2_ref, nc_ref, np_ref, cbuf, pext, wbc, dconv, *, pos0, alpha):
    S, L, d_model = h_ref.shape
    N = S * L
    d_conv = cbuf.shape[-1]
    d_pool = pext.shape[-1]

    h = h_ref[...].reshape(N, d_model)
    hb = h.astype(BF16)
    u = _dot(hb, w_in_ref[...])
    gates = jax.nn.sigmoid(_dot(hb, w_gate_ref[...]) + b_gate_ref[...])
    glu = u[:, :d_conv] * jax.nn.sigmoid(u[:, d_conv:2 * d_conv])

    for b in range(V7X_SUBLANES):
        lo = CONV_PAD - CONV_HIST + b
        cbuf[b, :, lo:lo + CONV_HIST, :] = cst_ref[0]
    pext[:, :POOL_PAD - POOL_HIST, :] = jnp.zeros((S, POOL_PAD - POOL_HIST, d_pool), F32)
    pext[:, POOL_PAD - POOL_HIST:POOL_PAD, :] = pst_ref[0]
    _broadcast_taps(wbc, cw_ref)
    _store_delayed(cbuf, glu.reshape(S, L, d_conv))
    pext[:, POOL_PAD:, :] = u[:, 2 * d_conv:].reshape(S, L, d_pool)
    row = lax.broadcasted_iota(jnp.int32, (L, d_pool // len(POOL_WINDOWS)), 0)
    pooled = _pooled(pext, L, row + (pos0 + 1)).reshape(N, d_pool).astype(BF16)

    def conv_trip(s, carry):
        _conv_chunk(cbuf, wbc, cb_ref, dconv, s, 0, pl.multiple_of(s * L, L), L)
        return carry

    lax.fori_loop(0, S, conv_trip, 0)

    n_pieces = 2 * d_model // V7X_MXU_COLS
    gate_piece = lambda p: gates[:, p * V7X_MXU_COLS:(p + 1) * V7X_MXU_COLS]
    h2 = _merge(h, dconv, pooled, gate_piece, n_pieces, clg_ref, clb_ref, w_cp_ref, w_pg_ref, psc_ref, w_pp_ref,
                w_out_ref, l2g_ref, l2b_ref, alpha)
    h2_ref[...] = h2.reshape(S, L, d_model)
    nc_ref[0] = cbuf[0, :, CONV_PAD + L - CONV_HIST:CONV_PAD + L, :]
    np_ref[0] = pext[:, POOL_PAD + L - POOL_HIST:POOL_PAD + L, :]


def _mixer_state(h, conv_state, pool_state, weights, pos0, alpha):
    (w_in, w_gate, b_gate, conv_w, conv_b, cln_g, cln_b, w_cp, w_pg, pscale, w_pp, w_out, l2g, l2b) = weights
    bsz, L, d = h.shape
    d_conv = conv_w.shape[1]
    d_pool = w_pp.shape[0]
    S = STATE_SEQS_PER_STEP
    N = S * L
    assert bsz % S == 0 and L % V7X_SUBLANES == 0 and L >= CONV_HIST and L >= POOL_HIST

    hblock = (S, L, d)
    hmap = lambda i: (i, 0, 0)
    cstate_spec = pl.BlockSpec((1, S, CONV_HIST, d_conv), lambda i: (0, i, 0, 0))
    pstate_spec = pl.BlockSpec((1, S, POOL_HIST, d_pool), lambda i: (0, i, 0, 0))
    scratch = [((V7X_SUBLANES, S, CONV_PAD + L + V7X_SUBLANES, d_conv), F32), ((S, POOL_PAD + L, d_pool), F32),
               ((CONV_WIDTH, V7X_SUBLANES, d_conv), F32), ((N, d_conv), F32)]
    w_list = [w_in, w_gate, b_gate, conv_w, conv_b, cln_g, cln_b, w_cp, w_pg, pscale, w_pp, w_out, l2g, l2b]
    vmem = (sum(_nbytes(w.shape, w.dtype) for w in w_list)
            + 2 * 2 * _nbytes(hblock, F32)
            + 2 * 2 * (_nbytes((S, CONV_HIST, d_conv), F32) + _nbytes((S, POOL_HIST, d_pool), F32))
            + sum(_nbytes(shape, dtype) for shape, dtype in scratch)
            + _nbytes((N, w_in.shape[1]), F32) + _nbytes((N, 2 * d), F32)
            + 8 * _nbytes((N, d), F32))
    kern = functools.partial(_mixer_state_kernel, pos0=pos0, alpha=alpha)
    return pl.pallas_call(
        kern,
        grid=(bsz // S,),
        in_specs=[pl.BlockSpec(hblock, hmap), cstate_spec, pstate_spec] + [_resident(w.shape) for w in w_list],
        out_specs=[pl.BlockSpec(hblock, hmap), cstate_spec, pstate_spec],
        out_shape=[jax.ShapeDtypeStruct(h.shape, F32), jax.ShapeDtypeStruct(conv_state.shape, F32),
                   jax.ShapeDtypeStruct(pool_state.shape, F32)],
        scratch_shapes=[pltpu.VMEM(shape, dtype) for shape, dtype in scratch],
        compiler_params=pltpu.CompilerParams(dimension_semantics=("arbitrary",), vmem_limit_bytes=_vmem_limit(vmem)),
        name="mixer_state",
    )(h, conv_state, pool_state, *w_list)


def kernel(x_prompt, x_sample, state_conv, state_pool, w_ffn1_gate, w_ffn1_up, w_ffn1_down, ln1_g, ln1_b, w_in, w_gate, b_gate, conv_w, conv_b, conv_ln_g, conv_ln_b, w_conv_proj, w_pool_group, pool_scale, w_pool_proj, w_out, ln2_g, ln2_b, w_ffn2_gate, w_ffn2_up, w_ffn2_down, ln3_g, ln3_b):
    depth = w_in.shape[0]
    bsz, seq, d = x_prompt.shape
    dbsz, dseq, _ = x_sample.shape
    alpha = (2.0 * depth) ** 0.25
    row = lambda v: v.reshape(1, -1)

    hp = x_prompt.reshape(bsz * seq, d)
    hs = x_sample.reshape(dbsz * dseq, d)
    conv_p, conv_s, pool_p, pool_s = [], [], [], []
    for l in range(depth):
        hp, hs = _ffn_ln(hp, hs, w_ffn1_gate[l], w_ffn1_up[l], w_ffn1_down[l], row(ln1_g[l]), row(ln1_b[l]), alpha)
        mix_w = (w_in[l].astype(BF16), w_gate[l].astype(BF16), row(b_gate[l]), conv_w[l], row(conv_b[l]),
                 row(conv_ln_g[l]), row(conv_ln_b[l]), w_conv_proj[l].astype(BF16), w_pool_group[l].astype(BF16),
                 row(pool_scale[l]), w_pool_proj[l].astype(BF16), w_out[l].astype(BF16), row(ln2_g[l]), row(ln2_b[l]))
        hp3, cp, pp = _mixer_prompt(hp.reshape(bsz, seq, d), mix_w, alpha)
        hs3, cs, ps = _mixer_state(hs.reshape(dbsz, dseq, d), state_conv[l:l + 1], state_pool[l:l + 1], mix_w, PAST_LEN, alpha)
        hp, hs = _ffn_ln(hp3.reshape(bsz * seq, d), hs3.reshape(dbsz * dseq, d), w_ffn2_gate[l], w_ffn2_up[l],
                         w_ffn2_down[l], row(ln3_g[l]), row(ln3_b[l]), alpha)
        conv_p.append(cp)
        conv_s.append(cs)
        pool_p.append(pp)
        pool_s.append(ps)
    cat = lambda xs: xs[0] if len(xs) == 1 else jnp.concatenate(xs, axis=0)
    return (hp.reshape(bsz, seq, d), hs.reshape(dbsz, dseq, d), cat(conv_p), cat(conv_s), cat(pool_p), cat(pool_s))
```

```python
import functools

import jax
import jax.numpy as jnp
from jax import lax
from jax.experimental import pallas as pl
from jax.experimental.pallas import tpu as pltpu

CONV_WIDTH = 31
CONV_HIST = CONV_WIDTH - 1
POOL_WINDOWS = (2, 4, 8, 16)
POOL_HIST = max(POOL_WINDOWS) - 1
LN_EPS = 1e-5
PAST_LEN = 4096

V7X_SUBLANES = 8
V7X_LANES = 128
V7X_MXU_COLS = 256
V7X_VMEM_BYTES = 64 * 1024 * 1024

TOKEN_TILE = 512
STATE_SEQS_PER_STEP = 8
CONV_ROWS = 64
CONV_PAD = 32
POOL_PAD = 16
WEIGHT_CHUNK = 256
STAGE_SLOTS = 4

BF16 = jnp.bfloat16
F32 = jnp.float32


def _layer_norm(z, g, b):
    mu = jnp.mean(z, axis=-1, keepdims=True)
    zc = z - mu
    var = jnp.mean(zc * zc, axis=-1, keepdims=True)
    return zc * lax.rsqrt(var + LN_EPS) * g + b


def _dot(a, b):
    return jnp.dot(a, b, preferred_element_type=F32)


def _resident(shape):
    zeros = (0,) * len(shape)
    return pl.BlockSpec(shape, lambda *_: zeros, pipeline_mode=pl.Buffered(1))


def _nbytes(shape, dtype):
    n = 1
    for s in shape:
        n *= s
    return n * jnp.dtype(dtype).itemsize


def _vmem_limit(estimate_bytes):
    assert estimate_bytes <= V7X_VMEM_BYTES, estimate_bytes
    return int(estimate_bytes)


def _stage_shape(w_shape, axis):
    chunk = (WEIGHT_CHUNK, w_shape[1]) if axis == 0 else (w_shape[0], WEIGHT_CHUNK)
    return (STAGE_SLOTS,) + chunk


def _cast_weight(w_hbm, w_bf16, stage, sem, axis):
    assert w_hbm.shape[axis] % WEIGHT_CHUNK == 0 and stage.shape == _stage_shape(w_hbm.shape, axis)
    n = w_hbm.shape[axis] // WEIGHT_CHUNK
    piece = lambda i: pl.ds(i * WEIGHT_CHUNK, WEIGHT_CHUNK)

    def copy(i):
        src = w_hbm.at[piece(i), :] if axis == 0 else w_hbm.at[:, piece(i)]
        return pltpu.make_async_copy(src, stage.at[i % STAGE_SLOTS], sem.at[i % STAGE_SLOTS])

    ahead = STAGE_SLOTS - 1
    for i in range(min(ahead, n)):
        copy(i).start()
    for i in range(n):
        if i + ahead < n:
            copy(i + ahead).start()
        copy(i).wait()
        if axis == 0:
            w_bf16[piece(i), :] = stage[i % STAGE_SLOTS].astype(BF16)
        else:
            w_bf16[:, piece(i)] = stage[i % STAGE_SLOTS].astype(BF16)


def _ffn_ln_kernel(xp_ref, xs_ref, wg_hbm, wu_hbm, wd_hbm, g_ref, b_ref, op_ref, os_ref,
                   wg_ref, wu_ref, wd_ref, stage_cols, stage_rows, sem, *, n_prompt_tiles, alpha):
    i = pl.program_id(0)

    @pl.when(i == 0)
    def _():
        _cast_weight(wg_hbm, wg_ref, stage_cols, sem, 1)
        _cast_weight(wu_hbm, wu_ref, stage_cols, sem, 1)
        _cast_weight(wd_hbm, wd_ref, stage_rows, sem, 0)

    is_prompt = i < n_prompt_tiles
    x = jnp.where(is_prompt, xp_ref[...], xs_ref[...])
    xb = x.astype(BF16)
    gate = _dot(xb, wg_ref[...])
    up = _dot(xb, wu_ref[...])
    act = (gate * jax.nn.sigmoid(gate) * up).astype(BF16)
    ffn = _dot(act, wd_ref[...])
    y = _layer_norm(alpha * x + 0.5 * ffn, g_ref[...], b_ref[...])

    @pl.when(is_prompt)
    def _():
        op_ref[...] = y

    @pl.when(jnp.logical_not(is_prompt))
    def _():
        os_ref[...] = y


def _ffn_ln(xp, xs, wg, wu, wd, g, b, alpha):
    n_p, d = xp.shape
    n_s = xs.shape[0]
    d_ff = wg.shape[1]
    t = TOKEN_TILE
    assert n_p % t == 0 and n_s % t == 0
    npt, nst = n_p // t, n_s // t
    tile = (t, d)
    stage_cols, stage_rows = _stage_shape(wg.shape, 1), _stage_shape(wd.shape, 0)
    scratch = [(wg.shape, BF16), (wu.shape, BF16), (wd.shape, BF16), (stage_cols, F32), (stage_rows, F32)]
    vmem = (sum(_nbytes(shape, dtype) for shape, dtype in scratch)
            + 4 * 2 * _nbytes(tile, F32)
            + 2 * _nbytes((t, d_ff), F32) + _nbytes((t, d_ff), BF16)
            + 4 * _nbytes(tile, F32))
    kern = functools.partial(_ffn_ln_kernel, n_prompt_tiles=npt, alpha=alpha)
    return pl.pallas_call(
        kern,
        grid=(npt + nst,),
        in_specs=[
            pl.BlockSpec(tile, lambda i: (jnp.minimum(i, npt - 1), 0)),
            pl.BlockSpec(tile, lambda i: (jnp.maximum(i - npt, 0), 0)),
            pl.BlockSpec(memory_space=pl.ANY), pl.BlockSpec(memory_space=pl.ANY), pl.BlockSpec(memory_space=pl.ANY),
            _resident(g.shape), _resident(b.shape),
        ],
        out_specs=[
            pl.BlockSpec(tile, lambda i: (jnp.minimum(i, npt - 1), 0)),
            pl.BlockSpec(tile, lambda i: (jnp.maximum(i - npt, 0), 0)),
        ],
        out_shape=[jax.ShapeDtypeStruct(xp.shape, F32), jax.ShapeDtypeStruct(xs.shape, F32)],
        scratch_shapes=([pltpu.VMEM(shape, dtype) for shape, dtype in scratch]
                        + [pltpu.SemaphoreType.DMA((STAGE_SLOTS,))]),
        compiler_params=pltpu.CompilerParams(dimension_semantics=("arbitrary",), vmem_limit_bytes=_vmem_limit(vmem)),
        name="ffn_ln",
    )(xp, xs, wg, wu, wd, g, b)


def _store_conv_input(cbuf, glu3, first_frame):
    S, n, d_conv = glu3.shape
    tiles = d_conv // V7X_LANES
    for s in range(S):
        for c in range(tiles):
            rows = pl.ds(tiles * (CONV_PAD + first_frame) + c, n, stride=tiles)
            cbuf[s, rows, :] = glu3[s, :, c * V7X_LANES:(c + 1) * V7X_LANES]


def _broadcast_taps(wbc, cw_ref):
    for k in range(CONV_WIDTH):
        wbc[k] = jnp.broadcast_to(cw_ref[k:k + 1, :], wbc.shape[1:])


def _conv_chunk(cbuf, wbc, cb_ref, dconv, s, r0, out_row0, conv_rows):
    groups = conv_rows // V7X_SUBLANES
    tiles = wbc.shape[-1] // V7X_LANES
    for c in range(tiles):
        lanes = slice(c * V7X_LANES, (c + 1) * V7X_LANES)
        acc = [jnp.broadcast_to(cb_ref[:, lanes], (V7X_SUBLANES, V7X_LANES))] * groups
        for b in range(V7X_SUBLANES):
            a_max = (CONV_HIST - b) // V7X_SUBLANES
            rows = {g: cbuf[s, pl.ds(tiles * (r0 + CONV_PAD + V7X_SUBLANES * g - b) + c, V7X_SUBLANES, stride=tiles), :]
                    for g in range(-a_max, groups)}
            for a in range(a_max + 1):
                w = wbc[CONV_HIST - (V7X_SUBLANES * a + b), :, lanes]
                for m in range(groups):
                    acc[m] = acc[m] + rows[m - a] * w
        dconv[pl.ds(out_row0, conv_rows), lanes] = jnp.concatenate(acc, axis=0)


def _pooled(pext, L, pos1):
    pool_group = pext.shape[-1] // len(POOL_WINDOWS)
    out = []
    for gi, w in enumerate(POOL_WINDOWS):
        lanes = slice(gi * pool_group, (gi + 1) * pool_group)
        assert w & (w - 1) == 0 and w - 1 <= POOL_PAD
        total = pext[:, :, lanes]
        k = 1
        while k < w:
            total = total + jnp.concatenate([total[:, :k], total[:, :-k]], axis=1)
            k *= 2
        frame = pext[:, POOL_PAD:POOL_PAD + L, lanes]
        cnt = jnp.minimum(pos1, w).astype(F32)
        out.append(total[:, POOL_PAD:POOL_PAD + L] / cnt[None] - frame)
    return jnp.concatenate(out, axis=-1)


def _merge(h, dconv, pooled_bf16, gate_piece, n_pieces, clg_ref, clb_ref, w_cp_ref, w_pg_ref, psc_ref, w_pp_ref,
           w_out_ref, l2g_ref, l2b_ref, alpha):
    conv_act = _layer_norm(dconv[...], clg_ref[...], clb_ref[...])
    conv_act = conv_act * jax.nn.sigmoid(conv_act)
    branch_conv = _dot(conv_act.astype(BF16), w_cp_ref[...])
    pool_group = w_pg_ref.shape[-1]
    groups = []
    for gi in range(len(POOL_WINDOWS)):
        lanes = slice(gi * pool_group, (gi + 1) * pool_group)
        groups.append(_dot(pooled_bf16[:, lanes], w_pg_ref[gi]) * psc_ref[:, lanes])
    branch_pool = _dot(jnp.concatenate(groups, axis=-1).astype(BF16), w_pp_ref[...])
    half = n_pieces // 2
    gcols = branch_conv.shape[-1] // half
    merged = jnp.concatenate(
        [gate_piece(p) * branch_conv[:, p * gcols:(p + 1) * gcols]
         + gate_piece(half + p) * branch_pool[:, p * gcols:(p + 1) * gcols] for p in range(half)], axis=-1)
    mixed = _dot(merged.astype(BF16), w_out_ref[...])
    return _layer_norm(alpha * h + mixed, l2g_ref[...], l2b_ref[...])


def _mixer_prompt_kernel(h_ref, w_in_ref, w_gate_ref, b_gate_ref, cw_ref, cb_ref, clg_ref, clb_ref, w_cp_ref,
                         w_pg_ref, psc_ref, w_pp_ref, w_out_ref, l2g_ref, l2b_ref,
                         h2_ref, nc_ref, np_ref, cbuf, pext, wbc, dconv, *, alpha):
    t_idx = pl.program_id(1)
    L, d_model = h_ref.shape[1:]
    d_conv = wbc.shape[-1]
    d_pool = pext.shape[-1]
    hist_rows = cbuf.shape[1] - (d_conv // V7X_LANES) * L

    @pl.when(t_idx == 0)
    def _():
        cbuf[:, :hist_rows, :] = jnp.zeros((1, hist_rows, V7X_LANES), F32)
        pext[:, :POOL_PAD, :] = jnp.zeros((1, POOL_PAD, d_pool), F32)
        _broadcast_taps(wbc, cw_ref)

    @pl.when(t_idx > 0)
    def _():
        cbuf[:, :hist_rows, :] = cbuf[:, cbuf.shape[1] - hist_rows:, :]
        pext[:, :POOL_PAD, :] = pext[:, L:L + POOL_PAD, :]

    h = h_ref[0]
    hb = h.astype(BF16)
    u = _dot(hb, w_in_ref[...])
    gates = jax.nn.sigmoid(_dot(hb, w_gate_ref[...]) + b_gate_ref[...])
    glu = u[:, :d_conv] * jax.nn.sigmoid(u[:, d_conv:2 * d_conv])
    _store_conv_input(cbuf, glu.reshape(1, L, d_conv), 0)
    pext[:, POOL_PAD:, :] = u[:, 2 * d_conv:].reshape(1, L, d_pool)
    row = lax.broadcasted_iota(jnp.int32, (L, d_pool // len(POOL_WINDOWS)), 0)
    pooled = _pooled(pext, L, row + 1 + t_idx * L).reshape(L, d_pool).astype(BF16)

    def conv_trip(i, carry):
        r0 = pl.multiple_of(i * CONV_ROWS, CONV_ROWS)
        _conv_chunk(cbuf, wbc, cb_ref, dconv, 0, r0, r0, CONV_ROWS)
        return carry

    lax.fori_loop(0, L // CONV_ROWS, conv_trip, 0)

    n_pieces = 2 * d_model // V7X_MXU_COLS
    gate_piece = lambda p: gates[:, p * V7X_MXU_COLS:(p + 1) * V7X_MXU_COLS]
    h2_ref[0] = _merge(h, dconv, pooled, gate_piece, n_pieces, clg_ref, clb_ref, w_cp_ref, w_pg_ref, psc_ref, w_pp_ref,
                       w_out_ref, l2g_ref, l2b_ref, alpha)

    @pl.when(t_idx == pl.num_programs(1) - 1)
    def _():
        nc_ref[0] = glu[L - CONV_HIST:, :].reshape(1, CONV_HIST, d_conv)
        np_ref[0] = pext[:, POOL_PAD + L - POOL_HIST:POOL_PAD + L, :]


def _mixer_prompt(h, weights, alpha):
    (w_in, w_gate, b_gate, conv_w, conv_b, cln_g, cln_b, w_cp, w_pg, pscale, w_pp, w_out, l2g, l2b) = weights
    bsz, lseq, d = h.shape
    d_conv = conv_w.shape[1]
    d_pool = w_pp.shape[0]
    L = TOKEN_TILE
    assert lseq % L == 0 and L % CONV_ROWS == 0 and L >= CONV_HIST and L >= POOL_HIST
    hblock = (1, L, d)
    hmap = lambda b, t: (b, t, 0)
    state_map = lambda b, t: (0, b, 0, 0)
    scratch = [((1, (d_conv // V7X_LANES) * (CONV_PAD + L), V7X_LANES), F32), ((1, POOL_PAD + L, d_pool), F32),
               ((CONV_WIDTH, V7X_SUBLANES, d_conv), F32), ((L, d_conv), F32)]
    w_list = [w_in, w_gate, b_gate, conv_w, conv_b, cln_g, cln_b, w_cp, w_pg, pscale, w_pp, w_out, l2g, l2b]
    vmem = (sum(_nbytes(w.shape, w.dtype) for w in w_list)
            + 2 * 2 * _nbytes(hblock, F32)
            + 2 * 2 * (_nbytes((1, CONV_HIST, d_conv), F32) + _nbytes((1, POOL_HIST, d_pool), F32))
            + sum(_nbytes(shape, dtype) for shape, dtype in scratch)
            + _nbytes((L, w_in.shape[1]), F32) + _nbytes((L, 2 * d), F32)
            + 8 * _nbytes((L, d), F32))
    kern = functools.partial(_mixer_prompt_kernel, alpha=alpha)
    return pl.pallas_call(
        kern,
        grid=(bsz, lseq // L),
        in_specs=[pl.BlockSpec(hblock, hmap)] + [_resident(w.shape) for w in w_list],
        out_specs=[pl.BlockSpec(hblock, hmap), pl.BlockSpec((1, 1, CONV_HIST, d_conv), state_map),
                   pl.BlockSpec((1, 1, POOL_HIST, d_pool), state_map)],
        out_shape=[jax.ShapeDtypeStruct(h.shape, F32), jax.ShapeDtypeStruct((1, bsz, CONV_HIST, d_conv), F32),
                   jax.ShapeDtypeStruct((1, bsz, POOL_HIST, d_pool), F32)],
        scratch_shapes=[pltpu.VMEM(shape, dtype) for shape, dtype in scratch],
        compiler_params=pltpu.CompilerParams(dimension_semantics=("arbitrary", "arbitrary"),
                                             vmem_limit_bytes=_vmem_limit(vmem)),
        name="mixer_prompt",
    )(h, *w_list)


def _mixer_state_kernel(h_ref, cst_ref, pst_ref, w_in_ref, w_gate_ref, b_gate_ref, cw_ref, cb_ref, clg_ref, clb_ref,
                        w_cp_ref, w_pg_ref, psc_ref, w_pp_ref, w_out_ref, l2g_ref, l2b_ref,
                        h2_ref, nc_ref, np_ref, cbuf, pext, wbc, dconv, *, pos0, alpha):
    S, L, d_model = h_ref.shape
    N = S * L
    d_conv = wbc.shape[-1]
    d_pool = pext.shape[-1]

    h = h_ref[...].reshape(N, d_model)
    hb = h.astype(BF16)
    u = _dot(hb, w_in_ref[...])
    gates = jax.nn.sigmoid(_dot(hb, w_gate_ref[...]) + b_gate_ref[...])
    glu = u[:, :d_conv] * jax.nn.sigmoid(u[:, d_conv:2 * d_conv])

    _store_conv_input(cbuf, cst_ref[0], -CONV_HIST)
    pext[:, :POOL_PAD - POOL_HIST, :] = jnp.zeros((S, POOL_PAD - POOL_HIST, d_pool), F32)
    pext[:, POOL_PAD - POOL_HIST:POOL_PAD, :] = pst_ref[0]
    _broadcast_taps(wbc, cw_ref)
    glu3 = glu.reshape(S, L, d_conv)
    _store_conv_input(cbuf, glu3, 0)
    pext[:, POOL_PAD:, :] = u[:, 2 * d_conv:].reshape(S, L, d_pool)
    row = lax.broadcasted_iota(jnp.int32, (L, d_pool // len(POOL_WINDOWS)), 0)
    pooled = _pooled(pext, L, row + (pos0 + 1)).reshape(N, d_pool).astype(BF16)

    def conv_trip(s, carry):
        _conv_chunk(cbuf, wbc, cb_ref, dconv, s, 0, pl.multiple_of(s * L, L), L)
        return carry

    lax.fori_loop(0, S, conv_trip, 0)

    n_pieces = 2 * d_model // V7X_MXU_COLS
    gate_piece = lambda p: gates[:, p * V7X_MXU_COLS:(p + 1) * V7X_MXU_COLS]
    h2 = _merge(h, dconv, pooled, gate_piece, n_pieces, clg_ref, clb_ref, w_cp_ref, w_pg_ref, psc_ref, w_pp_ref,
                w_out_ref, l2g_ref, l2b_ref, alpha)
    h2_ref[...] = h2.reshape(S, L, d_model)
    nc_ref[0] = glu3[:, L - CONV_HIST:, :]
    np_ref[0] = pext[:, POOL_PAD + L - POOL_HIST:POOL_PAD + L, :]


def _mixer_state(h, conv_state, pool_state, weights, pos0, alpha):
    (w_in, w_gate, b_gate, conv_w, conv_b, cln_g, cln_b, w_cp, w_pg, pscale, w_pp, w_out, l2g, l2b) = weights
    bsz, L, d = h.shape
    d_conv = conv_w.shape[1]
    d_pool = w_pp.shape[0]
    S = STATE_SEQS_PER_STEP
    N = S * L
    assert bsz % S == 0 and L % V7X_SUBLANES == 0 and L >= CONV_HIST and L >= POOL_HIST

    hblock = (S, L, d)
    hmap = lambda i: (i, 0, 0)
    cstate_spec = pl.BlockSpec((1, S, CONV_HIST, d_conv), lambda i: (0, i, 0, 0))
    pstate_spec = pl.BlockSpec((1, S, POOL_HIST, d_pool), lambda i: (0, i, 0, 0))
    scratch = [((S, (d_conv // V7X_LANES) * (CONV_PAD + L), V7X_LANES), F32), ((S, POOL_PAD + L, d_pool), F32),
               ((CONV_WIDTH, V7X_SUBLANES, d_conv), F32), ((N, d_conv), F32)]
    w_list = [w_in, w_gate, b_gate, conv_w, conv_b, cln_g, cln_b, w_cp, w_pg, pscale, w_pp, w_out, l2g, l2b]
    vmem = (sum(_nbytes(w.shape, w.dtype) for w in w_list)
            + 2 * 2 * _nbytes(hblock, F32)
            + 2 * 2 * (_nbytes((S, CONV_HIST, d_conv), F32) + _nbytes((S, POOL_HIST, d_pool), F32))
            + sum(_nbytes(shape, dtype) for shape, dtype in scratch)
            + _nbytes((N, w_in.shape[1]), F32) + _nbytes((N, 2 * d), F32)
            + 8 * _nbytes((N, d), F32))
    kern = functools.partial(_mixer_state_kernel, pos0=pos0, alpha=alpha)
    return pl.pallas_call(
        kern,
        grid=(bsz // S,),
        in_specs=[pl.BlockSpec(hblock, hmap), cstate_spec, pstate_spec] + [_resident(w.shape) for w in w_list],
        out_specs=[pl.BlockSpec(hblock, hmap), cstate_spec, pstate_spec],
        out_shape=[jax.ShapeDtypeStruct(h.shape, F32), jax.ShapeDtypeStruct(conv_state.shape, F32),
                   jax.ShapeDtypeStruct(pool_state.shape, F32)],
        scratch_shapes=[pltpu.VMEM(shape, dtype) for shape, dtype in scratch],
        compiler_params=pltpu.CompilerParams(dimension_semantics=("arbitrary",), vmem_limit_bytes=_vmem_limit(vmem)),
        name="mixer_state",
    )(h, conv_state, pool_state, *w_list)


def kernel(x_prompt, x_sample, state_conv, state_pool, w_ffn1_gate, w_ffn1_up, w_ffn1_down, ln1_g, ln1_b, w_in, w_gate, b_gate, conv_w, conv_b, conv_ln_g, conv_ln_b, w_conv_proj, w_pool_group, pool_scale, w_pool_proj, w_out, ln2_g, ln2_b, w_ffn2_gate, w_ffn2_up, w_ffn2_down, ln3_g, ln3_b):
    depth = w_in.shape[0]
    bsz, seq, d = x_prompt.shape
    dbsz, dseq, _ = x_sample.shape
    alpha = (2.0 * depth) ** 0.25
    row = lambda v: v.reshape(1, -1)

    hp = x_prompt.reshape(bsz * seq, d)
    hs = x_sample.reshape(dbsz * dseq, d)
    conv_p, conv_s, pool_p, pool_s = [], [], [], []
    for l in range(depth):
        hp, hs = _ffn_ln(hp, hs, w_ffn1_gate[l], w_ffn1_up[l], w_ffn1_down[l], row(ln1_g[l]), row(ln1_b[l]), alpha)
        mix_w = (w_in[l].astype(BF16), w_gate[l].astype(BF16), row(b_gate[l]), conv_w[l], row(conv_b[l]),
                 row(conv_ln_g[l]), row(conv_ln_b[l]), w_conv_proj[l].astype(BF16), w_pool_group[l].astype(BF16),
                 row(pool_scale[l]), w_pool_proj[l].astype(BF16), w_out[l].astype(BF16), row(ln2_g[l]), row(ln2_b[l]))
        hp3, cp, pp = _mixer_prompt(hp.reshape(bsz, seq, d), mix_w, alpha)
        hs3, cs, ps = _mixer_state(hs.reshape(dbsz, dseq, d), state_conv[l:l + 1], state_pool[l:l + 1], mix_w, PAST_LEN, alpha)
        hp, hs = _ffn_ln(hp3.reshape(bsz * seq, d), hs3.reshape(dbsz * dseq, d), w_ffn2_gate[l], w_ffn2_up[l],
                         w_ffn2_down[l], row(ln3_g[l]), row(ln3_b[l]), alpha)
        conv_p.append(cp)
        conv_s.append(cs)
        pool_p.append(pp)
        pool_s.append(ps)
    cat = lambda xs: xs[0] if len(xs) == 1 else jnp.concatenate(xs, axis=0)
    return (hp.reshape(bsz, seq, d), hs.reshape(dbsz, dseq, d), cat(conv_p), cat(conv_s), cat(pool_p), cat(pool_s))
```

```python
import functools

import jax
import jax.numpy as jnp
from jax import lax
from jax.experimental import pallas as pl
from jax.experimental.pallas import tpu as pltpu

CONV_WIDTH = 31
CONV_HIST = CONV_WIDTH - 1
POOL_WINDOWS = (2, 4, 8, 16)
POOL_HIST = max(POOL_WINDOWS) - 1
LN_EPS = 1e-5
PAST_LEN = 4096

V7X_SUBLANES = 8
V7X_LANES = 128
V7X_MXU_COLS = 256
V7X_VMEM_BYTES = 64 * 1024 * 1024

TOKEN_TILE = 512
FFN_ROWS = 256
STATE_SEQS_PER_STEP = 8
CONV_ROWS = 64
CONV_PAD = 32
POOL_PAD = 16
WEIGHT_CHUNK = 256
STAGE_SLOTS = 4

BF16 = jnp.bfloat16
F32 = jnp.float32


def _layer_norm(z, g, b):
    mu = jnp.mean(z, axis=-1, keepdims=True)
    zc = z - mu
    var = jnp.mean(zc * zc, axis=-1, keepdims=True)
    return zc * lax.rsqrt(var + LN_EPS) * g + b


def _dot(a, b):
    return jnp.dot(a, b, preferred_element_type=F32)


def _resident(shape):
    zeros = (0,) * len(shape)
    return pl.BlockSpec(shape, lambda *_: zeros, pipeline_mode=pl.Buffered(1))


def _nbytes(shape, dtype):
    n = 1
    for s in shape:
        n *= s
    return n * jnp.dtype(dtype).itemsize


def _vmem_limit(estimate_bytes):
    assert estimate_bytes <= V7X_VMEM_BYTES, estimate_bytes
    return int(estimate_bytes)


def _stage_shape(w_shape, axis):
    chunk = (WEIGHT_CHUNK, w_shape[1]) if axis == 0 else (w_shape[0], WEIGHT_CHUNK)
    return (STAGE_SLOTS,) + chunk


def _cast_weight(w_hbm, w_bf16, stage, sem, axis):
    assert w_hbm.shape[axis] % WEIGHT_CHUNK == 0 and stage.shape == _stage_shape(w_hbm.shape, axis)
    n = w_hbm.shape[axis] // WEIGHT_CHUNK
    piece = lambda i: pl.ds(i * WEIGHT_CHUNK, WEIGHT_CHUNK)

    def copy(i):
        src = w_hbm.at[piece(i), :] if axis == 0 else w_hbm.at[:, piece(i)]
        return pltpu.make_async_copy(src, stage.at[i % STAGE_SLOTS], sem.at[i % STAGE_SLOTS])

    ahead = STAGE_SLOTS - 1
    for i in range(min(ahead, n)):
        copy(i).start()
    for i in range(n):
        if i + ahead < n:
            copy(i + ahead).start()
        copy(i).wait()
        if axis == 0:
            w_bf16[piece(i), :] = stage[i % STAGE_SLOTS].astype(BF16)
        else:
            w_bf16[:, piece(i)] = stage[i % STAGE_SLOTS].astype(BF16)


def _ffn_ln_kernel(xp_ref, xs_ref, wg_hbm, wu_hbm, wd_hbm, g_ref, b_ref, op_ref, os_ref,
                   wg_ref, wu_ref, wd_ref, stage_cols, stage_rows, sem, *, n_prompt_tiles, alpha):
    i = pl.program_id(0)

    @pl.when(i == 0)
    def _():
        _cast_weight(wg_hbm, wg_ref, stage_cols, sem, 1)
        _cast_weight(wu_hbm, wu_ref, stage_cols, sem, 1)
        _cast_weight(wd_hbm, wd_ref, stage_rows, sem, 0)

    is_prompt = i < n_prompt_tiles
    t = xp_ref.shape[0]
    ys = []
    for r in range(0, t, FFN_ROWS):
        x = jnp.where(is_prompt, xp_ref[r:r + FFN_ROWS, :], xs_ref[r:r + FFN_ROWS, :])
        xb = x.astype(BF16)
        gate = _dot(xb, wg_ref[...])
        up = _dot(xb, wu_ref[...])
        act = (gate * jax.nn.sigmoid(gate) * up).astype(BF16)
        ffn = _dot(act, wd_ref[...])
        ys.append(_layer_norm(alpha * x + 0.5 * ffn, g_ref[...], b_ref[...]))
    y = jnp.concatenate(ys, axis=0)

    @pl.when(is_prompt)
    def _():
        op_ref[...] = y

    @pl.when(jnp.logical_not(is_prompt))
    def _():
        os_ref[...] = y


def _ffn_ln(xp, xs, wg, wu, wd, g, b, alpha):
    n_p, d = xp.shape
    n_s = xs.shape[0]
    d_ff = wg.shape[1]
    t = TOKEN_TILE
    assert n_p % t == 0 and n_s % t == 0
    npt, nst = n_p // t, n_s // t
    tile = (t, d)
    stage_cols, stage_rows = _stage_shape(wg.shape, 1), _stage_shape(wd.shape, 0)
    scratch = [(wg.shape, BF16), (wu.shape, BF16), (wd.shape, BF16), (stage_cols, F32), (stage_rows, F32)]
    vmem = (sum(_nbytes(shape, dtype) for shape, dtype in scratch)
            + 4 * 2 * _nbytes(tile, F32)
            + 2 * _nbytes((t, d_ff), F32) + _nbytes((t, d_ff), BF16)
            + 4 * _nbytes(tile, F32))
    kern = functools.partial(_ffn_ln_kernel, n_prompt_tiles=npt, alpha=alpha)
    return pl.pallas_call(
        kern,
        grid=(npt + nst,),
        in_specs=[
            pl.BlockSpec(tile, lambda i: (jnp.minimum(i, npt - 1), 0)),
            pl.BlockSpec(tile, lambda i: (jnp.maximum(i - npt, 0), 0)),
            pl.BlockSpec(memory_space=pl.ANY), pl.BlockSpec(memory_space=pl.ANY), pl.BlockSpec(memory_space=pl.ANY),
            _resident(g.shape), _resident(b.shape),
        ],
        out_specs=[
            pl.BlockSpec(tile, lambda i: (jnp.minimum(i, npt - 1), 0)),
            pl.BlockSpec(tile, lambda i: (jnp.maximum(i - npt, 0), 0)),
        ],
        out_shape=[jax.ShapeDtypeStruct(xp.shape, F32), jax.ShapeDtypeStruct(xs.shape, F32)],
        scratch_shapes=([pltpu.VMEM(shape, dtype) for shape, dtype in scratch]
                        + [pltpu.SemaphoreType.DMA((STAGE_SLOTS,))]),
        compiler_params=pltpu.CompilerParams(dimension_semantics=("arbitrary",), vmem_limit_bytes=_vmem_limit(vmem)),
        name="ffn_ln",
    )(xp, xs, wg, wu, wd, g, b)


def _store_conv_input(cbuf, glu3, first_frame):
    S, n, d_conv = glu3.shape
    tiles = d_conv // V7X_LANES
    for s in range(S):
        for c in range(tiles):
            rows = pl.ds(tiles * (CONV_PAD + first_frame) + c, n, stride=tiles)
            cbuf[s, rows, :] = glu3[s, :, c * V7X_LANES:(c + 1) * V7X_LANES]


def _broadcast_taps(wbc, cw_ref):
    for k in range(CONV_WIDTH):
        wbc[k] = jnp.broadcast_to(cw_ref[k:k + 1, :], wbc.shape[1:])


def _conv_chunk(cbuf, wbc, cb_ref, dconv, s, r0, out_row0, conv_rows):
    groups = conv_rows // V7X_SUBLANES
    tiles = wbc.shape[-1] // V7X_LANES
    for c in range(tiles):
        lanes = slice(c * V7X_LANES, (c + 1) * V7X_LANES)
        acc = [jnp.broadcast_to(cb_ref[:, lanes], (V7X_SUBLANES, V7X_LANES))] * groups
        for b in range(V7X_SUBLANES):
            a_max = (CONV_HIST - b) // V7X_SUBLANES
            rows = {g: cbuf[s, pl.ds(tiles * (r0 + CONV_PAD + V7X_SUBLANES * g - b) + c, V7X_SUBLANES, stride=tiles), :]
                    for g in range(-a_max, groups)}
            for a in range(a_max + 1):
                w = wbc[CONV_HIST - (V7X_SUBLANES * a + b), :, lanes]
                for m in range(groups):
                    acc[m] = acc[m] + rows[m - a] * w
        dconv[pl.ds(out_row0, conv_rows), lanes] = jnp.concatenate(acc, axis=0)


def _pooled(pext, L, pos1):
    pool_group = pext.shape[-1] // len(POOL_WINDOWS)
    out = []
    for gi, w in enumerate(POOL_WINDOWS):
        lanes = slice(gi * pool_group, (gi + 1) * pool_group)
        assert w & (w - 1) == 0 and w - 1 <= POOL_PAD
        total = pext[:, :, lanes]
        k = 1
        while k < w:
            total = total + jnp.concatenate([total[:, :k], total[:, :-k]], axis=1)
            k *= 2
        frame = pext[:, POOL_PAD:POOL_PAD + L, lanes]
        cnt = jnp.minimum(pos1, w).astype(F32)
        out.append(total[:, POOL_PAD:POOL_PAD + L] / cnt[None] - frame)
    return jnp.concatenate(out, axis=-1)


def _merge(h, dconv, pooled_bf16, gate_piece, n_pieces, clg_ref, clb_ref, w_cp_ref, w_pg_ref, psc_ref, w_pp_ref,
           w_out_ref, l2g_ref, l2b_ref, alpha):
    conv_act = _layer_norm(dconv, clg_ref[...], clb_ref[...])
    conv_act = conv_act * jax.nn.sigmoid(conv_act)
    branch_conv = _dot(conv_act.astype(BF16), w_cp_ref[...])
    pool_group = w_pg_ref.shape[-1]
    groups = []
    for gi in range(len(POOL_WINDOWS)):
        lanes = slice(gi * pool_group, (gi + 1) * pool_group)
        groups.append(_dot(pooled_bf16[:, lanes], w_pg_ref[gi]) * psc_ref[:, lanes])
    branch_pool = _dot(jnp.concatenate(groups, axis=-1).astype(BF16), w_pp_ref[...])
    half = n_pieces // 2
    gcols = branch_conv.shape[-1] // half
    merged = jnp.concatenate(
        [gate_piece(p) * branch_conv[:, p * gcols:(p + 1) * gcols]
         + gate_piece(half + p) * branch_pool[:, p * gcols:(p + 1) * gcols] for p in range(half)], axis=-1)
    mixed = _dot(merged.astype(BF16), w_out_ref[...])
    return _layer_norm(alpha * h + mixed, l2g_ref[...], l2b_ref[...])


def _mixer_prompt_kernel(h_ref, w_in_ref, w_gate_ref, b_gate_ref, cw_ref, cb_ref, clg_ref, clb_ref, w_cp_ref,
                         w_pg_ref, psc_ref, w_pp_ref, w_out_ref, l2g_ref, l2b_ref,
                         h2_ref, nc_ref, np_ref, cbuf, pext, wbc, dconv, *, alpha):
    t_idx = pl.program_id(1)
    L, d_model = h_ref.shape[1:]
    d_conv = wbc.shape[-1]
    d_pool = pext.shape[-1]
    hist_rows = cbuf.shape[1] - (d_conv // V7X_LANES) * L

    @pl.when(t_idx == 0)
    def _():
        cbuf[:, :hist_rows, :] = jnp.zeros((1, hist_rows, V7X_LANES), F32)
        pext[:, :POOL_PAD, :] = jnp.zeros((1, POOL_PAD, d_pool), F32)
        _broadcast_taps(wbc, cw_ref)

    @pl.when(t_idx > 0)
    def _():
        cbuf[:, :hist_rows, :] = cbuf[:, cbuf.shape[1] - hist_rows:, :]
        pext[:, :POOL_PAD, :] = pext[:, L:L + POOL_PAD, :]

    h = h_ref[0]
    hb = h.astype(BF16)
    u = _dot(hb, w_in_ref[...])
    gates = jax.nn.sigmoid(_dot(hb, w_gate_ref[...]) + b_gate_ref[...])
    glu = u[:, :d_conv] * jax.nn.sigmoid(u[:, d_conv:2 * d_conv])
    _store_conv_input(cbuf, glu.reshape(1, L, d_conv), 0)
    pext[:, POOL_PAD:, :] = u[:, 2 * d_conv:].reshape(1, L, d_pool)
    row = lax.broadcasted_iota(jnp.int32, (L, d_pool // len(POOL_WINDOWS)), 0)
    pooled = _pooled(pext, L, row + 1 + t_idx * L).reshape(L, d_pool).astype(BF16)

    def conv_trip(i, carry):
        r0 = pl.multiple_of(i * CONV_ROWS, CONV_ROWS)
        _conv_chunk(cbuf, wbc, cb_ref, dconv, 0, r0, r0, CONV_ROWS)
        return carry

    lax.fori_loop(0, L // CONV_ROWS, conv_trip, 0)

    n_pieces = 2 * d_model // V7X_MXU_COLS
    gate_piece = lambda p: gates[:, p * V7X_MXU_COLS:(p + 1) * V7X_MXU_COLS]
    h2_ref[0] = _merge(h, dconv[...], pooled, gate_piece, n_pieces, clg_ref, clb_ref, w_cp_ref, w_pg_ref, psc_ref,
                       w_pp_ref, w_out_ref, l2g_ref, l2b_ref, alpha)

    @pl.when(t_idx == pl.num_programs(1) - 1)
    def _():
        nc_ref[0] = glu[L - CONV_HIST:, :].reshape(1, CONV_HIST, d_conv)
        np_ref[0] = pext[:, POOL_PAD + L - POOL_HIST:POOL_PAD + L, :]


def _mixer_prompt(h, weights, alpha):
    (w_in, w_gate, b_gate, conv_w, conv_b, cln_g, cln_b, w_cp, w_pg, pscale, w_pp, w_out, l2g, l2b) = weights
    bsz, lseq, d = h.shape
    d_conv = conv_w.shape[1]
    d_pool = w_pp.shape[0]
    L = TOKEN_TILE
    assert lseq % L == 0 and L % CONV_ROWS == 0 and L >= CONV_HIST and L >= POOL_HIST
    hblock = (1, L, d)
    hmap = lambda b, t: (b, t, 0)
    state_map = lambda b, t: (0, b, 0, 0)
    scratch = [((1, (d_conv // V7X_LANES) * (CONV_PAD + L), V7X_LANES), F32), ((1, POOL_PAD + L, d_pool), F32),
               ((CONV_WIDTH, V7X_SUBLANES, d_conv), F32), ((L, d_conv), F32)]
    w_list = [w_in, w_gate, b_gate, conv_w, conv_b, cln_g, cln_b, w_cp, w_pg, pscale, w_pp, w_out, l2g, l2b]
    vmem = (sum(_nbytes(w.shape, w.dtype) for w in w_list)
            + 2 * 2 * _nbytes(hblock, F32)
            + 2 * 2 * (_nbytes((1, CONV_HIST, d_conv), F32) + _nbytes((1, POOL_HIST, d_pool), F32))
            + sum(_nbytes(shape, dtype) for shape, dtype in scratch)
            + _nbytes((L, w_in.shape[1]), F32) + _nbytes((L, 2 * d), F32)
            + 8 * _nbytes((L, d), F32))
    kern = functools.partial(_mixer_prompt_kernel, alpha=alpha)
    return pl.pallas_call(
        kern,
        grid=(bsz, lseq // L),
        in_specs=[pl.BlockSpec(hblock, hmap)] + [_resident(w.shape) for w in w_list],
        out_specs=[pl.BlockSpec(hblock, hmap), pl.BlockSpec((1, 1, CONV_HIST, d_conv), state_map),
                   pl.BlockSpec((1, 1, POOL_HIST, d_pool), state_map)],
        out_shape=[jax.ShapeDtypeStruct(h.shape, F32), jax.ShapeDtypeStruct((1, bsz, CONV_HIST, d_conv), F32),
                   jax.ShapeDtypeStruct((1, bsz, POOL_HIST, d_pool), F32)],
        scratch_shapes=[pltpu.VMEM(shape, dtype) for shape, dtype in scratch],
        compiler_params=pltpu.CompilerParams(dimension_semantics=("arbitrary", "arbitrary"),
                                             vmem_limit_bytes=_vmem_limit(vmem)),
        name="mixer_prompt",
    )(h, *w_list)


def _mixer_state_kernel(h_ref, cst_ref, pst_ref, w_in_ref, w_gate_ref, b_gate_ref, cw_ref, cb_ref, clg_ref, clb_ref,
                        w_cp_ref, w_pg_ref, psc_ref, w_pp_ref, w_out_ref, l2g_ref, l2b_ref,
                        h2_ref, nc_ref, np_ref, cbuf, pext, wbc, dconv, *, pos0, alpha):
    S, L, d_model = h_ref.shape
    N = S * L
    d_conv = wbc.shape[-1]
    d_pool = pext.shape[-1]

    h = h_ref[...].reshape(N, d_model)
    hb = h.astype(BF16)
    u = _dot(hb, w_in_ref[...])
    gates = jax.nn.sigmoid(_dot(hb, w_gate_ref[...]) + b_gate_ref[...])
    glu = u[:, :d_conv] * jax.nn.sigmoid(u[:, d_conv:2 * d_conv])

    _store_conv_input(cbuf, cst_ref[0], -CONV_HIST)
    pext[:, :POOL_PAD - POOL_HIST, :] = jnp.zeros((S, POOL_PAD - POOL_HIST, d_pool), F32)
    pext[:, POOL_PAD - POOL_HIST:POOL_PAD, :] = pst_ref[0]
    _broadcast_taps(wbc, cw_ref)
    glu3 = glu.reshape(S, L, d_conv)
    _store_conv_input(cbuf, glu3, 0)
    pext[:, POOL_PAD:, :] = u[:, 2 * d_conv:].reshape(S, L, d_pool)
    row = lax.broadcasted_iota(jnp.int32, (L, d_pool // len(POOL_WINDOWS)), 0)
    pooled = _pooled(pext, L, row + (pos0 + 1)).reshape(N, d_pool).astype(BF16)

    def conv_trip(s, carry):
        _conv_chunk(cbuf, wbc, cb_ref, dconv, s, 0, pl.multiple_of(s * L, L), L)
        return carry

    lax.fori_loop(0, S, conv_trip, 0)

    n_pieces = 2 * d_model // V7X_MXU_COLS
    gate_piece = lambda p: gates[:, p * V7X_MXU_COLS:(p + 1) * V7X_MXU_COLS]
    h2 = _merge(h, dconv[...], pooled, gate_piece, n_pieces, clg_ref, clb_ref, w_cp_ref, w_pg_ref, psc_ref, w_pp_ref,
                w_out_ref, l2g_ref, l2b_ref, alpha)
    h2_ref[...] = h2.reshape(S, L, d_model)
    nc_ref[0] = glu3[:, L - CONV_HIST:, :]
    np_ref[0] = pext[:, POOL_PAD + L - POOL_HIST:POOL_PAD + L, :]


def _mixer_state(h, conv_state, pool_state, weights, pos0, alpha):
    (w_in, w_gate, b_gate, conv_w, conv_b, cln_g, cln_b, w_cp, w_pg, pscale, w_pp, w_out, l2g, l2b) = weights
    bsz, L, d = h.shape
    d_conv = conv_w.shape[1]
    d_pool = w_pp.shape[0]
    S = STATE_SEQS_PER_STEP
    N = S * L
    assert bsz % S == 0 and L % V7X_SUBLANES == 0 and L >= CONV_HIST and L >= POOL_HIST

    hblock = (S, L, d)
    hmap = lambda i: (i, 0, 0)
    cstate_spec = pl.BlockSpec((1, S, CONV_HIST, d_conv), lambda i: (0, i, 0, 0))
    pstate_spec = pl.BlockSpec((1, S, POOL_HIST, d_pool), lambda i: (0, i, 0, 0))
    scratch = [((S, (d_conv // V7X_LANES) * (CONV_PAD + L), V7X_LANES), F32), ((S, POOL_PAD + L, d_pool), F32),
               ((CONV_WIDTH, V7X_SUBLANES, d_conv), F32), ((N, d_conv), F32)]
    w_list = [w_in, w_gate, b_gate, conv_w, conv_b, cln_g, cln_b, w_cp, w_pg, pscale, w_pp, w_out, l2g, l2b]
    vmem = (sum(_nbytes(w.shape, w.dtype) for w in w_list)
            + 2 * 2 * _nbytes(hblock, F32)
            + 2 * 2 * (_nbytes((S, CONV_HIST, d_conv), F32) + _nbytes((S, POOL_HIST, d_pool), F32))
            + sum(_nbytes(shape, dtype) for shape, dtype in scratch)
            + _nbytes((N, w_in.shape[1]), F32) + _nbytes((N, 2 * d), F32)
            + 8 * _nbytes((N, d), F32))
    kern = functools.partial(_mixer_state_kernel, pos0=pos0, alpha=alpha)
    return pl.pallas_call(
        kern,
        grid=(bsz // S,),
        in_specs=[pl.BlockSpec(hblock, hmap), cstate_spec, pstate_spec] + [_resident(w.shape) for w in w_list],
        out_specs=[pl.BlockSpec(hblock, hmap), cstate_spec, pstate_spec],
        out_shape=[jax.ShapeDtypeStruct(h.shape, F32), jax.ShapeDtypeStruct(conv_state.shape, F32),
                   jax.ShapeDtypeStruct(pool_state.shape, F32)],
        scratch_shapes=[pltpu.VMEM(shape, dtype) for shape, dtype in scratch],
        compiler_params=pltpu.CompilerParams(dimension_semantics=("arbitrary",), vmem_limit_bytes=_vmem_limit(vmem)),
        name="mixer_state",
    )(h, conv_state, pool_state, *w_list)


def kernel(x_prompt, x_sample, state_conv, state_pool, w_ffn1_gate, w_ffn1_up, w_ffn1_down, ln1_g, ln1_b, w_in, w_gate, b_gate, conv_w, conv_b, conv_ln_g, conv_ln_b, w_conv_proj, w_pool_group, pool_scale, w_pool_proj, w_out, ln2_g, ln2_b, w_ffn2_gate, w_ffn2_up, w_ffn2_down, ln3_g, ln3_b):
    depth = w_in.shape[0]
    bsz, seq, d = x_prompt.shape
    dbsz, dseq, _ = x_sample.shape
    alpha = (2.0 * depth) ** 0.25
    row = lambda v: v.reshape(1, -1)

    hp = x_prompt.reshape(bsz * seq, d)
    hs = x_sample.reshape(dbsz * dseq, d)
    conv_p, conv_s, pool_p, pool_s = [], [], [], []
    for l in range(depth):
        hp, hs = _ffn_ln(hp, hs, w_ffn1_gate[l], w_ffn1_up[l], w_ffn1_down[l], row(ln1_g[l]), row(ln1_b[l]), alpha)
        mix_w = (w_in[l].astype(BF16), w_gate[l].astype(BF16), row(b_gate[l]), conv_w[l], row(conv_b[l]),
                 row(conv_ln_g[l]), row(conv_ln_b[l]), w_conv_proj[l].astype(BF16), w_pool_group[l].astype(BF16),
                 row(pool_scale[l]), w_pool_proj[l].astype(BF16), w_out[l].astype(BF16), row(ln2_g[l]), row(ln2_b[l]))
        hp3, cp, pp = _mixer_prompt(hp.reshape(bsz, seq, d), mix_w, alpha)
        hs3, cs, ps = _mixer_state(hs.reshape(dbsz, dseq, d), state_conv[l:l + 1], state_pool[l:l + 1], mix_w, PAST_LEN, alpha)
        hp, hs = _ffn_ln(hp3.reshape(bsz * seq, d), hs3.reshape(dbsz * dseq, d), w_ffn2_gate[l], w_ffn2_up[l],
                         w_ffn2_down[l], row(ln3_g[l]), row(ln3_b[l]), alpha)
        conv_p.append(cp)
        conv_s.append(cs)
        pool_p.append(pp)
        pool_s.append(ps)
    cat = lambda xs: xs[0] if len(xs) == 1 else jnp.concatenate(xs, axis=0)
    return (hp.reshape(bsz, seq, d), hs.reshape(dbsz, dseq, d), cat(conv_p), cat(conv_s), cat(pool_p), cat(pool_s))
```

```python
import functools

import jax
import jax.numpy as jnp
from jax import lax
from jax.experimental import pallas as pl
from jax.experimental.pallas import tpu as pltpu

CONV_WIDTH = 31
CONV_HIST = CONV_WIDTH - 1
POOL_WINDOWS = (2, 4, 8, 16)
POOL_HIST = max(POOL_WINDOWS) - 1
LN_EPS = 1e-5
PAST_LEN = 4096

V7X_SUBLANES = 8
V7X_LANES = 128
V7X_MXU_COLS = 256
V7X_VMEM_BYTES = 64 * 1024 * 1024

TOKEN_TILE = 512
FFN_BLOCKS = (256, 256)
STATE_SEQS_PER_STEP = 8
CONV_ROWS = 128
CONV_PAD = 32
POOL_PAD = 16
WEIGHT_CHUNK = 128
STAGE_SLOTS = 4

BF16 = jnp.bfloat16
F32 = jnp.float32


def _layer_norm(z, g, b):
    mu = jnp.mean(z, axis=-1, keepdims=True)
    zc = z - mu
    var = jnp.mean(zc * zc, axis=-1, keepdims=True)
    return zc * lax.rsqrt(var + LN_EPS) * g + b


def _dot(a, b):
    return jnp.dot(a, b, preferred_element_type=F32)


def _resident(shape):
    zeros = (0,) * len(shape)
    return pl.BlockSpec(shape, lambda *_: zeros, pipeline_mode=pl.Buffered(1))


def _nbytes(shape, dtype):
    n = 1
    for s in shape:
        n *= s
    return n * jnp.dtype(dtype).itemsize


def _vmem_limit(estimate_bytes):
    assert estimate_bytes <= V7X_VMEM_BYTES, estimate_bytes
    return int(estimate_bytes)


def _stage_shape(w_shape):
    return (STAGE_SLOTS, WEIGHT_CHUNK, w_shape[1])


def _cast_weight(w_hbm, w_bf16, stage, sem):
    assert w_hbm.shape[0] % WEIGHT_CHUNK == 0 and stage.shape == _stage_shape(w_hbm.shape)
    n = w_hbm.shape[0] // WEIGHT_CHUNK
    piece = lambda i: pl.ds(i * WEIGHT_CHUNK, WEIGHT_CHUNK)

    def copy(i):
        return pltpu.make_async_copy(w_hbm.at[piece(i), :], stage.at[i % STAGE_SLOTS], sem.at[i % STAGE_SLOTS])

    ahead = STAGE_SLOTS - 1
    for i in range(min(ahead, n)):
        copy(i).start()
    for i in range(n):
        if i + ahead < n:
            copy(i + ahead).start()
        copy(i).wait()
        w_bf16[piece(i), :] = stage[i % STAGE_SLOTS].astype(BF16)


def _ffn_ln_kernel(xp_ref, xs_ref, wg_hbm, wu_hbm, wd_hbm, g_ref, b_ref, op_ref, os_ref,
                   wg_ref, wu_ref, wd_ref, stage_wide, stage_narrow, sem, *, n_prompt_tiles, alpha):
    i = pl.program_id(0)

    @pl.when(i == 0)
    def _():
        _cast_weight(wg_hbm, wg_ref, stage_wide, sem)
        _cast_weight(wu_hbm, wu_ref, stage_wide, sem)
        _cast_weight(wd_hbm, wd_ref, stage_narrow, sem)

    is_prompt = i < n_prompt_tiles
    assert sum(FFN_BLOCKS) == xp_ref.shape[0]
    ys = []
    r = 0
    for rows in FFN_BLOCKS:
        x = jnp.where(is_prompt, xp_ref[r:r + rows, :], xs_ref[r:r + rows, :])
        r += rows
        xb = x.astype(BF16)
        gate = _dot(xb, wg_ref[...])
        up = _dot(xb, wu_ref[...])
        act = (gate * jax.nn.sigmoid(gate) * up).astype(BF16)
        ffn = _dot(act, wd_ref[...])
        ys.append(_layer_norm(alpha * x + 0.5 * ffn, g_ref[...], b_ref[...]))
    y = jnp.concatenate(ys, axis=0)

    @pl.when(is_prompt)
    def _():
        op_ref[...] = y

    @pl.when(jnp.logical_not(is_prompt))
    def _():
        os_ref[...] = y


def _ffn_ln(xp, xs, wg, wu, wd, g, b, alpha):
    n_p, d = xp.shape
    n_s = xs.shape[0]
    d_ff = wg.shape[1]
    t = TOKEN_TILE
    assert n_p % t == 0 and n_s % t == 0
    npt, nst = n_p // t, n_s // t
    tile = (t, d)
    scratch = [(wg.shape, BF16), (wu.shape, BF16), (wd.shape, BF16), (_stage_shape(wg.shape), F32),
               (_stage_shape(wd.shape), F32)]
    vmem = (sum(_nbytes(shape, dtype) for shape, dtype in scratch)
            + 4 * 2 * _nbytes(tile, F32)
            + 2 * _nbytes((t, d_ff), F32) + _nbytes((t, d_ff), BF16)
            + 4 * _nbytes(tile, F32))
    kern = functools.partial(_ffn_ln_kernel, n_prompt_tiles=npt, alpha=alpha)
    return pl.pallas_call(
        kern,
        grid=(npt + nst,),
        in_specs=[
            pl.BlockSpec(tile, lambda i: (jnp.minimum(i, npt - 1), 0)),
            pl.BlockSpec(tile, lambda i: (jnp.maximum(i - npt, 0), 0)),
            pl.BlockSpec(memory_space=pl.ANY), pl.BlockSpec(memory_space=pl.ANY), pl.BlockSpec(memory_space=pl.ANY),
            _resident(g.shape), _resident(b.shape),
        ],
        out_specs=[
            pl.BlockSpec(tile, lambda i: (jnp.minimum(i, npt - 1), 0)),
            pl.BlockSpec(tile, lambda i: (jnp.maximum(i - npt, 0), 0)),
        ],
        out_shape=[jax.ShapeDtypeStruct(xp.shape, F32), jax.ShapeDtypeStruct(xs.shape, F32)],
        scratch_shapes=([pltpu.VMEM(shape, dtype) for shape, dtype in scratch]
                        + [pltpu.SemaphoreType.DMA((STAGE_SLOTS,))]),
        compiler_params=pltpu.CompilerParams(dimension_semantics=("arbitrary",), vmem_limit_bytes=_vmem_limit(vmem)),
        name="ffn_ln",
    )(xp, xs, wg, wu, wd, g, b)


def _store_conv_input(cbuf, glu3, first_frame):
    S, n, d_conv = glu3.shape
    tiles = d_conv // V7X_LANES
    for s in range(S):
        for c in range(tiles):
            rows = pl.ds(tiles * (CONV_PAD + first_frame) + c, n, stride=tiles)
            cbuf[s, rows, :] = glu3[s, :, c * V7X_LANES:(c + 1) * V7X_LANES]


def _broadcast_taps(wbc, cw_ref):
    for k in range(CONV_WIDTH):
        wbc[k] = jnp.broadcast_to(cw_ref[k:k + 1, :], wbc.shape[1:])


def _conv_chunk(cbuf, wbc, cb_ref, dconv, s, r0, out_row0, conv_rows):
    groups = conv_rows // V7X_SUBLANES
    tiles = wbc.shape[-1] // V7X_LANES
    for c in range(tiles):
        lanes = slice(c * V7X_LANES, (c + 1) * V7X_LANES)
        acc = [jnp.broadcast_to(cb_ref[:, lanes], (V7X_SUBLANES, V7X_LANES))] * groups
        for b in range(V7X_SUBLANES):
            a_max = (CONV_HIST - b) // V7X_SUBLANES
            rows = {g: cbuf[s, pl.ds(tiles * (r0 + CONV_PAD + V7X_SUBLANES * g - b) + c, V7X_SUBLANES, stride=tiles), :]
                    for g in range(-a_max, groups)}
            for a in range(a_max + 1):
                w = wbc[CONV_HIST - (V7X_SUBLANES * a + b), :, lanes]
                for m in range(groups):
                    acc[m] = acc[m] + rows[m - a] * w
        dconv[pl.ds(out_row0, conv_rows), lanes] = jnp.concatenate(acc, axis=0)


def _pooled(pext, L, pos1):
    pool_group = pext.shape[-1] // len(POOL_WINDOWS)
    out = []
    for gi, w in enumerate(POOL_WINDOWS):
        lanes = slice(gi * pool_group, (gi + 1) * pool_group)
        assert w & (w - 1) == 0 and w - 1 <= POOL_PAD
        total = pext[:, :, lanes]
        k = 1
        while k < w:
            total = total + jnp.concatenate([total[:, :k], total[:, :-k]], axis=1)
            k *= 2
        frame = pext[:, POOL_PAD:POOL_PAD + L, lanes]
        cnt = jnp.minimum(pos1, w).astype(F32)
        out.append(total[:, POOL_PAD:POOL_PAD + L] / cnt[None] - frame)
    return jnp.concatenate(out, axis=-1)


def _merge(h, dconv, pooled_bf16, gate_piece, n_pieces, clg_ref, clb_ref, w_cp_ref, w_pg_ref, psc_ref, w_pp_ref,
           w_out_ref, l2g_ref, l2b_ref, alpha):
    conv_act = _layer_norm(dconv, clg_ref[...], clb_ref[...])
    conv_act = conv_act * jax.nn.sigmoid(conv_act)
    branch_conv = _dot(conv_act.astype(BF16), w_cp_ref[...])
    pool_group = w_pg_ref.shape[-1]
    groups = []
    for gi in range(len(POOL_WINDOWS)):
        lanes = slice(gi * pool_group, (gi + 1) * pool_group)
        groups.append(_dot(pooled_bf16[:, lanes], w_pg_ref[gi]) * psc_ref[:, lanes])
    branch_pool = _dot(jnp.concatenate(groups, axis=-1).astype(BF16), w_pp_ref[...])
    half = n_pieces // 2
    gcols = branch_conv.shape[-1] // half
    merged = jnp.concatenate(
        [gate_piece(p) * branch_conv[:, p * gcols:(p + 1) * gcols]
         + gate_piece(half + p) * branch_pool[:, p * gcols:(p + 1) * gcols] for p in range(half)], axis=-1)
    mixed = _dot(merged.astype(BF16), w_out_ref[...])
    return _layer_norm(alpha * h + mixed, l2g_ref[...], l2b_ref[...])


def _mixer_prompt_kernel(h_ref, w_in_ref, w_gate_ref, b_gate_ref, cw_ref, cb_ref, clg_ref, clb_ref, w_cp_ref,
                         w_pg_ref, psc_ref, w_pp_ref, w_out_ref, l2g_ref, l2b_ref,
                         h2_ref, nc_ref, np_ref, cbuf, pext, wbc, dconv, *, alpha):
    t_idx = pl.program_id(1)
    L, d_model = h_ref.shape[1:]
    d_conv = wbc.shape[-1]
    d_pool = pext.shape[-1]
    hist_rows = cbuf.shape[1] - (d_conv // V7X_LANES) * L

    @pl.when(t_idx == 0)
    def _():
        cbuf[:, :hist_rows, :] = jnp.zeros((1, hist_rows, V7X_LANES), F32)
        pext[:, :POOL_PAD, :] = jnp.zeros((1, POOL_PAD, d_pool), F32)
        _broadcast_taps(wbc, cw_ref)

    @pl.when(t_idx > 0)
    def _():
        cbuf[:, :hist_rows, :] = cbuf[:, cbuf.shape[1] - hist_rows:, :]
        pext[:, :POOL_PAD, :] = pext[:, L:L + POOL_PAD, :]

    h = h_ref[0]
    hb = h.astype(BF16)
    u = _dot(hb, w_in_ref[...])
    gates = jax.nn.sigmoid(_dot(hb, w_gate_ref[...]) + b_gate_ref[...])
    glu = u[:, :d_conv] * jax.nn.sigmoid(u[:, d_conv:2 * d_conv])
    _store_conv_input(cbuf, glu.reshape(1, L, d_conv), 0)
    pext[:, POOL_PAD:, :] = u[:, 2 * d_conv:].reshape(1, L, d_pool)
    row = lax.broadcasted_iota(jnp.int32, (L, d_pool // len(POOL_WINDOWS)), 0)
    pooled = _pooled(pext, L, row + 1 + t_idx * L).reshape(L, d_pool).astype(BF16)

    def conv_trip(i, carry):
        r0 = pl.multiple_of(i * CONV_ROWS, CONV_ROWS)
        _conv_chunk(cbuf, wbc, cb_ref, dconv, 0, r0, r0, CONV_ROWS)
        return carry

    lax.fori_loop(0, L // CONV_ROWS, conv_trip, 0)

    n_pieces = 2 * d_model // V7X_MXU_COLS
    gate_piece = lambda p: gates[:, p * V7X_MXU_COLS:(p + 1) * V7X_MXU_COLS]
    h2_ref[0] = _merge(h, dconv[...], pooled, gate_piece, n_pieces, clg_ref, clb_ref, w_cp_ref, w_pg_ref, psc_ref,
                       w_pp_ref, w_out_ref, l2g_ref, l2b_ref, alpha)

    @pl.when(t_idx == pl.num_programs(1) - 1)
    def _():
        nc_ref[0] = glu[L - CONV_HIST:, :].reshape(1, CONV_HIST, d_conv)
        np_ref[0] = pext[:, POOL_PAD + L - POOL_HIST:POOL_PAD + L, :]


def _mixer_prompt(h, weights, alpha):
    (w_in, w_gate, b_gate, conv_w, conv_b, cln_g, cln_b, w_cp, w_pg, pscale, w_pp, w_out, l2g, l2b) = weights
    bsz, lseq, d = h.shape
    d_conv = conv_w.shape[1]
    d_pool = w_pp.shape[0]
    L = TOKEN_TILE
    assert lseq % L == 0 and L % CONV_ROWS == 0 and L >= CONV_HIST and L >= POOL_HIST
    hblock = (1, L, d)
    hmap = lambda b, t: (b, t, 0)
    state_map = lambda b, t: (0, b, 0, 0)
    scratch = [((1, (d_conv // V7X_LANES) * (CONV_PAD + L), V7X_LANES), F32), ((1, POOL_PAD + L, d_pool), F32),
               ((CONV_WIDTH, V7X_SUBLANES, d_conv), F32), ((L, d_conv), F32)]
    w_list = [w_in, w_gate, b_gate, conv_w, conv_b, cln_g, cln_b, w_cp, w_pg, pscale, w_pp, w_out, l2g, l2b]
    vmem = (sum(_nbytes(w.shape, w.dtype) for w in w_list)
            + 2 * 2 * _nbytes(hblock, F32)
            + 2 * 2 * (_nbytes((1, CONV_HIST, d_conv), F32) + _nbytes((1, POOL_HIST, d_pool), F32))
            + sum(_nbytes(shape, dtype) for shape, dtype in scratch)
            + _nbytes((L, w_in.shape[1]), F32) + _nbytes((L, 2 * d), F32)
            + 8 * _nbytes((L, d), F32))
    kern = functools.partial(_mixer_prompt_kernel, alpha=alpha)
    return pl.pallas_call(
        kern,
        grid=(bsz, lseq // L),
        in_specs=[pl.BlockSpec(hblock, hmap)] + [_resident(w.shape) for w in w_list],
        out_specs=[pl.BlockSpec(hblock, hmap), pl.BlockSpec((1, 1, CONV_HIST, d_conv), state_map),
                   pl.BlockSpec((1, 1, POOL_HIST, d_pool), state_map)],
        out_shape=[jax.ShapeDtypeStruct(h.shape, F32), jax.ShapeDtypeStruct((1, bsz, CONV_HIST, d_conv), F32),
                   jax.ShapeDtypeStruct((1, bsz, POOL_HIST, d_pool), F32)],
        scratch_shapes=[pltpu.VMEM(shape, dtype) for shape, dtype in scratch],
        compiler_params=pltpu.CompilerParams(dimension_semantics=("arbitrary", "arbitrary"),
                                             vmem_limit_bytes=_vmem_limit(vmem)),
        name="mixer_prompt",
    )(h, *w_list)


def _mixer_state_kernel(h_ref, cst_ref, pst_ref, w_in_ref, w_gate_ref, b_gate_ref, cw_ref, cb_ref, clg_ref, clb_ref,
                        w_cp_ref, w_pg_ref, psc_ref, w_pp_ref, w_out_ref, l2g_ref, l2b_ref,
                        h2_ref, nc_ref, np_ref, cbuf, pext, wbc, dconv, *, pos0, alpha):
    S, L, d_model = h_ref.shape
    N = S * L
    d_conv = wbc.shape[-1]
    d_pool = pext.shape[-1]

    h = h_ref[...].reshape(N, d_model)
    hb = h.astype(BF16)
    u = _dot(hb, w_in_ref[...])
    gates = jax.nn.sigmoid(_dot(hb, w_gate_ref[...]) + b_gate_ref[...])
    glu = u[:, :d_conv] * jax.nn.sigmoid(u[:, d_conv:2 * d_conv])

    _store_conv_input(cbuf, cst_ref[0], -CONV_HIST)
    pext[:, :POOL_PAD - POOL_HIST, :] = jnp.zeros((S, POOL_PAD - POOL_HIST, d_pool), F32)
    pext[:, POOL_PAD - POOL_HIST:POOL_PAD, :] = pst_ref[0]
    _broadcast_taps(wbc, cw_ref)
    glu3 = glu.reshape(S, L, d_conv)
    _store_conv_input(cbuf, glu3, 0)
    pext[:, POOL_PAD:, :] = u[:, 2 * d_conv:].reshape(S, L, d_pool)
    row = lax.broadcasted_iota(jnp.int32, (L, d_pool // len(POOL_WINDOWS)), 0)
    pooled = _pooled(pext, L, row + (pos0 + 1)).reshape(N, d_pool).astype(BF16)

    def conv_trip(s, carry):
        _conv_chunk(cbuf, wbc, cb_ref, dconv, s, 0, pl.multiple_of(s * L, L), L)
        return carry

    lax.fori_loop(0, S, conv_trip, 0)

    n_pieces = 2 * d_model // V7X_MXU_COLS
    gate_piece = lambda p: gates[:, p * V7X_MXU_COLS:(p + 1) * V7X_MXU_COLS]
    h2 = _merge(h, dconv[...], pooled, gate_piece, n_pieces, clg_ref, clb_ref, w_cp_ref, w_pg_ref, psc_ref, w_pp_ref,
                w_out_ref, l2g_ref, l2b_ref, alpha)
    h2_ref[...] = h2.reshape(S, L, d_model)
    nc_ref[0] = glu3[:, L - CONV_HIST:, :]
    np_ref[0] = pext[:, POOL_PAD + L - POOL_HIST:POOL_PAD + L, :]


def _mixer_state(h, conv_state, pool_state, weights, pos0, alpha):
    (w_in, w_gate, b_gate, conv_w, conv_b, cln_g, cln_b, w_cp, w_pg, pscale, w_pp, w_out, l2g, l2b) = weights
    bsz, L, d = h.shape
    d_conv = conv_w.shape[1]
    d_pool = w_pp.shape[0]
    S = STATE_SEQS_PER_STEP
    N = S * L
    assert bsz % S == 0 and L % V7X_SUBLANES == 0 and L >= CONV_HIST and L >= POOL_HIST

    hblock = (S, L, d)
    hmap = lambda i: (i, 0, 0)
    cstate_spec = pl.BlockSpec((1, S, CONV_HIST, d_conv), lambda i: (0, i, 0, 0))
    pstate_spec = pl.BlockSpec((1, S, POOL_HIST, d_pool), lambda i: (0, i, 0, 0))
    scratch = [((S, (d_conv // V7X_LANES) * (CONV_PAD + L), V7X_LANES), F32), ((S, POOL_PAD + L, d_pool), F32),
               ((CONV_WIDTH, V7X_SUBLANES, d_conv), F32), ((N, d_conv), F32)]
    w_list = [w_in, w_gate, b_gate, conv_w, conv_b, cln_g, cln_b, w_cp, w_pg, pscale, w_pp, w_out, l2g, l2b]
    vmem = (sum(_nbytes(w.shape, w.dtype) for w in w_list)
            + 2 * 2 * _nbytes(hblock, F32)
            + 2 * 2 * (_nbytes((S, CONV_HIST, d_conv), F32) + _nbytes((S, POOL_HIST, d_pool), F32))
            + sum(_nbytes(shape, dtype) for shape, dtype in scratch)
            + _nbytes((N, w_in.shape[1]), F32) + _nbytes((N, 2 * d), F32)
            + 8 * _nbytes((N, d), F32))
    kern = functools.partial(_mixer_state_kernel, pos0=pos0, alpha=alpha)
    return pl.pallas_call(
        kern,
        grid=(bsz // S,),
        in_specs=[pl.BlockSpec(hblock, hmap), cstate_spec, pstate_spec] + [_resident(w.shape) for w in w_list],
        out_specs=[pl.BlockSpec(hblock, hmap), cstate_spec, pstate_spec],
        out_shape=[jax.ShapeDtypeStruct(h.shape, F32), jax.ShapeDtypeStruct(conv_state.shape, F32),
                   jax.ShapeDtypeStruct(pool_state.shape, F32)],
        scratch_shapes=[pltpu.VMEM(shape, dtype) for shape, dtype in scratch],
        compiler_params=pltpu.CompilerParams(dimension_semantics=("arbitrary",), vmem_limit_bytes=_vmem_limit(vmem)),
        name="mixer_state",
    )(h, conv_state, pool_state, *w_list)


def kernel(x_prompt, x_sample, state_conv, state_pool, w_ffn1_gate, w_ffn1_up, w_ffn1_down, ln1_g, ln1_b, w_in, w_gate, b_gate, conv_w, conv_b, conv_ln_g, conv_ln_b, w_conv_proj, w_pool_group, pool_scale, w_pool_proj, w_out, ln2_g, ln2_b, w_ffn2_gate, w_ffn2_up, w_ffn2_down, ln3_g, ln3_b):
    depth = w_in.shape[0]
    bsz, seq, d = x_prompt.shape
    dbsz, dseq, _ = x_sample.shape
    alpha = (2.0 * depth) ** 0.25
    row = lambda v: v.reshape(1, -1)

    hp = x_prompt.reshape(bsz * seq, d)
    hs = x_sample.reshape(dbsz * dseq, d)
    conv_p, conv_s, pool_p, pool_s = [], [], [], []
    for l in range(depth):
        hp, hs = _ffn_ln(hp, hs, w_ffn1_gate[l], w_ffn1_up[l], w_ffn1_down[l], row(ln1_g[l]), row(ln1_b[l]), alpha)
        mix_w = (w_in[l].astype(BF16), w_gate[l].astype(BF16), row(b_gate[l]), conv_w[l], row(conv_b[l]),
                 row(conv_ln_g[l]), row(conv_ln_b[l]), w_conv_proj[l].astype(BF16), w_pool_group[l].astype(BF16),
                 row(pool_scale[l]), w_pool_proj[l].astype(BF16), w_out[l].astype(BF16), row(ln2_g[l]), row(ln2_b[l]))
        hp3, cp, pp = _mixer_prompt(hp.reshape(bsz, seq, d), mix_w, alpha)
        hs3, cs, ps = _mixer_state(hs.reshape(dbsz, dseq, d), state_conv[l:l + 1], state_pool[l:l + 1], mix_w, PAST_LEN, alpha)
        hp, hs = _ffn_ln(hp3.reshape(bsz * seq, d), hs3.reshape(dbsz * dseq, d), w_ffn2_gate[l], w_ffn2_up[l],
                         w_ffn2_down[l], row(ln3_g[l]), row(ln3_b[l]), alpha)
        conv_p.append(cp)
        conv_s.append(cs)
        pool_p.append(pp)
        pool_s.append(ps)
    cat = lambda xs: xs[0] if len(xs) == 1 else jnp.concatenate(xs, axis=0)
    return (hp.reshape(bsz, seq, d), hs.reshape(dbsz, dseq, d), cat(conv_p), cat(conv_s), cat(pool_p), cat(pool_s))
```

```python
import functools

import jax
import jax.numpy as jnp
from jax import lax
from jax.experimental import pallas as pl
from jax.experimental.pallas import tpu as pltpu

CONV_WIDTH = 31
CONV_HIST = CONV_WIDTH - 1
POOL_WINDOWS = (2, 4, 8, 16)
POOL_HIST = max(POOL_WINDOWS) - 1
LN_EPS = 1e-5
PAST_LEN = 4096

V7X_SUBLANES = 8
V7X_LANES = 128
V7X_MXU_COLS = 256
V7X_VMEM_BYTES = 64 * 1024 * 1024

TOKEN_TILE = 512
FFN_BLOCKS = (256, 256)
STATE_SEQS_PER_STEP = 8
CONV_ROWS = 128
CONV_PAD = 32
POOL_PAD = 16
WEIGHT_CHUNK = 128
STAGE_SLOTS = 4

BF16 = jnp.bfloat16
F32 = jnp.float32


def _layer_norm(z, g, b):
    mu = jnp.mean(z, axis=-1, keepdims=True)
    zc = z - mu
    var = jnp.mean(zc * zc, axis=-1, keepdims=True)
    return zc * lax.rsqrt(var + LN_EPS) * g + b


def _dot(a, b):
    return jnp.dot(a, b, preferred_element_type=F32)


def _resident(shape):
    zeros = (0,) * len(shape)
    return pl.BlockSpec(shape, lambda *_: zeros, pipeline_mode=pl.Buffered(1))


def _nbytes(shape, dtype):
    n = 1
    for s in shape:
        n *= s
    return n * jnp.dtype(dtype).itemsize


def _vmem_limit(estimate_bytes):
    assert estimate_bytes <= V7X_VMEM_BYTES, estimate_bytes
    return int(estimate_bytes)


def _stage_shape(w_shape):
    return (STAGE_SLOTS, WEIGHT_CHUNK, w_shape[1])


def _cast_weight(w_hbm, w_bf16, stage, sem):
    assert w_hbm.shape[0] % WEIGHT_CHUNK == 0 and stage.shape == _stage_shape(w_hbm.shape)
    n = w_hbm.shape[0] // WEIGHT_CHUNK
    piece = lambda i: pl.ds(i * WEIGHT_CHUNK, WEIGHT_CHUNK)

    def copy(i):
        return pltpu.make_async_copy(w_hbm.at[piece(i), :], stage.at[i % STAGE_SLOTS], sem.at[i % STAGE_SLOTS])

    ahead = STAGE_SLOTS - 1
    for i in range(min(ahead, n)):
        copy(i).start()
    for i in range(n):
        if i + ahead < n:
            copy(i + ahead).start()
        copy(i).wait()
        w_bf16[piece(i), :] = stage[i % STAGE_SLOTS].astype(BF16)


def _ffn_ln_kernel(xp_ref, xs_ref, wg_hbm, wu_hbm, wd_hbm, g_ref, b_ref, op_ref, os_ref,
                   wg_ref, wu_ref, wd_ref, stage_wide, stage_narrow, sem, *, n_sample_tiles, alpha):
    i = pl.program_id(0)

    @pl.when(i == 0)
    def _():
        _cast_weight(wg_hbm, wg_ref, stage_wide, sem)
        _cast_weight(wu_hbm, wu_ref, stage_wide, sem)
        _cast_weight(wd_hbm, wd_ref, stage_narrow, sem)

    is_sample = i < n_sample_tiles
    assert sum(FFN_BLOCKS) == xp_ref.shape[0]
    r = 0
    for rows in FFN_BLOCKS:
        x = jnp.where(is_sample, xs_ref[r:r + rows, :], xp_ref[r:r + rows, :])
        xb = x.astype(BF16)
        gate = _dot(xb, wg_ref[...])
        up = _dot(xb, wu_ref[...])
        act = (gate * jax.nn.sigmoid(gate) * up).astype(BF16)
        ffn = _dot(act, wd_ref[...])
        y = _layer_norm(alpha * x + 0.5 * ffn, g_ref[...], b_ref[...])
        op_ref[r:r + rows, :] = y
        os_ref[r:r + rows, :] = y
        r += rows


def _ffn_ln(xp, xs, wg, wu, wd, g, b, alpha):
    n_p, d = xp.shape
    n_s = xs.shape[0]
    d_ff = wg.shape[1]
    t = TOKEN_TILE
    assert n_p % t == 0 and n_s % t == 0
    npt, nst = n_p // t, n_s // t
    tile = (t, d)
    scratch = [(wg.shape, BF16), (wu.shape, BF16), (wd.shape, BF16), (_stage_shape(wg.shape), F32),
               (_stage_shape(wd.shape), F32)]
    vmem = (sum(_nbytes(shape, dtype) for shape, dtype in scratch)
            + 4 * 2 * _nbytes(tile, F32)
            + 2 * _nbytes((t, d_ff), F32) + _nbytes((t, d_ff), BF16)
            + 4 * _nbytes(tile, F32))
    kern = functools.partial(_ffn_ln_kernel, n_sample_tiles=nst, alpha=alpha)
    prompt_map = lambda i: (jnp.maximum(i - nst, 0), 0)
    yp, ys = pl.pallas_call(
        kern,
        grid=(nst + npt,),
        in_specs=[
            pl.BlockSpec(tile, prompt_map),
            pl.BlockSpec(tile, lambda i: (jnp.minimum(i, nst - 1), 0)),
            pl.BlockSpec(memory_space=pl.ANY), pl.BlockSpec(memory_space=pl.ANY), pl.BlockSpec(memory_space=pl.ANY),
            _resident(g.shape), _resident(b.shape),
        ],
        out_specs=[
            pl.BlockSpec(tile, prompt_map),
            pl.BlockSpec(tile, lambda i: (jnp.minimum(i, nst), 0)),
        ],
        out_shape=[jax.ShapeDtypeStruct(xp.shape, F32), jax.ShapeDtypeStruct((n_s + t, d), F32)],
        scratch_shapes=([pltpu.VMEM(shape, dtype) for shape, dtype in scratch]
                        + [pltpu.SemaphoreType.DMA((STAGE_SLOTS,))]),
        compiler_params=pltpu.CompilerParams(dimension_semantics=("arbitrary",), vmem_limit_bytes=_vmem_limit(vmem)),
        name="ffn_ln",
    )(xp, xs, wg, wu, wd, g, b)
    return yp, ys[:n_s]


def _store_conv_input(cbuf, glu3, first_frame):
    S, n, d_conv = glu3.shape
    tiles = d_conv // V7X_LANES
    for s in range(S):
        for c in range(tiles):
            rows = pl.ds(tiles * (CONV_PAD + first_frame) + c, n, stride=tiles)
            cbuf[s, rows, :] = glu3[s, :, c * V7X_LANES:(c + 1) * V7X_LANES]


def _broadcast_taps(wbc, cw_ref):
    for k in range(CONV_WIDTH):
        wbc[k] = jnp.broadcast_to(cw_ref[k:k + 1, :], wbc.shape[1:])


def _conv_chunk(cbuf, wbc, cb_ref, dconv, s, r0, out_row0, conv_rows):
    groups = conv_rows // V7X_SUBLANES
    tiles = wbc.shape[-1] // V7X_LANES
    for c in range(tiles):
        lanes = slice(c * V7X_LANES, (c + 1) * V7X_LANES)
        acc = [jnp.broadcast_to(cb_ref[:, lanes], (V7X_SUBLANES, V7X_LANES))] * groups
        for b in range(V7X_SUBLANES):
            a_max = (CONV_HIST - b) // V7X_SUBLANES
            rows = {g: cbuf[s, pl.ds(tiles * (r0 + CONV_PAD + V7X_SUBLANES * g - b) + c, V7X_SUBLANES, stride=tiles), :]
                    for g in range(-a_max, groups)}
            for a in range(a_max + 1):
                w = wbc[CONV_HIST - (V7X_SUBLANES * a + b), :, lanes]
                for m in range(groups):
                    acc[m] = acc[m] + rows[m - a] * w
        dconv[pl.ds(out_row0, conv_rows), lanes] = jnp.concatenate(acc, axis=0)


def _pooled(pext, L, pos1):
    pool_group = pext.shape[-1] // len(POOL_WINDOWS)
    out = []
    for gi, w in enumerate(POOL_WINDOWS):
        lanes = slice(gi * pool_group, (gi + 1) * pool_group)
        assert w & (w - 1) == 0 and w - 1 <= POOL_PAD
        total = pext[:, :, lanes]
        k = 1
        while k < w:
            total = total + jnp.concatenate([total[:, :k], total[:, :-k]], axis=1)
            k *= 2
        frame = pext[:, POOL_PAD:POOL_PAD + L, lanes]
        cnt = jnp.minimum(pos1, w).astype(F32)
        out.append(total[:, POOL_PAD:POOL_PAD + L] / cnt[None] - frame)
    return jnp.concatenate(out, axis=-1)


def _merge(h, dconv, pooled_bf16, gate_piece, n_pieces, clg_ref, clb_ref, w_cp_ref, w_pg_ref, psc_ref, w_pp_ref,
           w_out_ref, l2g_ref, l2b_ref, alpha):
    conv_act = _layer_norm(dconv, clg_ref[...], clb_ref[...])
    conv_act = conv_act * jax.nn.sigmoid(conv_act)
    branch_conv = _dot(conv_act.astype(BF16), w_cp_ref[...])
    pool_group = w_pg_ref.shape[-1]
    groups = []
    for gi in range(len(POOL_WINDOWS)):
        lanes = slice(gi * pool_group, (gi + 1) * pool_group)
        groups.append(_dot(pooled_bf16[:, lanes], w_pg_ref[gi]) * psc_ref[:, lanes])
    branch_pool = _dot(jnp.concatenate(groups, axis=-1).astype(BF16), w_pp_ref[...])
    half = n_pieces // 2
    gcols = branch_conv.shape[-1] // half
    merged = jnp.concatenate(
        [gate_piece(p) * branch_conv[:, p * gcols:(p + 1) * gcols]
         + gate_piece(half + p) * branch_pool[:, p * gcols:(p + 1) * gcols] for p in range(half)], axis=-1)
    mixed = _dot(merged.astype(BF16), w_out_ref[...])
    return _layer_norm(alpha * h + mixed, l2g_ref[...], l2b_ref[...])


def _mixer_prompt_kernel(h_ref, w_in_ref, w_gate_ref, b_gate_ref, cw_ref, cb_ref, clg_ref, clb_ref, w_cp_ref,
                         w_pg_ref, psc_ref, w_pp_ref, w_out_ref, l2g_ref, l2b_ref,
                         h2_ref, nc_ref, np_ref, cbuf, pext, wbc, dconv, *, alpha):
    t_idx = pl.program_id(1)
    L, d_model = h_ref.shape[1:]
    d_conv = wbc.shape[-1]
    d_pool = pext.shape[-1]
    hist_rows = cbuf.shape[1] - (d_conv // V7X_LANES) * L

    @pl.when(t_idx == 0)
    def _():
        cbuf[:, :hist_rows, :] = jnp.zeros((1, hist_rows, V7X_LANES), F32)
        pext[:, :POOL_PAD, :] = jnp.zeros((1, POOL_PAD, d_pool), F32)
        _broadcast_taps(wbc, cw_ref)

    @pl.when(t_idx > 0)
    def _():
        cbuf[:, :hist_rows, :] = cbuf[:, cbuf.shape[1] - hist_rows:, :]
        pext[:, :POOL_PAD, :] = pext[:, L:L + POOL_PAD, :]

    h = h_ref[0]
    hb = h.astype(BF16)
    u = _dot(hb, w_in_ref[...])
    gates = jax.nn.sigmoid(_dot(hb, w_gate_ref[...]) + b_gate_ref[...])
    glu = u[:, :d_conv] * jax.nn.sigmoid(u[:, d_conv:2 * d_conv])
    _store_conv_input(cbuf, glu.reshape(1, L, d_conv), 0)
    pext[:, POOL_PAD:, :] = u[:, 2 * d_conv:].reshape(1, L, d_pool)
    row = lax.broadcasted_iota(jnp.int32, (L, d_pool // len(POOL_WINDOWS)), 0)
    pooled = _pooled(pext, L, row + 1 + t_idx * L).reshape(L, d_pool).astype(BF16)

    def conv_trip(i, carry):
        r0 = pl.multiple_of(i * CONV_ROWS, CONV_ROWS)
        _conv_chunk(cbuf, wbc, cb_ref, dconv, 0, r0, r0, CONV_ROWS)
        return carry

    lax.fori_loop(0, L // CONV_ROWS, conv_trip, 0)

    n_pieces = 2 * d_model // V7X_MXU_COLS
    gate_piece = lambda p: gates[:, p * V7X_MXU_COLS:(p + 1) * V7X_MXU_COLS]
    h2_ref[0] = _merge(h, dconv[...], pooled, gate_piece, n_pieces, clg_ref, clb_ref, w_cp_ref, w_pg_ref, psc_ref,
                       w_pp_ref, w_out_ref, l2g_ref, l2b_ref, alpha)

    @pl.when(t_idx == pl.num_programs(1) - 1)
    def _():
        nc_ref[0] = glu[L - CONV_HIST:, :].reshape(1, CONV_HIST, d_conv)
        np_ref[0] = pext[:, POOL_PAD + L - POOL_HIST:POOL_PAD + L, :]


def _mixer_prompt(h, weights, alpha):
    (w_in, w_gate, b_gate, conv_w, conv_b, cln_g, cln_b, w_cp, w_pg, pscale, w_pp, w_out, l2g, l2b) = weights
    bsz, lseq, d = h.shape
    d_conv = conv_w.shape[1]
    d_pool = w_pp.shape[0]
    L = TOKEN_TILE
    assert lseq % L == 0 and L % CONV_ROWS == 0 and L >= CONV_HIST and L >= POOL_HIST
    hblock = (1, L, d)
    hmap = lambda b, t: (b, t, 0)
    state_map = lambda b, t: (0, b, 0, 0)
    scratch = [((1, (d_conv // V7X_LANES) * (CONV_PAD + L), V7X_LANES), F32), ((1, POOL_PAD + L, d_pool), F32),
               ((CONV_WIDTH, V7X_SUBLANES, d_conv), F32), ((L, d_conv), F32)]
    w_list = [w_in, w_gate, b_gate, conv_w, conv_b, cln_g, cln_b, w_cp, w_pg, pscale, w_pp, w_out, l2g, l2b]
    vmem = (sum(_nbytes(w.shape, w.dtype) for w in w_list)
            + 2 * 2 * _nbytes(hblock, F32)
            + 2 * 2 * (_nbytes((1, CONV_HIST, d_conv), F32) + _nbytes((1, POOL_HIST, d_pool), F32))
            + sum(_nbytes(shape, dtype) for shape, dtype in scratch)
            + _nbytes((L, w_in.shape[1]), F32) + _nbytes((L, 2 * d), F32)
            + 8 * _nbytes((L, d), F32))
    kern = functools.partial(_mixer_prompt_kernel, alpha=alpha)
    return pl.pallas_call(
        kern,
        grid=(bsz, lseq // L),
        in_specs=[pl.BlockSpec(hblock, hmap)] + [_resident(w.shape) for w in w_list],
        out_specs=[pl.BlockSpec(hblock, hmap), pl.BlockSpec((1, 1, CONV_HIST, d_conv), state_map),
                   pl.BlockSpec((1, 1, POOL_HIST, d_pool), state_map)],
        out_shape=[jax.ShapeDtypeStruct(h.shape, F32), jax.ShapeDtypeStruct((1, bsz, CONV_HIST, d_conv), F32),
                   jax.ShapeDtypeStruct((1, bsz, POOL_HIST, d_pool), F32)],
        scratch_shapes=[pltpu.VMEM(shape, dtype) for shape, dtype in scratch],
        compiler_params=pltpu.CompilerParams(dimension_semantics=("arbitrary", "arbitrary"),
                                             vmem_limit_bytes=_vmem_limit(vmem)),
        name="mixer_prompt",
    )(h, *w_list)


def _mixer_state_kernel(h_ref, cst_ref, pst_ref, w_in_ref, w_gate_ref, b_gate_ref, cw_ref, cb_ref, clg_ref, clb_ref,
                        w_cp_ref, w_pg_ref, psc_ref, w_pp_ref, w_out_ref, l2g_ref, l2b_ref,
                        h2_ref, nc_ref, np_ref, cbuf, pext, wbc, dconv, *, pos0, alpha):
    S, L, d_model = h_ref.shape
    N = S * L
    d_conv = wbc.shape[-1]
    d_pool = pext.shape[-1]

    h = h_ref[...].reshape(N, d_model)
    hb = h.astype(BF16)
    u = _dot(hb, w_in_ref[...])
    gates = jax.nn.sigmoid(_dot(hb, w_gate_ref[...]) + b_gate_ref[...])
    glu = u[:, :d_conv] * jax.nn.sigmoid(u[:, d_conv:2 * d_conv])

    _store_conv_input(cbuf, cst_ref[0], -CONV_HIST)
    pext[:, :POOL_PAD - POOL_HIST, :] = jnp.zeros((S, POOL_PAD - POOL_HIST, d_pool), F32)
    pext[:, POOL_PAD - POOL_HIST:POOL_PAD, :] = pst_ref[0]
    _broadcast_taps(wbc, cw_ref)
    glu3 = glu.reshape(S, L, d_conv)
    _store_conv_input(cbuf, glu3, 0)
    pext[:, POOL_PAD:, :] = u[:, 2 * d_conv:].reshape(S, L, d_pool)
    row = lax.broadcasted_iota(jnp.int32, (L, d_pool // len(POOL_WINDOWS)), 0)
    pooled = _pooled(pext, L, row + (pos0 + 1)).reshape(N, d_pool).astype(BF16)

    def conv_trip(s, carry):
        _conv_chunk(cbuf, wbc, cb_ref, dconv, s, 0, pl.multiple_of(s * L, L), L)
        return carry

    lax.fori_loop(0, S, conv_trip, 0)

    n_pieces = 2 * d_model // V7X_MXU_COLS
    gate_piece = lambda p: gates[:, p * V7X_MXU_COLS:(p + 1) * V7X_MXU_COLS]
    h2 = _merge(h, dconv[...], pooled, gate_piece, n_pieces, clg_ref, clb_ref, w_cp_ref, w_pg_ref, psc_ref, w_pp_ref,
                w_out_ref, l2g_ref, l2b_ref, alpha)
    h2_ref[...] = h2.reshape(S, L, d_model)
    nc_ref[0] = glu3[:, L - CONV_HIST:, :]
    np_ref[0] = pext[:, POOL_PAD + L - POOL_HIST:POOL_PAD + L, :]


def _mixer_state(h, conv_state, pool_state, weights, pos0, alpha):
    (w_in, w_gate, b_gate, conv_w, conv_b, cln_g, cln_b, w_cp, w_pg, pscale, w_pp, w_out, l2g, l2b) = weights
    bsz, L, d = h.shape
    d_conv = conv_w.shape[1]
    d_pool = w_pp.shape[0]
    S = STATE_SEQS_PER_STEP
    N = S * L
    assert bsz % S == 0 and L % V7X_SUBLANES == 0 and L >= CONV_HIST and L >= POOL_HIST

    hblock = (S, L, d)
    hmap = lambda i: (i, 0, 0)
    cstate_spec = pl.BlockSpec((1, S, CONV_HIST, d_conv), lambda i: (0, i, 0, 0))
    pstate_spec = pl.BlockSpec((1, S, POOL_HIST, d_pool), lambda i: (0, i, 0, 0))
    scratch = [((S, (d_conv // V7X_LANES) * (CONV_PAD + L), V7X_LANES), F32), ((S, POOL_PAD + L, d_pool), F32),
               ((CONV_WIDTH, V7X_SUBLANES, d_conv), F32), ((N, d_conv), F32)]
    w_list = [w_in, w_gate, b_gate, conv_w, conv_b, cln_g, cln_b, w_cp, w_pg, pscale, w_pp, w_out, l2g, l2b]
    vmem = (sum(_nbytes(w.shape, w.dtype) for w in w_list)
            + 2 * 2 * _nbytes(hblock, F32)
            + 2 * 2 * (_nbytes((S, CONV_HIST, d_conv), F32) + _nbytes((S, POOL_HIST, d_pool), F32))
            + sum(_nbytes(shape, dtype) for shape, dtype in scratch)
            + _nbytes((N, w_in.shape[1]), F32) + _nbytes((N, 2 * d), F32)
            + 8 * _nbytes((N, d), F32))
    kern = functools.partial(_mixer_state_kernel, pos0=pos0, alpha=alpha)
    return pl.pallas_call(
        kern,
        grid=(bsz // S,),
        in_specs=[pl.BlockSpec(hblock, hmap), cstate_spec, pstate_spec] + [_resident(w.shape) for w in w_list],
        out_specs=[pl.BlockSpec(hblock, hmap), cstate_spec, pstate_spec],
        out_shape=[jax.ShapeDtypeStruct(h.shape, F32), jax.ShapeDtypeStruct(conv_state.shape, F32),
                   jax.ShapeDtypeStruct(pool_state.shape, F32)],
        scratch_shapes=[pltpu.VMEM(shape, dtype) for shape, dtype in scratch],
        compiler_params=pltpu.CompilerParams(dimension_semantics=("arbitrary",), vmem_limit_bytes=_vmem_limit(vmem)),
        name="mixer_state",
    )(h, conv_state, pool_state, *w_list)


def kernel(x_prompt, x_sample, state_conv, state_pool, w_ffn1_gate, w_ffn1_up, w_ffn1_down, ln1_g, ln1_b, w_in, w_gate, b_gate, conv_w, conv_b, conv_ln_g, conv_ln_b, w_conv_proj, w_pool_group, pool_scale, w_pool_proj, w_out, ln2_g, ln2_b, w_ffn2_gate, w_ffn2_up, w_ffn2_down, ln3_g, ln3_b):
    depth = w_in.shape[0]
    bsz, seq, d = x_prompt.shape
    dbsz, dseq, _ = x_sample.shape
    alpha = (2.0 * depth) ** 0.25
    row = lambda v: v.reshape(1, -1)

    hp = x_prompt.reshape(bsz * seq, d)
    hs = x_sample.reshape(dbsz * dseq, d)
    conv_p, conv_s, pool_p, pool_s = [], [], [], []
    for l in range(depth):
        hp, hs = _ffn_ln(hp, hs, w_ffn1_gate[l], w_ffn1_up[l], w_ffn1_down[l], row(ln1_g[l]), row(ln1_b[l]), alpha)
        mix_w = (w_in[l].astype(BF16), w_gate[l].astype(BF16), row(b_gate[l]), conv_w[l], row(conv_b[l]),
                 row(conv_ln_g[l]), row(conv_ln_b[l]), w_conv_proj[l].astype(BF16), w_pool_group[l].astype(BF16),
                 row(pool_scale[l]), w_pool_proj[l].astype(BF16), w_out[l].astype(BF16), row(ln2_g[l]), row(ln2_b[l]))
        hp3, cp, pp = _mixer_prompt(hp.reshape(bsz, seq, d), mix_w, alpha)
        hs3, cs, ps = _mixer_state(hs.reshape(dbsz, dseq, d), state_conv[l:l + 1], state_pool[l:l + 1], mix_w, PAST_LEN, alpha)
        hp, hs = _ffn_ln(hp3.reshape(bsz * seq, d), hs3.reshape(dbsz * dseq, d), w_ffn2_gate[l], w_ffn2_up[l],
                         w_ffn2_down[l], row(ln3_g[l]), row(ln3_b[l]), alpha)
        conv_p.append(cp)
        conv_s.append(cs)
        pool_p.append(pp)
        pool_s.append(ps)
    cat = lambda xs: xs[0] if len(xs) == 1 else jnp.concatenate(xs, axis=0)
    return (hp.reshape(bsz, seq, d), hs.reshape(dbsz, dseq, d), cat(conv_p), cat(conv_s), cat(pool_p), cat(pool_s))
```

```python
import functools

import jax
import jax.numpy as jnp
from jax import lax
from jax.experimental import pallas as pl
from jax.experimental.pallas import tpu as pltpu

CONV_WIDTH = 31
CONV_HIST = CONV_WIDTH - 1
POOL_WINDOWS = (2, 4, 8, 16)
POOL_HIST = max(POOL_WINDOWS) - 1
LN_EPS = 1e-5
PAST_LEN = 4096

V7X_SUBLANES = 8
V7X_LANES = 128
V7X_MXU_COLS = 256
V7X_VMEM_BYTES = 64 * 1024 * 1024

TOKEN_TILE = 512
FFN_BLOCKS = (256, 256)
MERGE_ROWS = 256
STATE_SEQS_PER_STEP = 8
CONV_ROWS = 128
CONV_PAD = 32
POOL_PAD = 16
WEIGHT_CHUNK = 128
STAGE_SLOTS = 4

BF16 = jnp.bfloat16
F32 = jnp.float32


def _layer_norm(z, g, b):
    mu = jnp.mean(z, axis=-1, keepdims=True)
    zc = z - mu
    var = jnp.mean(zc * zc, axis=-1, keepdims=True)
    return zc * lax.rsqrt(var + LN_EPS) * g + b


def _dot(a, b):
    return jnp.dot(a, b, preferred_element_type=F32)


def _resident(shape):
    zeros = (0,) * len(shape)
    return pl.BlockSpec(shape, lambda *_: zeros, pipeline_mode=pl.Buffered(1))


def _nbytes(shape, dtype):
    n = 1
    for s in shape:
        n *= s
    return n * jnp.dtype(dtype).itemsize


def _vmem_limit(estimate_bytes):
    assert estimate_bytes <= V7X_VMEM_BYTES, estimate_bytes
    return int(estimate_bytes)


def _stage_shape(w_shape):
    return (STAGE_SLOTS, WEIGHT_CHUNK, w_shape[1])


def _cast_weight(w_hbm, w_bf16, stage, sem):
    assert w_hbm.shape[0] % WEIGHT_CHUNK == 0 and stage.shape == _stage_shape(w_hbm.shape)
    n = w_hbm.shape[0] // WEIGHT_CHUNK
    piece = lambda i: pl.ds(i * WEIGHT_CHUNK, WEIGHT_CHUNK)

    def copy(i):
        return pltpu.make_async_copy(w_hbm.at[piece(i), :], stage.at[i % STAGE_SLOTS], sem.at[i % STAGE_SLOTS])

    ahead = STAGE_SLOTS - 1
    for i in range(min(ahead, n)):
        copy(i).start()
    for i in range(n):
        if i + ahead < n:
            copy(i + ahead).start()
        copy(i).wait()
        w_bf16[piece(i), :] = stage[i % STAGE_SLOTS].astype(BF16)


def _ffn_ln_kernel(xp_ref, xs_ref, wg_hbm, wu_hbm, wd_hbm, g_ref, b_ref, op_ref, os_ref,
                   wg_ref, wu_ref, wd_ref, stage_wide, stage_narrow, sem, *, n_prompt_tiles, alpha):
    i = pl.program_id(0)

    @pl.when(i == 0)
    def _():
        _cast_weight(wg_hbm, wg_ref, stage_wide, sem)
        _cast_weight(wu_hbm, wu_ref, stage_wide, sem)
        _cast_weight(wd_hbm, wd_ref, stage_narrow, sem)

    is_prompt = i < n_prompt_tiles
    assert sum(FFN_BLOCKS) == xp_ref.shape[0]
    ys = []
    r = 0
    for rows in FFN_BLOCKS:
        x = jnp.where(is_prompt, xp_ref[r:r + rows, :], xs_ref[r:r + rows, :])
        r += rows
        xb = x.astype(BF16)
        gate = _dot(xb, wg_ref[...])
        up = _dot(xb, wu_ref[...])
        act = (gate * jax.nn.sigmoid(gate) * up).astype(BF16)
        ffn = _dot(act, wd_ref[...])
        ys.append(_layer_norm(alpha * x + 0.5 * ffn, g_ref[...], b_ref[...]))
    y = jnp.concatenate(ys, axis=0)

    @pl.when(is_prompt)
    def _():
        op_ref[...] = y

    @pl.when(jnp.logical_not(is_prompt))
    def _():
        os_ref[...] = y


def _ffn_ln(xp, xs, wg, wu, wd, g, b, alpha):
    n_p, d = xp.shape
    n_s = xs.shape[0]
    d_ff = wg.shape[1]
    t = TOKEN_TILE
    assert n_p % t == 0 and n_s % t == 0
    npt, nst = n_p // t, n_s // t
    tile = (t, d)
    scratch = [(wg.shape, BF16), (wu.shape, BF16), (wd.shape, BF16), (_stage_shape(wg.shape), F32),
               (_stage_shape(wd.shape), F32)]
    vmem = (sum(_nbytes(shape, dtype) for shape, dtype in scratch)
            + 4 * 2 * _nbytes(tile, F32)
            + 2 * _nbytes((t, d_ff), F32) + _nbytes((t, d_ff), BF16)
            + 4 * _nbytes(tile, F32))
    kern = functools.partial(_ffn_ln_kernel, n_prompt_tiles=npt, alpha=alpha)
    return pl.pallas_call(
        kern,
        grid=(npt + nst,),
        in_specs=[
            pl.BlockSpec(tile, lambda i: (jnp.minimum(i, npt - 1), 0)),
            pl.BlockSpec(tile, lambda i: (jnp.maximum(i - npt, 0), 0)),
            pl.BlockSpec(memory_space=pl.ANY), pl.BlockSpec(memory_space=pl.ANY), pl.BlockSpec(memory_space=pl.ANY),
            _resident(g.shape), _resident(b.shape),
        ],
        out_specs=[
            pl.BlockSpec(tile, lambda i: (jnp.minimum(i, npt - 1), 0)),
            pl.BlockSpec(tile, lambda i: (jnp.maximum(i - npt, 0), 0)),
        ],
        out_shape=[jax.ShapeDtypeStruct(xp.shape, F32), jax.ShapeDtypeStruct(xs.shape, F32)],
        scratch_shapes=([pltpu.VMEM(shape, dtype) for shape, dtype in scratch]
                        + [pltpu.SemaphoreType.DMA((STAGE_SLOTS,))]),
        compiler_params=pltpu.CompilerParams(dimension_semantics=("arbitrary",), vmem_limit_bytes=_vmem_limit(vmem)),
        name="ffn_ln",
    )(xp, xs, wg, wu, wd, g, b)


def _store_conv_input(cbuf, glu3, first_frame):
    S, n, d_conv = glu3.shape
    tiles = d_conv // V7X_LANES
    for s in range(S):
        for c in range(tiles):
            rows = pl.ds(tiles * (CONV_PAD + first_frame) + c, n, stride=tiles)
            cbuf[s, rows, :] = glu3[s, :, c * V7X_LANES:(c + 1) * V7X_LANES]


def _broadcast_taps(wbc, cw_ref):
    for k in range(CONV_WIDTH):
        wbc[k] = jnp.broadcast_to(cw_ref[k:k + 1, :], wbc.shape[1:])


def _conv_chunk(cbuf, wbc, cb_ref, dconv, s, r0, out_row0, conv_rows):
    groups = conv_rows // V7X_SUBLANES
    tiles = wbc.shape[-1] // V7X_LANES
    for c in range(tiles):
        lanes = slice(c * V7X_LANES, (c + 1) * V7X_LANES)
        acc = [jnp.broadcast_to(cb_ref[:, lanes], (V7X_SUBLANES, V7X_LANES))] * groups
        for b in range(V7X_SUBLANES):
            a_max = (CONV_HIST - b) // V7X_SUBLANES
            rows = {g: cbuf[s, pl.ds(tiles * (r0 + CONV_PAD + V7X_SUBLANES * g - b) + c, V7X_SUBLANES, stride=tiles), :]
                    for g in range(-a_max, groups)}
            for a in range(a_max + 1):
                w = wbc[CONV_HIST - (V7X_SUBLANES * a + b), :, lanes]
                for m in range(groups):
                    acc[m] = acc[m] + rows[m - a] * w
        dconv[pl.ds(out_row0, conv_rows), lanes] = jnp.concatenate(acc, axis=0)


def _pooled(pext, L, pos1):
    pool_group = pext.shape[-1] // len(POOL_WINDOWS)
    out = []
    for gi, w in enumerate(POOL_WINDOWS):
        lanes = slice(gi * pool_group, (gi + 1) * pool_group)
        assert w & (w - 1) == 0 and w - 1 <= POOL_PAD
        total = pext[:, :, lanes]
        k = 1
        while k < w:
            total = total + jnp.concatenate([total[:, :k], total[:, :-k]], axis=1)
            k *= 2
        frame = pext[:, POOL_PAD:POOL_PAD + L, lanes]
        cnt = jnp.minimum(pos1, w).astype(F32)
        out.append(total[:, POOL_PAD:POOL_PAD + L] / cnt[None] - frame)
    return jnp.concatenate(out, axis=-1)


def _merge(h, dconv, pooled_bf16, gate_piece, n_pieces, clg_ref, clb_ref, w_cp_ref, w_pg_ref, psc_ref, w_pp_ref,
           w_out_ref, l2g_ref, l2b_ref, alpha):
    conv_act = _layer_norm(dconv, clg_ref[...], clb_ref[...])
    conv_act = conv_act * jax.nn.sigmoid(conv_act)
    branch_conv = _dot(conv_act.astype(BF16), w_cp_ref[...])
    pool_group = w_pg_ref.shape[-1]
    groups = []
    for gi in range(len(POOL_WINDOWS)):
        lanes = slice(gi * pool_group, (gi + 1) * pool_group)
        groups.append(_dot(pooled_bf16[:, lanes], w_pg_ref[gi]) * psc_ref[:, lanes])
    branch_pool = _dot(jnp.concatenate(groups, axis=-1).astype(BF16), w_pp_ref[...])
    half = n_pieces // 2
    gcols = branch_conv.shape[-1] // half
    merged = jnp.concatenate(
        [gate_piece(p) * branch_conv[:, p * gcols:(p + 1) * gcols]
         + gate_piece(half + p) * branch_pool[:, p * gcols:(p + 1) * gcols] for p in range(half)], axis=-1)
    merged = merged.astype(BF16)
    n = h.shape[0]
    rows = min(n, MERGE_ROWS)
    assert n % rows == 0
    out = []
    for r in range(0, n, rows):
        mixed = _dot(merged[r:r + rows], w_out_ref[...])
        out.append(_layer_norm(alpha * h[r:r + rows] + mixed, l2g_ref[...], l2b_ref[...]))
    return jnp.concatenate(out, axis=0)


def _mixer_prompt_kernel(h_ref, w_in_ref, w_gate_ref, b_gate_ref, cw_ref, cb_ref, clg_ref, clb_ref, w_cp_ref,
                         w_pg_ref, psc_ref, w_pp_ref, w_out_ref, l2g_ref, l2b_ref,
                         h2_ref, nc_ref, np_ref, cbuf, pext, wbc, dconv, *, alpha):
    t_idx = pl.program_id(1)
    L, d_model = h_ref.shape[1:]
    d_conv = wbc.shape[-1]
    d_pool = pext.shape[-1]
    hist_rows = cbuf.shape[1] - (d_conv // V7X_LANES) * L

    @pl.when(t_idx == 0)
    def _():
        cbuf[:, :hist_rows, :] = jnp.zeros((1, hist_rows, V7X_LANES), F32)
        pext[:, :POOL_PAD, :] = jnp.zeros((1, POOL_PAD, d_pool), F32)
        _broadcast_taps(wbc, cw_ref)

    @pl.when(t_idx > 0)
    def _():
        cbuf[:, :hist_rows, :] = cbuf[:, cbuf.shape[1] - hist_rows:, :]
        pext[:, :POOL_PAD, :] = pext[:, L:L + POOL_PAD, :]

    h = h_ref[0]
    hb = h.astype(BF16)
    u = _dot(hb, w_in_ref[...])
    gates = jax.nn.sigmoid(_dot(hb, w_gate_ref[...]) + b_gate_ref[...])
    glu = u[:, :d_conv] * jax.nn.sigmoid(u[:, d_conv:2 * d_conv])
    _store_conv_input(cbuf, glu.reshape(1, L, d_conv), 0)
    pext[:, POOL_PAD:, :] = u[:, 2 * d_conv:].reshape(1, L, d_pool)
    row = lax.broadcasted_iota(jnp.int32, (L, d_pool // len(POOL_WINDOWS)), 0)
    pooled = _pooled(pext, L, row + 1 + t_idx * L).reshape(L, d_pool).astype(BF16)

    def conv_trip(i, carry):
        r0 = pl.multiple_of(i * CONV_ROWS, CONV_ROWS)
        _conv_chunk(cbuf, wbc, cb_ref, dconv, 0, r0, r0, CONV_ROWS)
        return carry

    lax.fori_loop(0, L // CONV_ROWS, conv_trip, 0)

    n_pieces = 2 * d_model // V7X_MXU_COLS
    gate_piece = lambda p: gates[:, p * V7X_MXU_COLS:(p + 1) * V7X_MXU_COLS]
    h2_ref[0] = _merge(h, dconv[...], pooled, gate_piece, n_pieces, clg_ref, clb_ref, w_cp_ref, w_pg_ref, psc_ref,
                       w_pp_ref, w_out_ref, l2g_ref, l2b_ref, alpha)

    @pl.when(t_idx == pl.num_programs(1) - 1)
    def _():
        nc_ref[0] = glu[L - CONV_HIST:, :].reshape(1, CONV_HIST, d_conv)
        np_ref[0] = pext[:, POOL_PAD + L - POOL_HIST:POOL_PAD + L, :]


def _mixer_prompt(h, weights, alpha):
    (w_in, w_gate, b_gate, conv_w, conv_b, cln_g, cln_b, w_cp, w_pg, pscale, w_pp, w_out, l2g, l2b) = weights
    bsz, lseq, d = h.shape
    d_conv = conv_w.shape[1]
    d_pool = w_pp.shape[0]
    L = TOKEN_TILE
    assert lseq % L == 0 and L % CONV_ROWS == 0 and L >= CONV_HIST and L >= POOL_HIST
    hblock = (1, L, d)
    hmap = lambda b, t: (b, t, 0)
    state_map = lambda b, t: (0, b, 0, 0)
    scratch = [((1, (d_conv // V7X_LANES) * (CONV_PAD + L), V7X_LANES), F32), ((1, POOL_PAD + L, d_pool), F32),
               ((CONV_WIDTH, V7X_SUBLANES, d_conv), F32), ((L, d_conv), F32)]
    w_list = [w_in, w_gate, b_gate, conv_w, conv_b, cln_g, cln_b, w_cp, w_pg, pscale, w_pp, w_out, l2g, l2b]
    vmem = (sum(_nbytes(w.shape, w.dtype) for w in w_list)
            + 2 * 2 * _nbytes(hblock, F32)
            + 2 * 2 * (_nbytes((1, CONV_HIST, d_conv), F32) + _nbytes((1, POOL_HIST, d_pool), F32))
            + sum(_nbytes(shape, dtype) for shape, dtype in scratch)
            + _nbytes((L, w_in.shape[1]), F32) + _nbytes((L, 2 * d), F32)
            + 8 * _nbytes((L, d), F32))
    kern = functools.partial(_mixer_prompt_kernel, alpha=alpha)
    return pl.pallas_call(
        kern,
        grid=(bsz, lseq // L),
        in_specs=[pl.BlockSpec(hblock, hmap)] + [_resident(w.shape) for w in w_list],
        out_specs=[pl.BlockSpec(hblock, hmap), pl.BlockSpec((1, 1, CONV_HIST, d_conv), state_map),
                   pl.BlockSpec((1, 1, POOL_HIST, d_pool), state_map)],
        out_shape=[jax.ShapeDtypeStruct(h.shape, F32), jax.ShapeDtypeStruct((1, bsz, CONV_HIST, d_conv), F32),
                   jax.ShapeDtypeStruct((1, bsz, POOL_HIST, d_pool), F32)],
        scratch_shapes=[pltpu.VMEM(shape, dtype) for shape, dtype in scratch],
        compiler_params=pltpu.CompilerParams(dimension_semantics=("arbitrary", "arbitrary"),
                                             vmem_limit_bytes=_vmem_limit(vmem)),
        name="mixer_prompt",
    )(h, *w_list)


def _mixer_state_kernel(h_ref, cst_ref, pst_ref, w_in_ref, w_gate_ref, b_gate_ref, cw_ref, cb_ref, clg_ref, clb_ref,
                        w_cp_ref, w_pg_ref, psc_ref, w_pp_ref, w_out_ref, l2g_ref, l2b_ref,
                        h2_ref, nc_ref, np_ref, cbuf, pext, wbc, dconv, *, pos0, alpha):
    S, L, d_model = h_ref.shape
    N = S * L
    d_conv = wbc.shape[-1]
    d_pool = pext.shape[-1]

    h = h_ref[...].reshape(N, d_model)
    hb = h.astype(BF16)
    u = _dot(hb, w_in_ref[...])
    gates = jax.nn.sigmoid(_dot(hb, w_gate_ref[...]) + b_gate_ref[...])
    glu = u[:, :d_conv] * jax.nn.sigmoid(u[:, d_conv:2 * d_conv])

    _store_conv_input(cbuf, cst_ref[0], -CONV_HIST)
    pext[:, :POOL_PAD - POOL_HIST, :] = jnp.zeros((S, POOL_PAD - POOL_HIST, d_pool), F32)
    pext[:, POOL_PAD - POOL_HIST:POOL_PAD, :] = pst_ref[0]
    _broadcast_taps(wbc, cw_ref)
    glu3 = glu.reshape(S, L, d_conv)
    _store_conv_input(cbuf, glu3, 0)
    pext[:, POOL_PAD:, :] = u[:, 2 * d_conv:].reshape(S, L, d_pool)
    row = lax.broadcasted_iota(jnp.int32, (L, d_pool // len(POOL_WINDOWS)), 0)
    pooled = _pooled(pext, L, row + (pos0 + 1)).reshape(N, d_pool).astype(BF16)

    def conv_trip(s, carry):
        _conv_chunk(cbuf, wbc, cb_ref, dconv, s, 0, pl.multiple_of(s * L, L), L)
        return carry

    lax.fori_loop(0, S, conv_trip, 0)

    n_pieces = 2 * d_model // V7X_MXU_COLS
    gate_piece = lambda p: gates[:, p * V7X_MXU_COLS:(p + 1) * V7X_MXU_COLS]
    h2 = _merge(h, dconv[...], pooled, gate_piece, n_pieces, clg_ref, clb_ref, w_cp_ref, w_pg_ref, psc_ref, w_pp_ref,
                w_out_ref, l2g_ref, l2b_ref, alpha)
    h2_ref[...] = h2.reshape(S, L, d_model)
    nc_ref[0] = glu3[:, L - CONV_HIST:, :]
    np_ref[0] = pext[:, POOL_PAD + L - POOL_HIST:POOL_PAD + L, :]


def _mixer_state(h, conv_state, pool_state, weights, pos0, alpha):
    (w_in, w_gate, b_gate, conv_w, conv_b, cln_g, cln_b, w_cp, w_pg, pscale, w_pp, w_out, l2g, l2b) = weights
    bsz, L, d = h.shape
    d_conv = conv_w.shape[1]
    d_pool = w_pp.shape[0]
    S = STATE_SEQS_PER_STEP
    N = S * L
    assert bsz % S == 0 and L % V7X_SUBLANES == 0 and L >= CONV_HIST and L >= POOL_HIST

    hblock = (S, L, d)
    hmap = lambda i: (i, 0, 0)
    cstate_spec = pl.BlockSpec((1, S, CONV_HIST, d_conv), lambda i: (0, i, 0, 0))
    pstate_spec = pl.BlockSpec((1, S, POOL_HIST, d_pool), lambda i: (0, i, 0, 0))
    scratch = [((S, (d_conv // V7X_LANES) * (CONV_PAD + L), V7X_LANES), F32), ((S, POOL_PAD + L, d_pool), F32),
               ((CONV_WIDTH, V7X_SUBLANES, d_conv), F32), ((N, d_conv), F32)]
    w_list = [w_in, w_gate, b_gate, conv_w, conv_b, cln_g, cln_b, w_cp, w_pg, pscale, w_pp, w_out, l2g, l2b]
    vmem = (sum(_nbytes(w.shape, w.dtype) for w in w_list)
            + 2 * 2 * _nbytes(hblock, F32)
            + 2 * 2 * (_nbytes((S, CONV_HIST, d_conv), F32) + _nbytes((S, POOL_HIST, d_pool), F32))
            + sum(_nbytes(shape, dtype) for shape, dtype in scratch)
            + _nbytes((N, w_in.shape[1]), F32) + _nbytes((N, 2 * d), F32)
            + 8 * _nbytes((N, d), F32))
    kern = functools.partial(_mixer_state_kernel, pos0=pos0, alpha=alpha)
    return pl.pallas_call(
        kern,
        grid=(bsz // S,),
        in_specs=[pl.BlockSpec(hblock, hmap), cstate_spec, pstate_spec] + [_resident(w.shape) for w in w_list],
        out_specs=[pl.BlockSpec(hblock, hmap), cstate_spec, pstate_spec],
        out_shape=[jax.ShapeDtypeStruct(h.shape, F32), jax.ShapeDtypeStruct(conv_state.shape, F32),
                   jax.ShapeDtypeStruct(pool_state.shape, F32)],
        scratch_shapes=[pltpu.VMEM(shape, dtype) for shape, dtype in scratch],
        compiler_params=pltpu.CompilerParams(dimension_semantics=("arbitrary",), vmem_limit_bytes=_vmem_limit(vmem)),
        name="mixer_state",
    )(h, conv_state, pool_state, *w_list)


def kernel(x_prompt, x_sample, state_conv, state_pool, w_ffn1_gate, w_ffn1_up, w_ffn1_down, ln1_g, ln1_b, w_in, w_gate, b_gate, conv_w, conv_b, conv_ln_g, conv_ln_b, w_conv_proj, w_pool_group, pool_scale, w_pool_proj, w_out, ln2_g, ln2_b, w_ffn2_gate, w_ffn2_up, w_ffn2_down, ln3_g, ln3_b):
    depth = w_in.shape[0]
    bsz, seq, d = x_prompt.shape
    dbsz, dseq, _ = x_sample.shape
    alpha = (2.0 * depth) ** 0.25
    row = lambda v: v.reshape(1, -1)

    hp = x_prompt.reshape(bsz * seq, d)
    hs = x_sample.reshape(dbsz * dseq, d)
    conv_p, conv_s, pool_p, pool_s = [], [], [], []
    for l in range(depth):
        hp, hs = _ffn_ln(hp, hs, w_ffn1_gate[l], w_ffn1_up[l], w_ffn1_down[l], row(ln1_g[l]), row(ln1_b[l]), alpha)
        mix_w = (w_in[l].astype(BF16), w_gate[l].astype(BF16), row(b_gate[l]), conv_w[l], row(conv_b[l]),
                 row(conv_ln_g[l]), row(conv_ln_b[l]), w_conv_proj[l].astype(BF16), w_pool_group[l].astype(BF16),
                 row(pool_scale[l]), w_pool_proj[l].astype(BF16), w_out[l].astype(BF16), row(ln2_g[l]), row(ln2_b[l]))
        hp3, cp, pp = _mixer_prompt(hp.reshape(bsz, seq, d), mix_w, alpha)
        hs3, cs, ps = _mixer_state(hs.reshape(dbsz, dseq, d), state_conv[l:l + 1], state_pool[l:l + 1], mix_w, PAST_LEN, alpha)
        hp, hs = _ffn_ln(hp3.reshape(bsz * seq, d), hs3.reshape(dbsz * dseq, d), w_ffn2_gate[l], w_ffn2_up[l],
                         w_ffn2_down[l], row(ln3_g[l]), row(ln3_b[l]), alpha)
        conv_p.append(cp)
        conv_s.append(cs)
        pool_p.append(pp)
        pool_s.append(ps)
    cat = lambda xs: xs[0] if len(xs) == 1 else jnp.concatenate(xs, axis=0)
    return (hp.reshape(bsz, seq, d), hs.reshape(dbsz, dseq, d), cat(conv_p), cat(conv_s), cat(pool_p), cat(pool_s))
```

```python
import functools

import jax
import jax.numpy as jnp
from jax import lax
from jax.experimental import pallas as pl
from jax.experimental.pallas import tpu as pltpu

CONV_WIDTH = 31
CONV_HIST = CONV_WIDTH - 1
POOL_WINDOWS = (2, 4, 8, 16)
POOL_HIST = max(POOL_WINDOWS) - 1
LN_EPS = 1e-5
PAST_LEN = 4096

V7X_SUBLANES = 8
V7X_LANES = 128
V7X_MXU_COLS = 256
V7X_VMEM_BYTES = 64 * 1024 * 1024

TOKEN_TILE = 512
FFN_BLOCKS = (256, 256)
MERGE_ROWS = 256
STATE_SEQS_PER_STEP = 16
CONV_ROWS = 128
CONV_PAD = 32
POOL_PAD = 16
WEIGHT_CHUNK = 128
STAGE_SLOTS = 4

BF16 = jnp.bfloat16
F32 = jnp.float32


def _layer_norm(z, g, b):
    mu = jnp.mean(z, axis=-1, keepdims=True)
    zc = z - mu
    var = jnp.mean(zc * zc, axis=-1, keepdims=True)
    return zc * lax.rsqrt(var + LN_EPS) * g + b


def _dot(a, b):
    return jnp.dot(a, b, preferred_element_type=F32)


def _resident(shape):
    zeros = (0,) * len(shape)
    return pl.BlockSpec(shape, lambda *_: zeros, pipeline_mode=pl.Buffered(1))


def _nbytes(shape, dtype):
    n = 1
    for s in shape:
        n *= s
    return n * jnp.dtype(dtype).itemsize


def _vmem_limit(estimate_bytes):
    assert estimate_bytes <= V7X_VMEM_BYTES, estimate_bytes
    return int(estimate_bytes)


def _stage_shape(w_shape):
    return (STAGE_SLOTS, WEIGHT_CHUNK, w_shape[1])


def _cast_weight(w_hbm, w_bf16, stage, sem):
    assert w_hbm.shape[0] % WEIGHT_CHUNK == 0 and stage.shape == _stage_shape(w_hbm.shape)
    n = w_hbm.shape[0] // WEIGHT_CHUNK
    piece = lambda i: pl.ds(i * WEIGHT_CHUNK, WEIGHT_CHUNK)

    def copy(i):
        return pltpu.make_async_copy(w_hbm.at[piece(i), :], stage.at[i % STAGE_SLOTS], sem.at[i % STAGE_SLOTS])

    ahead = STAGE_SLOTS - 1
    for i in range(min(ahead, n)):
        copy(i).start()
    for i in range(n):
        if i + ahead < n:
            copy(i + ahead).start()
        copy(i).wait()
        w_bf16[piece(i), :] = stage[i % STAGE_SLOTS].astype(BF16)


def _ffn_ln_kernel(xp_ref, xs_ref, wg_hbm, wu_hbm, wd_hbm, g_ref, b_ref, op_ref, os_ref,
                   wg_ref, wu_ref, wd_ref, stage_wide, stage_narrow, sem, *, n_prompt_tiles, alpha):
    i = pl.program_id(0)

    @pl.when(i == 0)
    def _():
        _cast_weight(wg_hbm, wg_ref, stage_wide, sem)
        _cast_weight(wu_hbm, wu_ref, stage_wide, sem)
        _cast_weight(wd_hbm, wd_ref, stage_narrow, sem)

    is_prompt = i < n_prompt_tiles
    assert sum(FFN_BLOCKS) == xp_ref.shape[0]
    ys = []
    r = 0
    for rows in FFN_BLOCKS:
        x = jnp.where(is_prompt, xp_ref[r:r + rows, :], xs_ref[r:r + rows, :])
        r += rows
        xb = x.astype(BF16)
        gate = _dot(xb, wg_ref[...])
        up = _dot(xb, wu_ref[...])
        act = (gate * jax.nn.sigmoid(gate) * up).astype(BF16)
        ffn = _dot(act, wd_ref[...])
        ys.append(_layer_norm(alpha * x + 0.5 * ffn, g_ref[...], b_ref[...]))
    y = jnp.concatenate(ys, axis=0)

    @pl.when(is_prompt)
    def _():
        op_ref[...] = y

    @pl.when(jnp.logical_not(is_prompt))
    def _():
        os_ref[...] = y


def _ffn_ln(xp, xs, wg, wu, wd, g, b, alpha):
    n_p, d = xp.shape
    n_s = xs.shape[0]
    d_ff = wg.shape[1]
    t = TOKEN_TILE
    assert n_p % t == 0 and n_s % t == 0
    npt, nst = n_p // t, n_s // t
    tile = (t, d)
    scratch = [(wg.shape, BF16), (wu.shape, BF16), (wd.shape, BF16), (_stage_shape(wg.shape), F32),
               (_stage_shape(wd.shape), F32)]
    vmem = (sum(_nbytes(shape, dtype) for shape, dtype in scratch)
            + 4 * 2 * _nbytes(tile, F32)
            + 2 * _nbytes((t, d_ff), F32) + _nbytes((t, d_ff), BF16)
            + 4 * _nbytes(tile, F32))
    kern = functools.partial(_ffn_ln_kernel, n_prompt_tiles=npt, alpha=alpha)
    return pl.pallas_call(
        kern,
        grid=(npt + nst,),
        in_specs=[
            pl.BlockSpec(tile, lambda i: (jnp.minimum(i, npt - 1), 0)),
            pl.BlockSpec(tile, lambda i: (jnp.maximum(i - npt, 0), 0)),
            pl.BlockSpec(memory_space=pl.ANY), pl.BlockSpec(memory_space=pl.ANY), pl.BlockSpec(memory_space=pl.ANY),
            _resident(g.shape), _resident(b.shape),
        ],
        out_specs=[
            pl.BlockSpec(tile, lambda i: (jnp.minimum(i, npt - 1), 0)),
            pl.BlockSpec(tile, lambda i: (jnp.maximum(i - npt, 0), 0)),
        ],
        out_shape=[jax.ShapeDtypeStruct(xp.shape, F32), jax.ShapeDtypeStruct(xs.shape, F32)],
        scratch_shapes=([pltpu.VMEM(shape, dtype) for shape, dtype in scratch]
                        + [pltpu.SemaphoreType.DMA((STAGE_SLOTS,))]),
        compiler_params=pltpu.CompilerParams(dimension_semantics=("arbitrary",), vmem_limit_bytes=_vmem_limit(vmem)),
        name="ffn_ln",
    )(xp, xs, wg, wu, wd, g, b)


def _store_conv_input(cbuf, glu3, first_frame):
    S, n, d_conv = glu3.shape
    tiles = d_conv // V7X_LANES
    for s in range(S):
        for c in range(tiles):
            rows = pl.ds(tiles * (CONV_PAD + first_frame) + c, n, stride=tiles)
            cbuf[s, rows, :] = glu3[s, :, c * V7X_LANES:(c + 1) * V7X_LANES]


def _broadcast_taps(wbc, cw_ref):
    for k in range(CONV_WIDTH):
        wbc[k] = jnp.broadcast_to(cw_ref[k:k + 1, :], wbc.shape[1:])


def _conv_chunk(cbuf, wbc, cb_ref, dconv, s, r0, out_row0, conv_rows):
    groups = conv_rows // V7X_SUBLANES
    tiles = wbc.shape[-1] // V7X_LANES
    for c in range(tiles):
        lanes = slice(c * V7X_LANES, (c + 1) * V7X_LANES)
        acc = [jnp.broadcast_to(cb_ref[:, lanes], (V7X_SUBLANES, V7X_LANES))] * groups
        for b in range(V7X_SUBLANES):
            a_max = (CONV_HIST - b) // V7X_SUBLANES
            rows = {g: cbuf[s, pl.ds(tiles * (r0 + CONV_PAD + V7X_SUBLANES * g - b) + c, V7X_SUBLANES, stride=tiles), :]
                    for g in range(-a_max, groups)}
            for a in range(a_max + 1):
                w = wbc[CONV_HIST - (V7X_SUBLANES * a + b), :, lanes]
                for m in range(groups):
                    acc[m] = acc[m] + rows[m - a] * w
        dconv[pl.ds(out_row0, conv_rows), lanes] = jnp.concatenate(acc, axis=0)


def _pooled(pext, L, pos1):
    pool_group = pext.shape[-1] // len(POOL_WINDOWS)
    out = []
    for gi, w in enumerate(POOL_WINDOWS):
        lanes = slice(gi * pool_group, (gi + 1) * pool_group)
        assert w & (w - 1) == 0 and w - 1 <= POOL_PAD
        total = pext[:, :, lanes]
        k = 1
        while k < w:
            total = total + jnp.concatenate([total[:, :k], total[:, :-k]], axis=1)
            k *= 2
        frame = pext[:, POOL_PAD:POOL_PAD + L, lanes]
        cnt = jnp.minimum(pos1, w).astype(F32)
        out.append(total[:, POOL_PAD:POOL_PAD + L] / cnt[None] - frame)
    return jnp.concatenate(out, axis=-1)


def _merge(h, dconv, pooled_bf16, gate_piece, n_pieces, clg_ref, clb_ref, w_cp_ref, w_pg_ref, psc_ref, w_pp_ref,
           w_out_ref, l2g_ref, l2b_ref, alpha):
    conv_act = _layer_norm(dconv, clg_ref[...], clb_ref[...])
    conv_act = conv_act * jax.nn.sigmoid(conv_act)
    branch_conv = _dot(conv_act.astype(BF16), w_cp_ref[...])
    pool_group = w_pg_ref.shape[-1]
    groups = []
    for gi in range(len(POOL_WINDOWS)):
        lanes = slice(gi * pool_group, (gi + 1) * pool_group)
        groups.append(_dot(pooled_bf16[:, lanes], w_pg_ref[gi]) * psc_ref[:, lanes])
    branch_pool = _dot(jnp.concatenate(groups, axis=-1).astype(BF16), w_pp_ref[...])
    half = n_pieces // 2
    gcols = branch_conv.shape[-1] // half
    merged = jnp.concatenate(
        [gate_piece(p) * branch_conv[:, p * gcols:(p + 1) * gcols]
         + gate_piece(half + p) * branch_pool[:, p * gcols:(p + 1) * gcols] for p in range(half)], axis=-1)
    merged = merged.astype(BF16)
    n = h.shape[0]
    rows = min(n, MERGE_ROWS)
    assert n % rows == 0
    out = []
    for r in range(0, n, rows):
        mixed = _dot(merged[r:r + rows], w_out_ref[...])
        out.append(_layer_norm(alpha * h[r:r + rows] + mixed, l2g_ref[...], l2b_ref[...]))
    return jnp.concatenate(out, axis=0)


def _mixer_prompt_kernel(h_ref, w_in_ref, w_gate_ref, b_gate_ref, cw_ref, cb_ref, clg_ref, clb_ref, w_cp_ref,
                         w_pg_ref, psc_ref, w_pp_ref, w_out_ref, l2g_ref, l2b_ref,
                         h2_ref, nc_ref, np_ref, cbuf, pext, wbc, dconv, *, alpha):
    t_idx = pl.program_id(1)
    L, d_model = h_ref.shape[1:]
    d_conv = wbc.shape[-1]
    d_pool = pext.shape[-1]
    hist_rows = cbuf.shape[1] - (d_conv // V7X_LANES) * L

    @pl.when(t_idx == 0)
    def _():
        cbuf[:, :hist_rows, :] = jnp.zeros((1, hist_rows, V7X_LANES), F32)
        pext[:, :POOL_PAD, :] = jnp.zeros((1, POOL_PAD, d_pool), F32)
        _broadcast_taps(wbc, cw_ref)

    @pl.when(t_idx > 0)
    def _():
        cbuf[:, :hist_rows, :] = cbuf[:, cbuf.shape[1] - hist_rows:, :]
        pext[:, :POOL_PAD, :] = pext[:, L:L + POOL_PAD, :]

    h = h_ref[0]
    hb = h.astype(BF16)
    u = _dot(hb, w_in_ref[...])
    gates = jax.nn.sigmoid(_dot(hb, w_gate_ref[...]) + b_gate_ref[...])
    glu = u[:, :d_conv] * jax.nn.sigmoid(u[:, d_conv:2 * d_conv])
    _store_conv_input(cbuf, glu.reshape(1, L, d_conv), 0)
    pext[:, POOL_PAD:, :] = u[:, 2 * d_conv:].reshape(1, L, d_pool)
    row = lax.broadcasted_iota(jnp.int32, (L, d_pool // len(POOL_WINDOWS)), 0)
    pooled = _pooled(pext, L, row + 1 + t_idx * L).reshape(L, d_pool).astype(BF16)

    def conv_trip(i, carry):
        r0 = pl.multiple_of(i * CONV_ROWS, CONV_ROWS)
        _conv_chunk(cbuf, wbc, cb_ref, dconv, 0, r0, r0, CONV_ROWS)
        return carry

    lax.fori_loop(0, L // CONV_ROWS, conv_trip, 0)

    n_pieces = 2 * d_model // V7X_MXU_COLS
    gate_piece = lambda p: gates[:, p * V7X_MXU_COLS:(p + 1) * V7X_MXU_COLS]
    h2_ref[0] = _merge(h, dconv[...], pooled, gate_piece, n_pieces, clg_ref, clb_ref, w_cp_ref, w_pg_ref, psc_ref,
                       w_pp_ref, w_out_ref, l2g_ref, l2b_ref, alpha)

    @pl.when(t_idx == pl.num_programs(1) - 1)
    def _():
        nc_ref[0] = glu[L - CONV_HIST:, :].reshape(1, CONV_HIST, d_conv)
        np_ref[0] = pext[:, POOL_PAD + L - POOL_HIST:POOL_PAD + L, :]


def _mixer_prompt(h, weights, alpha):
    (w_in, w_gate, b_gate, conv_w, conv_b, cln_g, cln_b, w_cp, w_pg, pscale, w_pp, w_out, l2g, l2b) = weights
    bsz, lseq, d = h.shape
    d_conv = conv_w.shape[1]
    d_pool = w_pp.shape[0]
    L = TOKEN_TILE
    assert lseq % L == 0 and L % CONV_ROWS == 0 and L >= CONV_HIST and L >= POOL_HIST
    hblock = (1, L, d)
    hmap = lambda b, t: (b, t, 0)
    state_map = lambda b, t: (0, b, 0, 0)
    scratch = [((1, (d_conv // V7X_LANES) * (CONV_PAD + L), V7X_LANES), F32), ((1, POOL_PAD + L, d_pool), F32),
               ((CONV_WIDTH, V7X_SUBLANES, d_conv), F32), ((L, d_conv), F32)]
    w_list = [w_in, w_gate, b_gate, conv_w, conv_b, cln_g, cln_b, w_cp, w_pg, pscale, w_pp, w_out, l2g, l2b]
    vmem = (sum(_nbytes(w.shape, w.dtype) for w in w_list)
            + 2 * 2 * _nbytes(hblock, F32)
            + 2 * 2 * (_nbytes((1, CONV_HIST, d_conv), F32) + _nbytes((1, POOL_HIST, d_pool), F32))
            + sum(_nbytes(shape, dtype) for shape, dtype in scratch)
            + _nbytes((L, w_in.shape[1]), F32) + _nbytes((L, 2 * d), F32)
            + 8 * _nbytes((L, d), F32))
    kern = functools.partial(_mixer_prompt_kernel, alpha=alpha)
    return pl.pallas_call(
        kern,
        grid=(bsz, lseq // L),
        in_specs=[pl.BlockSpec(hblock, hmap)] + [_resident(w.shape) for w in w_list],
        out_specs=[pl.BlockSpec(hblock, hmap), pl.BlockSpec((1, 1, CONV_HIST, d_conv), state_map),
                   pl.BlockSpec((1, 1, POOL_HIST, d_pool), state_map)],
        out_shape=[jax.ShapeDtypeStruct(h.shape, F32), jax.ShapeDtypeStruct((1, bsz, CONV_HIST, d_conv), F32),
                   jax.ShapeDtypeStruct((1, bsz, POOL_HIST, d_pool), F32)],
        scratch_shapes=[pltpu.VMEM(shape, dtype) for shape, dtype in scratch],
        compiler_params=pltpu.CompilerParams(dimension_semantics=("arbitrary", "arbitrary"),
                                             vmem_limit_bytes=_vmem_limit(vmem)),
        name="mixer_prompt",
    )(h, *w_list)


def _mixer_state_kernel(h_ref, cst_ref, pst_ref, w_in_ref, w_gate_ref, b_gate_ref, cw_ref, cb_ref, clg_ref, clb_ref,
                        w_cp_ref, w_pg_ref, psc_ref, w_pp_ref, w_out_ref, l2g_ref, l2b_ref,
                        h2_ref, nc_ref, np_ref, cbuf, pext, wbc, dconv, *, pos0, alpha):
    S, L, d_model = h_ref.shape
    N = S * L
    d_conv = wbc.shape[-1]
    d_pool = pext.shape[-1]

    h = h_ref[...].reshape(N, d_model)
    hb = h.astype(BF16)
    u = _dot(hb, w_in_ref[...])
    gates = jax.nn.sigmoid(_dot(hb, w_gate_ref[...]) + b_gate_ref[...])
    glu = u[:, :d_conv] * jax.nn.sigmoid(u[:, d_conv:2 * d_conv])

    _store_conv_input(cbuf, cst_ref[0], -CONV_HIST)
    pext[:, :POOL_PAD - POOL_HIST, :] = jnp.zeros((S, POOL_PAD - POOL_HIST, d_pool), F32)
    pext[:, POOL_PAD - POOL_HIST:POOL_PAD, :] = pst_ref[0]
    _broadcast_taps(wbc, cw_ref)
    glu3 = glu.reshape(S, L, d_conv)
    _store_conv_input(cbuf, glu3, 0)
    pext[:, POOL_PAD:, :] = u[:, 2 * d_conv:].reshape(S, L, d_pool)
    row = lax.broadcasted_iota(jnp.int32, (L, d_pool // len(POOL_WINDOWS)), 0)
    pooled = _pooled(pext, L, row + (pos0 + 1)).reshape(N, d_pool).astype(BF16)

    def conv_trip(s, carry):
        _conv_chunk(cbuf, wbc, cb_ref, dconv, s, 0, pl.multiple_of(s * L, L), L)
        return carry

    lax.fori_loop(0, S, conv_trip, 0)

    n_pieces = 2 * d_model // V7X_MXU_COLS
    gate_piece = lambda p: gates[:, p * V7X_MXU_COLS:(p + 1) * V7X_MXU_COLS]
    h2 = _merge(h, dconv[...], pooled, gate_piece, n_pieces, clg_ref, clb_ref, w_cp_ref, w_pg_ref, psc_ref, w_pp_ref,
                w_out_ref, l2g_ref, l2b_ref, alpha)
    h2_ref[...] = h2.reshape(S, L, d_model)
    nc_ref[0] = glu3[:, L - CONV_HIST:, :]
    np_ref[0] = pext[:, POOL_PAD + L - POOL_HIST:POOL_PAD + L, :]


def _mixer_state(h, conv_state, pool_state, weights, pos0, alpha):
    (w_in, w_gate, b_gate, conv_w, conv_b, cln_g, cln_b, w_cp, w_pg, pscale, w_pp, w_out, l2g, l2b) = weights
    bsz, L, d = h.shape
    d_conv = conv_w.shape[1]
    d_pool = w_pp.shape[0]
    S = STATE_SEQS_PER_STEP
    N = S * L
    assert bsz % S == 0 and L % V7X_SUBLANES == 0 and L >= CONV_HIST and L >= POOL_HIST

    hblock = (S, L, d)
    hmap = lambda i: (i, 0, 0)
    cstate_spec = pl.BlockSpec((1, S, CONV_HIST, d_conv), lambda i: (0, i, 0, 0))
    pstate_spec = pl.BlockSpec((1, S, POOL_HIST, d_pool), lambda i: (0, i, 0, 0))
    scratch = [((S, (d_conv // V7X_LANES) * (CONV_PAD + L), V7X_LANES), F32), ((S, POOL_PAD + L, d_pool), F32),
               ((CONV_WIDTH, V7X_SUBLANES, d_conv), F32), ((N, d_conv), F32)]
    w_list = [w_in, w_gate, b_gate, conv_w, conv_b, cln_g, cln_b, w_cp, w_pg, pscale, w_pp, w_out, l2g, l2b]
    vmem = (sum(_nbytes(w.shape, w.dtype) for w in w_list)
            + 2 * 2 * _nbytes(hblock, F32)
            + 2 * 2 * (_nbytes((S, CONV_HIST, d_conv), F32) + _nbytes((S, POOL_HIST, d_pool), F32))
            + sum(_nbytes(shape, dtype) for shape, dtype in scratch)
            + _nbytes((N, w_in.shape[1]), F32) + _nbytes((N, 2 * d), F32)
            + 8 * _nbytes((N, d), F32))
    kern = functools.partial(_mixer_state_kernel, pos0=pos0, alpha=alpha)
    return pl.pallas_call(
        kern,
        grid=(bsz // S,),
        in_specs=[pl.BlockSpec(hblock, hmap), cstate_spec, pstate_spec] + [_resident(w.shape) for w in w_list],
        out_specs=[pl.BlockSpec(hblock, hmap), cstate_spec, pstate_spec],
        out_shape=[jax.ShapeDtypeStruct(h.shape, F32), jax.ShapeDtypeStruct(conv_state.shape, F32),
                   jax.ShapeDtypeStruct(pool_state.shape, F32)],
        scratch_shapes=[pltpu.VMEM(shape, dtype) for shape, dtype in scratch],
        compiler_params=pltpu.CompilerParams(dimension_semantics=("arbitrary",), vmem_limit_bytes=_vmem_limit(vmem)),
        name="mixer_state",
    )(h, conv_state, pool_state, *w_list)


def kernel(x_prompt, x_sample, state_conv, state_pool, w_ffn1_gate, w_ffn1_up, w_ffn1_down, ln1_g, ln1_b, w_in, w_gate, b_gate, conv_w, conv_b, conv_ln_g, conv_ln_b, w_conv_proj, w_pool_group, pool_scale, w_pool_proj, w_out, ln2_g, ln2_b, w_ffn2_gate, w_ffn2_up, w_ffn2_down, ln3_g, ln3_b):
    depth = w_in.shape[0]
    bsz, seq, d = x_prompt.shape
    dbsz, dseq, _ = x_sample.shape
    alpha = (2.0 * depth) ** 0.25
    row = lambda v: v.reshape(1, -1)

    hp = x_prompt.reshape(bsz * seq, d)
    hs = x_sample.reshape(dbsz * dseq, d)
    conv_p, conv_s, pool_p, pool_s = [], [], [], []
    for l in range(depth):
        hp, hs = _ffn_ln(hp, hs, w_ffn1_gate[l], w_ffn1_up[l], w_ffn1_down[l], row(ln1_g[l]), row(ln1_b[l]), alpha)
        mix_w = (w_in[l].astype(BF16), w_gate[l].astype(BF16), row(b_gate[l]), conv_w[l], row(conv_b[l]),
                 row(conv_ln_g[l]), row(conv_ln_b[l]), w_conv_proj[l].astype(BF16), w_pool_group[l].astype(BF16),
                 row(pool_scale[l]), w_pool_proj[l].astype(BF16), w_out[l].astype(BF16), row(ln2_g[l]), row(ln2_b[l]))
        hp3, cp, pp = _mixer_prompt(hp.reshape(bsz, seq, d), mix_w, alpha)
        hs3, cs, ps = _mixer_state(hs.reshape(dbsz, dseq, d), state_conv[l:l + 1], state_pool[l:l + 1], mix_w, PAST_LEN, alpha)
        hp, hs = _ffn_ln(hp3.reshape(bsz * seq, d), hs3.reshape(dbsz * dseq, d), w_ffn2_gate[l], w_ffn2_up[l],
                         w_ffn2_down[l], row(ln3_g[l]), row(ln3_b[l]), alpha)
        conv_p.append(cp)
        conv_s.append(cs)
        pool_p.append(pp)
        pool_s.append(ps)
    cat = lambda xs: xs[0] if len(xs) == 1 else jnp.concatenate(xs, axis=0)
    return (hp.reshape(bsz, seq, d), hs.reshape(dbsz, dseq, d), cat(conv_p), cat(conv_s), cat(pool_p), cat(pool_s))
```

```python
import functools

import jax
import jax.numpy as jnp
from jax import lax
from jax.experimental import pallas as pl
from jax.experimental.pallas import tpu as pltpu

CONV_WIDTH = 31
CONV_HIST = CONV_WIDTH - 1
POOL_WINDOWS = (2, 4, 8, 16)
POOL_HIST = max(POOL_WINDOWS) - 1
LN_EPS = 1e-5
PAST_LEN = 4096

V7X_SUBLANES = 8
V7X_LANES = 128
V7X_MXU_COLS = 256
V7X_VMEM_BYTES = 64 * 1024 * 1024

TOKEN_TILE = 512
FFN_BLOCKS = (256, 256)
MERGE_ROWS = 256
STATE_SEQS_PER_STEP = 8
CONV_ROWS = 128
CONV_PAD = 32
POOL_PAD = 16
WEIGHT_CHUNK = 128
STAGE_SLOTS = 4

BF16 = jnp.bfloat16
F32 = jnp.float32


def _layer_norm(z, g, b):
    mu = jnp.mean(z, axis=-1, keepdims=True)
    zc = z - mu
    var = jnp.mean(zc * zc, axis=-1, keepdims=True)
    return zc * lax.rsqrt(var + LN_EPS) * g + b


def _dot(a, b):
    return jnp.dot(a, b, preferred_element_type=F32)


def _resident(shape):
    zeros = (0,) * len(shape)
    return pl.BlockSpec(shape, lambda *_: zeros, pipeline_mode=pl.Buffered(1))


def _nbytes(shape, dtype):
    n = 1
    for s in shape:
        n *= s
    return n * jnp.dtype(dtype).itemsize


def _vmem_limit(estimate_bytes):
    assert estimate_bytes <= V7X_VMEM_BYTES, estimate_bytes
    return int(estimate_bytes)


def _stage_shape(w_shape):
    return (STAGE_SLOTS, WEIGHT_CHUNK, w_shape[1])


def _cast_weight(w_hbm, w_bf16, stage, sem):
    assert w_hbm.shape[0] % WEIGHT_CHUNK == 0 and stage.shape == _stage_shape(w_hbm.shape)
    n = w_hbm.shape[0] // WEIGHT_CHUNK
    piece = lambda i: pl.ds(i * WEIGHT_CHUNK, WEIGHT_CHUNK)

    def copy(i):
        return pltpu.make_async_copy(w_hbm.at[piece(i), :], stage.at[i % STAGE_SLOTS], sem.at[i % STAGE_SLOTS])

    ahead = STAGE_SLOTS - 1
    for i in range(min(ahead, n)):
        copy(i).start()
    for i in range(n):
        if i + ahead < n:
            copy(i + ahead).start()
        copy(i).wait()
        w_bf16[piece(i), :] = stage[i % STAGE_SLOTS].astype(BF16)


def _ffn_ln_kernel(xp_ref, xs_ref, wg_hbm, wu_hbm, wd_hbm, g_ref, b_ref, op_ref, os_ref,
                   wg_ref, wu_ref, wd_ref, stage_wide, stage_narrow, sem, *, n_prompt_tiles, alpha):
    i = pl.program_id(0)

    @pl.when(i == 0)
    def _():
        _cast_weight(wg_hbm, wg_ref, stage_wide, sem)
        _cast_weight(wu_hbm, wu_ref, stage_wide, sem)
        _cast_weight(wd_hbm, wd_ref, stage_narrow, sem)

    is_prompt = i < n_prompt_tiles
    assert sum(FFN_BLOCKS) == xp_ref.shape[0]
    ys = []
    r = 0
    for rows in FFN_BLOCKS:
        x = jnp.where(is_prompt, xp_ref[r:r + rows, :], xs_ref[r:r + rows, :])
        r += rows
        xb = x.astype(BF16)
        gate = _dot(xb, wg_ref[...])
        up = _dot(xb, wu_ref[...])
        act = (gate * jax.nn.sigmoid(gate) * up).astype(BF16)
        ffn = _dot(act, wd_ref[...])
        ys.append(_layer_norm(alpha * x + 0.5 * ffn, g_ref[...], b_ref[...]))
    y = jnp.concatenate(ys, axis=0)

    @pl.when(is_prompt)
    def _():
        op_ref[...] = y

    @pl.when(jnp.logical_not(is_prompt))
    def _():
        os_ref[...] = y


def _ffn_ln(xp, xs, wg, wu, wd, g, b, alpha):
    n_p, d = xp.shape
    n_s = xs.shape[0]
    d_ff = wg.shape[1]
    t = TOKEN_TILE
    assert n_p % t == 0 and n_s % t == 0
    npt, nst = n_p // t, n_s // t
    tile = (t, d)
    scratch = [(wg.shape, BF16), (wu.shape, BF16), (wd.shape, BF16), (_stage_shape(wg.shape), F32),
               (_stage_shape(wd.shape), F32)]
    vmem = (sum(_nbytes(shape, dtype) for shape, dtype in scratch)
            + 4 * 2 * _nbytes(tile, F32)
            + 2 * _nbytes((t, d_ff), F32) + _nbytes((t, d_ff), BF16)
            + 4 * _nbytes(tile, F32))
    kern = functools.partial(_ffn_ln_kernel, n_prompt_tiles=npt, alpha=alpha)
    return pl.pallas_call(
        kern,
        grid=(npt + nst,),
        in_specs=[
            pl.BlockSpec(tile, lambda i: (jnp.minimum(i, npt - 1), 0)),
            pl.BlockSpec(tile, lambda i: (jnp.maximum(i - npt, 0), 0)),
            pl.BlockSpec(memory_space=pl.ANY), pl.BlockSpec(memory_space=pl.ANY), pl.BlockSpec(memory_space=pl.ANY),
            _resident(g.shape), _resident(b.shape),
        ],
        out_specs=[
            pl.BlockSpec(tile, lambda i: (jnp.minimum(i, npt - 1), 0)),
            pl.BlockSpec(tile, lambda i: (jnp.maximum(i - npt, 0), 0)),
        ],
        out_shape=[jax.ShapeDtypeStruct(xp.shape, F32), jax.ShapeDtypeStruct(xs.shape, F32)],
        scratch_shapes=([pltpu.VMEM(shape, dtype) for shape, dtype in scratch]
                        + [pltpu.SemaphoreType.DMA((STAGE_SLOTS,))]),
        compiler_params=pltpu.CompilerParams(dimension_semantics=("arbitrary",), vmem_limit_bytes=_vmem_limit(vmem)),
        name="ffn_ln",
    )(xp, xs, wg, wu, wd, g, b)


def _store_conv_input(cbuf, glu3, first_frame):
    S, n, d_conv = glu3.shape
    tiles = d_conv // V7X_LANES
    for s in range(S):
        for c in range(tiles):
            rows = pl.ds(tiles * (CONV_PAD + first_frame) + c, n, stride=tiles)
            cbuf[s, rows, :] = glu3[s, :, c * V7X_LANES:(c + 1) * V7X_LANES]


def _broadcast_taps(wbc, cw_ref):
    for k in range(CONV_WIDTH):
        wbc[k] = jnp.broadcast_to(cw_ref[k:k + 1, :], wbc.shape[1:])


def _conv_chunk(cbuf, wbc, cb_ref, dconv, s, r0, out_row0, conv_rows):
    groups = conv_rows // V7X_SUBLANES
    tiles = wbc.shape[-1] // V7X_LANES
    for c in range(tiles):
        lanes = slice(c * V7X_LANES, (c + 1) * V7X_LANES)
        acc = [jnp.broadcast_to(cb_ref[:, lanes], (V7X_SUBLANES, V7X_LANES))] * groups
        for b in range(V7X_SUBLANES):
            a_max = (CONV_HIST - b) // V7X_SUBLANES
            rows = {g: cbuf[s, pl.ds(tiles * (r0 + CONV_PAD + V7X_SUBLANES * g - b) + c, V7X_SUBLANES, stride=tiles), :]
                    for g in range(-a_max, groups)}
            for a in range(a_max + 1):
                w = wbc[CONV_HIST - (V7X_SUBLANES * a + b), :, lanes]
                for m in range(groups):
                    acc[m] = acc[m] + rows[m - a] * w
        dconv[pl.ds(out_row0, conv_rows), lanes] = jnp.concatenate(acc, axis=0)


def _pooled(pext, L, pos1):
    pool_group = pext.shape[-1] // len(POOL_WINDOWS)
    out = []
    for gi, w in enumerate(POOL_WINDOWS):
        lanes = slice(gi * pool_group, (gi + 1) * pool_group)
        assert w & (w - 1) == 0 and w - 1 <= POOL_PAD
        total = pext[:, :, lanes]
        k = 1
        while k < w:
            total = total + jnp.concatenate([total[:, :k], total[:, :-k]], axis=1)
            k *= 2
        frame = pext[:, POOL_PAD:POOL_PAD + L, lanes]
        cnt = jnp.minimum(pos1, w).astype(F32)
        out.append(total[:, POOL_PAD:POOL_PAD + L] / cnt[None] - frame)
    return jnp.concatenate(out, axis=-1)


def _merge(h, dconv, pooled_bf16, gate_piece, n_pieces, clg_ref, clb_ref, w_cp_ref, w_pg_ref, psc_ref, w_pp_ref,
           w_out_ref, l2g_ref, l2b_ref, alpha):
    conv_act = _layer_norm(dconv, clg_ref[...], clb_ref[...])
    conv_act = conv_act * jax.nn.sigmoid(conv_act)
    branch_conv = _dot(conv_act.astype(BF16), w_cp_ref[...])
    pool_group = w_pg_ref.shape[-1]
    groups = []
    for gi in range(len(POOL_WINDOWS)):
        lanes = slice(gi * pool_group, (gi + 1) * pool_group)
        groups.append(_dot(pooled_bf16[:, lanes], w_pg_ref[gi]) * psc_ref[:, lanes])
    branch_pool = _dot(jnp.concatenate(groups, axis=-1).astype(BF16), w_pp_ref[...])
    half = n_pieces // 2
    gcols = branch_conv.shape[-1] // half
    merged = jnp.concatenate(
        [gate_piece(p) * branch_conv[:, p * gcols:(p + 1) * gcols]
         + gate_piece(half + p) * branch_pool[:, p * gcols:(p + 1) * gcols] for p in range(half)], axis=-1)
    merged = merged.astype(BF16)
    n = h.shape[0]
    rows = min(n, MERGE_ROWS)
    assert n % rows == 0
    out = []
    for r in range(0, n, rows):
        mixed = _dot(merged[r:r + rows], w_out_ref[...])
        out.append(_layer_norm(alpha * h[r:r + rows] + mixed, l2g_ref[...], l2b_ref[...]))
    return jnp.concatenate(out, axis=0)


def _mixer_prompt_kernel(h_ref, w_in_ref, w_gate_ref, b_gate_ref, cw_ref, cb_ref, clg_ref, clb_ref, w_cp_ref,
                         w_pg_ref, psc_ref, w_pp_ref, w_out_ref, l2g_ref, l2b_ref,
                         h2_ref, nc_ref, np_ref, cbuf, pext, wbc, dconv, *, alpha):
    t_idx = pl.program_id(1)
    L, d_model = h_ref.shape[1:]
    d_conv = wbc.shape[-1]
    d_pool = pext.shape[-1]
    hist_rows = cbuf.shape[1] - (d_conv // V7X_LANES) * L

    @pl.when(t_idx == 0)
    def _():
        cbuf[:, :hist_rows, :] = jnp.zeros((1, hist_rows, V7X_LANES), F32)
        pext[:, :POOL_PAD, :] = jnp.zeros((1, POOL_PAD, d_pool), F32)
        _broadcast_taps(wbc, cw_ref)

    @pl.when(t_idx > 0)
    def _():
        cbuf[:, :hist_rows, :] = cbuf[:, cbuf.shape[1] - hist_rows:, :]
        pext[:, :POOL_PAD, :] = pext[:, L:L + POOL_PAD, :]

    h = h_ref[0]
    hb = h.astype(BF16)
    u = _dot(hb, w_in_ref[...])
    gates = jax.nn.sigmoid(_dot(hb, w_gate_ref[...]) + b_gate_ref[...])
    glu = u[:, :d_conv] * jax.nn.sigmoid(u[:, d_conv:2 * d_conv])
    _store_conv_input(cbuf, glu.reshape(1, L, d_conv), 0)
    pext[:, POOL_PAD:, :] = u[:, 2 * d_conv:].reshape(1, L, d_pool)
    row = lax.broadcasted_iota(jnp.int32, (L, d_pool // len(POOL_WINDOWS)), 0)
    pooled = _pooled(pext, L, row + 1 + t_idx * L).reshape(L, d_pool).astype(BF16)

    def conv_trip(i, carry):
        r0 = pl.multiple_of(i * CONV_ROWS, CONV_ROWS)
        _conv_chunk(cbuf, wbc, cb_ref, dconv, 0, r0, r0, CONV_ROWS)
        return carry

    lax.fori_loop(0, L // CONV_ROWS, conv_trip, 0)

    n_pieces = 2 * d_model // V7X_MXU_COLS
    gate_piece = lambda p: gates[:, p * V7X_MXU_COLS:(p + 1) * V7X_MXU_COLS]
    h2_ref[0] = _merge(h, dconv[...], pooled, gate_piece, n_pieces, clg_ref, clb_ref, w_cp_ref, w_pg_ref, psc_ref,
                       w_pp_ref, w_out_ref, l2g_ref, l2b_ref, alpha)

    @pl.when(t_idx == pl.num_programs(1) - 1)
    def _():
        nc_ref[0] = glu[L - CONV_HIST:, :].reshape(1, CONV_HIST, d_conv)
        np_ref[0] = pext[:, POOL_PAD + L - POOL_HIST:POOL_PAD + L, :]


def _mixer_prompt(h, weights, alpha):
    (w_in, w_gate, b_gate, conv_w, conv_b, cln_g, cln_b, w_cp, w_pg, pscale, w_pp, w_out, l2g, l2b) = weights
    bsz, lseq, d = h.shape
    d_conv = conv_w.shape[1]
    d_pool = w_pp.shape[0]
    L = TOKEN_TILE
    assert lseq % L == 0 and L % CONV_ROWS == 0 and L >= CONV_HIST and L >= POOL_HIST
    hblock = (1, L, d)
    hmap = lambda b, t: (b, t, 0)
    state_map = lambda b, t: (0, b, 0, 0)
    scratch = [((1, (d_conv // V7X_LANES) * (CONV_PAD + L), V7X_LANES), F32), ((1, POOL_PAD + L, d_pool), F32),
               ((CONV_WIDTH, V7X_SUBLANES, d_conv), F32), ((L, d_conv), F32)]
    w_list = [w_in, w_gate, b_gate, conv_w, conv_b, cln_g, cln_b, w_cp, w_pg, pscale, w_pp, w_out, l2g, l2b]
    vmem = (sum(_nbytes(w.shape, w.dtype) for w in w_list)
            + 2 * 2 * _nbytes(hblock, F32)
            + 2 * 2 * (_nbytes((1, CONV_HIST, d_conv), F32) + _nbytes((1, POOL_HIST, d_pool), F32))
            + sum(_nbytes(shape, dtype) for shape, dtype in scratch)
            + _nbytes((L, w_in.shape[1]), F32) + _nbytes((L, 2 * d), F32)
            + 8 * _nbytes((L, d), F32))
    kern = functools.partial(_mixer_prompt_kernel, alpha=alpha)
    return pl.pallas_call(
        kern,
        grid=(bsz, lseq // L),
        in_specs=[pl.BlockSpec(hblock, hmap)] + [_resident(w.shape) for w in w_list],
        out_specs=[pl.BlockSpec(hblock, hmap), pl.BlockSpec((1, 1, CONV_HIST, d_conv), state_map),
                   pl.BlockSpec((1, 1, POOL_HIST, d_pool), state_map)],
        out_shape=[jax.ShapeDtypeStruct(h.shape, F32), jax.ShapeDtypeStruct((1, bsz, CONV_HIST, d_conv), F32),
                   jax.ShapeDtypeStruct((1, bsz, POOL_HIST, d_pool), F32)],
        scratch_shapes=[pltpu.VMEM(shape, dtype) for shape, dtype in scratch],
        compiler_params=pltpu.CompilerParams(dimension_semantics=("arbitrary", "arbitrary"),
                                             vmem_limit_bytes=_vmem_limit(vmem)),
        name="mixer_prompt",
    )(h, *w_list)


def _mixer_state_kernel(h_ref, cst_ref, pst_ref, w_in_ref, w_gate_ref, b_gate_ref, cw_ref, cb_ref, clg_ref, clb_ref,
                        w_cp_ref, w_pg_ref, psc_ref, w_pp_ref, w_out_ref, l2g_ref, l2b_ref,
                        h2_ref, nc_ref, np_ref, cbuf, pext, wbc, dconv, *, pos0, alpha):
    S, L, d_model = h_ref.shape
    N = S * L
    d_conv = wbc.shape[-1]
    d_pool = pext.shape[-1]

    h = h_ref[...].reshape(N, d_model)
    hb = h.astype(BF16)
    u = _dot(hb, w_in_ref[...])
    gates = jax.nn.sigmoid(_dot(hb, w_gate_ref[...]) + b_gate_ref[...])
    glu = u[:, :d_conv] * jax.nn.sigmoid(u[:, d_conv:2 * d_conv])

    _store_conv_input(cbuf, cst_ref[0], -CONV_HIST)
    pext[:, :POOL_PAD - POOL_HIST, :] = jnp.zeros((S, POOL_PAD - POOL_HIST, d_pool), F32)
    pext[:, POOL_PAD - POOL_HIST:POOL_PAD, :] = pst_ref[0]
    _broadcast_taps(wbc, cw_ref)
    glu3 = glu.reshape(S, L, d_conv)
    _store_conv_input(cbuf, glu3, 0)
    pext[:, POOL_PAD:, :] = u[:, 2 * d_conv:].reshape(S, L, d_pool)
    row = lax.broadcasted_iota(jnp.int32, (L, d_pool // len(POOL_WINDOWS)), 0)
    pooled = _pooled(pext, L, row + (pos0 + 1)).reshape(N, d_pool).astype(BF16)

    def conv_trip(s, carry):
        _conv_chunk(cbuf, wbc, cb_ref, dconv, s, 0, pl.multiple_of(s * L, L), L)
        return carry

    lax.fori_loop(0, S, conv_trip, 0)

    n_pieces = 2 * d_model // V7X_MXU_COLS
    gate_piece = lambda p: gates[:, p * V7X_MXU_COLS:(p + 1) * V7X_MXU_COLS]
    h2 = _merge(h, dconv[...], pooled, gate_piece, n_pieces, clg_ref, clb_ref, w_cp_ref, w_pg_ref, psc_ref, w_pp_ref,
                w_out_ref, l2g_ref, l2b_ref, alpha)
    h2_ref[...] = h2.reshape(S, L, d_model)
    nc_ref[0] = glu3[:, L - CONV_HIST:, :]
    np_ref[0] = pext[:, POOL_PAD + L - POOL_HIST:POOL_PAD + L, :]


def _mixer_state(h, conv_state, pool_state, weights, pos0, alpha):
    (w_in, w_gate, b_gate, conv_w, conv_b, cln_g, cln_b, w_cp, w_pg, pscale, w_pp, w_out, l2g, l2b) = weights
    bsz, L, d = h.shape
    d_conv = conv_w.shape[1]
    d_pool = w_pp.shape[0]
    S = STATE_SEQS_PER_STEP
    N = S * L
    assert bsz % S == 0 and L % V7X_SUBLANES == 0 and L >= CONV_HIST and L >= POOL_HIST

    hblock = (S, L, d)
    hmap = lambda i: (i, 0, 0)
    cstate_spec = pl.BlockSpec((1, S, CONV_HIST, d_conv), lambda i: (0, i, 0, 0))
    pstate_spec = pl.BlockSpec((1, S, POOL_HIST, d_pool), lambda i: (0, i, 0, 0))
    scratch = [((S, (d_conv // V7X_LANES) * (CONV_PAD + L), V7X_LANES), F32), ((S, POOL_PAD + L, d_pool), F32),
               ((CONV_WIDTH, V7X_SUBLANES, d_conv), F32), ((N, d_conv), F32)]
    w_list = [w_in, w_gate, b_gate, conv_w, conv_b, cln_g, cln_b, w_cp, w_pg, pscale, w_pp, w_out, l2g, l2b]
    vmem = (sum(_nbytes(w.shape, w.dtype) for w in w_list)
            + 2 * 2 * _nbytes(hblock, F32)
            + 2 * 2 * (_nbytes((S, CONV_HIST, d_conv), F32) + _nbytes((S, POOL_HIST, d_pool), F32))
            + sum(_nbytes(shape, dtype) for shape, dtype in scratch)
            + _nbytes((N, w_in.shape[1]), F32) + _nbytes((N, 2 * d), F32)
            + 8 * _nbytes((N, d), F32))
    kern = functools.partial(_mixer_state_kernel, pos0=pos0, alpha=alpha)
    return pl.pallas_call(
        kern,
        grid=(bsz // S,),
        in_specs=[pl.BlockSpec(hblock, hmap), cstate_spec, pstate_spec] + [_resident(w.shape) for w in w_list],
        out_specs=[pl.BlockSpec(hblock, hmap), cstate_spec, pstate_spec],
        out_shape=[jax.ShapeDtypeStruct(h.shape, F32), jax.ShapeDtypeStruct(conv_state.shape, F32),
                   jax.ShapeDtypeStruct(pool_state.shape, F32)],
        scratch_shapes=[pltpu.VMEM(shape, dtype) for shape, dtype in scratch],
        compiler_params=pltpu.CompilerParams(dimension_semantics=("arbitrary",), vmem_limit_bytes=_vmem_limit(vmem)),
        name="mixer_state",
    )(h, conv_state, pool_state, *w_list)


def kernel(x_prompt, x_sample, state_conv, state_pool, w_ffn1_gate, w_ffn1_up, w_ffn1_down, ln1_g, ln1_b, w_in, w_gate, b_gate, conv_w, conv_b, conv_ln_g, conv_ln_b, w_conv_proj, w_pool_group, pool_scale, w_pool_proj, w_out, ln2_g, ln2_b, w_ffn2_gate, w_ffn2_up, w_ffn2_down, ln3_g, ln3_b):
    depth = w_in.shape[0]
    bsz, seq, d = x_prompt.shape
    dbsz, dseq, _ = x_sample.shape
    alpha = (2.0 * depth) ** 0.25
    row = lambda v: v.reshape(1, -1)

    hp = x_prompt.reshape(bsz * seq, d)
    hs = x_sample.reshape(dbsz * dseq, d)
    conv_p, conv_s, pool_p, pool_s = [], [], [], []
    for l in range(depth):
        hp, hs = _ffn_ln(hp, hs, w_ffn1_gate[l], w_ffn1_up[l], w_ffn1_down[l], row(ln1_g[l]), row(ln1_b[l]), alpha)
        mix_w = (w_in[l].astype(BF16), w_gate[l].astype(BF16), row(b_gate[l]), conv_w[l], row(conv_b[l]),
                 row(conv_ln_g[l]), row(conv_ln_b[l]), w_conv_proj[l].astype(BF16), w_pool_group[l].astype(BF16),
                 row(pool_scale[l]), w_pool_proj[l].astype(BF16), w_out[l].astype(BF16), row(ln2_g[l]), row(ln2_b[l]))
        hp3, cp, pp = _mixer_prompt(hp.reshape(bsz, seq, d), mix_w, alpha)
        hs3, cs, ps = _mixer_state(hs.reshape(dbsz, dseq, d), state_conv[l:l + 1], state_pool[l:l + 1], mix_w, PAST_LEN, alpha)
        hp, hs = _ffn_ln(hp3.reshape(bsz * seq, d), hs3.reshape(dbsz * dseq, d), w_ffn2_gate[l], w_ffn2_up[l],
                         w_ffn2_down[l], row(ln3_g[l]), row(ln3_b[l]), alpha)
        conv_p.append(cp)
        conv_s.append(cs)
        pool_p.append(pp)
        pool_s.append(ps)
    cat = lambda xs: xs[0] if len(xs) == 1 else jnp.concatenate(xs, axis=0)
    return (hp.reshape(bsz, seq, d), hs.reshape(dbsz, dseq, d), cat(conv_p), cat(conv_s), cat(pool_p), cat(pool_s))
```

```python
import functools

import jax
import jax.numpy as jnp
from jax import lax
from jax.experimental import pallas as pl
from jax.experimental.pallas import tpu as pltpu

CONV_WIDTH = 31
CONV_HIST = CONV_WIDTH - 1
POOL_WINDOWS = (2, 4, 8, 16)
POOL_HIST = max(POOL_WINDOWS) - 1
LN_EPS = 1e-5
PAST_LEN = 4096

V7X_SUBLANES = 8
V7X_LANES = 128
V7X_MXU_COLS = 256
V7X_VMEM_BYTES = 64 * 1024 * 1024

TOKEN_TILE = 512
FFN_BLOCKS = (256, 256)
MERGE_ROWS = 256
STATE_SEQS_PER_STEP = 8
CONV_ROWS = 128
CONV_PAD = 32
POOL_PAD = 16
WEIGHT_CHUNK = 128
STAGE_SLOTS = 4

BF16 = jnp.bfloat16
F32 = jnp.float32


def _layer_norm(z, g, b):
    mu = jnp.mean(z, axis=-1, keepdims=True)
    zc = z - mu
    var = jnp.mean(zc * zc, axis=-1, keepdims=True)
    return zc * lax.rsqrt(var + LN_EPS) * g + b


def _dot(a, b):
    return jnp.dot(a, b, preferred_element_type=F32)


def _resident(shape):
    zeros = (0,) * len(shape)
    return pl.BlockSpec(shape, lambda *_: zeros, pipeline_mode=pl.Buffered(1))


def _nbytes(shape, dtype):
    n = 1
    for s in shape:
        n *= s
    return n * jnp.dtype(dtype).itemsize


def _vmem_limit(estimate_bytes):
    assert estimate_bytes <= V7X_VMEM_BYTES, estimate_bytes
    return int(estimate_bytes)


def _stage_shape(w_shape):
    return (STAGE_SLOTS, WEIGHT_CHUNK, w_shape[1])


def _cast_weight(w_hbm, w_bf16, stage, sem):
    assert w_hbm.shape[0] % WEIGHT_CHUNK == 0 and stage.shape == _stage_shape(w_hbm.shape)
    n = w_hbm.shape[0] // WEIGHT_CHUNK
    piece = lambda i: pl.ds(i * WEIGHT_CHUNK, WEIGHT_CHUNK)

    def copy(i):
        return pltpu.make_async_copy(w_hbm.at[piece(i), :], stage.at[i % STAGE_SLOTS], sem.at[i % STAGE_SLOTS])

    ahead = STAGE_SLOTS - 1
    for i in range(min(ahead, n)):
        copy(i).start()
    for i in range(n):
        if i + ahead < n:
            copy(i + ahead).start()
        copy(i).wait()
        w_bf16[piece(i), :] = stage[i % STAGE_SLOTS].astype(BF16)


def _ffn_ln_kernel(xp_ref, xs_ref, wg_hbm, wu_hbm, wd_hbm, g_ref, b_ref, *refs, n_prompt_tiles, alpha, joined):
    outs, (wg_ref, wu_ref, wd_ref, stage_wide, stage_narrow, sem) = refs[:-6], refs[-6:]
    i = pl.program_id(0)

    @pl.when(i == 0)
    def _():
        _cast_weight(wg_hbm, wg_ref, stage_wide, sem)
        _cast_weight(wu_hbm, wu_ref, stage_wide, sem)
        _cast_weight(wd_hbm, wd_ref, stage_narrow, sem)

    is_prompt = i < n_prompt_tiles
    assert sum(FFN_BLOCKS) == xp_ref.shape[0]
    ys = []
    r = 0
    for rows in FFN_BLOCKS:
        x = jnp.where(is_prompt, xp_ref[r:r + rows, :], xs_ref[r:r + rows, :])
        r += rows
        xb = x.astype(BF16)
        gate = _dot(xb, wg_ref[...])
        up = _dot(xb, wu_ref[...])
        act = (gate * jax.nn.sigmoid(gate) * up).astype(BF16)
        ffn = _dot(act, wd_ref[...])
        y = _layer_norm(alpha * x + 0.5 * ffn, g_ref[...], b_ref[...])
        if joined:
            outs[0][r - rows:r, :] = y
        ys.append(y)
    if joined:
        return
    op_ref, os_ref = outs
    y = jnp.concatenate(ys, axis=0)

    @pl.when(is_prompt)
    def _():
        op_ref[...] = y

    @pl.when(jnp.logical_not(is_prompt))
    def _():
        os_ref[...] = y


def _ffn_ln(xp, xs, wg, wu, wd, g, b, alpha, joined):
    n_p, d = xp.shape
    n_s = xs.shape[0]
    d_ff = wg.shape[1]
    t = TOKEN_TILE
    assert n_p % t == 0 and n_s % t == 0
    npt, nst = n_p // t, n_s // t
    tile = (t, d)
    scratch = [(wg.shape, BF16), (wu.shape, BF16), (wd.shape, BF16), (_stage_shape(wg.shape), F32),
               (_stage_shape(wd.shape), F32)]
    vmem = (sum(_nbytes(shape, dtype) for shape, dtype in scratch)
            + 4 * 2 * _nbytes(tile, F32)
            + 2 * _nbytes((t, d_ff), F32) + _nbytes((t, d_ff), BF16)
            + 4 * _nbytes(tile, F32))
    kern = functools.partial(_ffn_ln_kernel, n_prompt_tiles=npt, alpha=alpha, joined=joined)
    if joined:
        out_specs = [pl.BlockSpec(tile, lambda i: (i, 0))]
        out_shape = [jax.ShapeDtypeStruct((n_p + n_s, d), F32)]
    else:
        out_specs = [pl.BlockSpec(tile, lambda i: (jnp.minimum(i, npt - 1), 0)),
                     pl.BlockSpec(tile, lambda i: (jnp.maximum(i - npt, 0), 0))]
        out_shape = [jax.ShapeDtypeStruct(xp.shape, F32), jax.ShapeDtypeStruct(xs.shape, F32)]
    out = pl.pallas_call(
        kern,
        grid=(npt + nst,),
        in_specs=[
            pl.BlockSpec(tile, lambda i: (jnp.minimum(i, npt - 1), 0)),
            pl.BlockSpec(tile, lambda i: (jnp.maximum(i - npt, 0), 0)),
            pl.BlockSpec(memory_space=pl.ANY), pl.BlockSpec(memory_space=pl.ANY), pl.BlockSpec(memory_space=pl.ANY),
            _resident(g.shape), _resident(b.shape),
        ],
        out_specs=out_specs,
        out_shape=out_shape,
        scratch_shapes=([pltpu.VMEM(shape, dtype) for shape, dtype in scratch]
                        + [pltpu.SemaphoreType.DMA((STAGE_SLOTS,))]),
        compiler_params=pltpu.CompilerParams(dimension_semantics=("arbitrary",), vmem_limit_bytes=_vmem_limit(vmem)),
        name="ffn_ln",
    )(xp, xs, wg, wu, wd, g, b)
    return out[0] if joined else out


def _store_conv_input(cbuf, glu3, first_frame):
    S, n, d_conv = glu3.shape
    tiles = d_conv // V7X_LANES
    for s in range(S):
        for c in range(tiles):
            rows = pl.ds(tiles * (CONV_PAD + first_frame) + c, n, stride=tiles)
            cbuf[s, rows, :] = glu3[s, :, c * V7X_LANES:(c + 1) * V7X_LANES]


def _broadcast_taps(wbc, cw_ref):
    for k in range(CONV_WIDTH):
        wbc[k] = jnp.broadcast_to(cw_ref[k:k + 1, :], wbc.shape[1:])


def _conv_chunk(cbuf, wbc, cb_ref, dconv, s, r0, out_row0, conv_rows):
    groups = conv_rows // V7X_SUBLANES
    tiles = wbc.shape[-1] // V7X_LANES
    for c in range(tiles):
        lanes = slice(c * V7X_LANES, (c + 1) * V7X_LANES)
        acc = [jnp.broadcast_to(cb_ref[:, lanes], (V7X_SUBLANES, V7X_LANES))] * groups
        for b in range(V7X_SUBLANES):
            a_max = (CONV_HIST - b) // V7X_SUBLANES
            rows = {g: cbuf[s, pl.ds(tiles * (r0 + CONV_PAD + V7X_SUBLANES * g - b) + c, V7X_SUBLANES, stride=tiles), :]
                    for g in range(-a_max, groups)}
            for a in range(a_max + 1):
                w = wbc[CONV_HIST - (V7X_SUBLANES * a + b), :, lanes]
                for m in range(groups):
                    acc[m] = acc[m] + rows[m - a] * w
        dconv[pl.ds(out_row0, conv_rows), lanes] = jnp.concatenate(acc, axis=0)


def _pooled(pext, L, pos1):
    pool_group = pext.shape[-1] // len(POOL_WINDOWS)
    out = []
    for gi, w in enumerate(POOL_WINDOWS):
        lanes = slice(gi * pool_group, (gi + 1) * pool_group)
        assert w & (w - 1) == 0 and w - 1 <= POOL_PAD
        total = pext[:, :, lanes]
        k = 1
        while k < w:
            total = total + jnp.concatenate([total[:, :k], total[:, :-k]], axis=1)
            k *= 2
        frame = pext[:, POOL_PAD:POOL_PAD + L, lanes]
        cnt = jnp.minimum(pos1, w).astype(F32)
        out.append(total[:, POOL_PAD:POOL_PAD + L] / cnt[None] - frame)
    return jnp.concatenate(out, axis=-1)


def _merge(h, dconv, pooled_bf16, gate_piece, n_pieces, clg_ref, clb_ref, w_cp_ref, w_pg_ref, psc_ref, w_pp_ref,
           w_out_ref, l2g_ref, l2b_ref, alpha):
    conv_act = _layer_norm(dconv, clg_ref[...], clb_ref[...])
    conv_act = conv_act * jax.nn.sigmoid(conv_act)
    branch_conv = _dot(conv_act.astype(BF16), w_cp_ref[...])
    pool_group = w_pg_ref.shape[-1]
    groups = []
    for gi in range(len(POOL_WINDOWS)):
        lanes = slice(gi * pool_group, (gi + 1) * pool_group)
        groups.append(_dot(pooled_bf16[:, lanes], w_pg_ref[gi]) * psc_ref[:, lanes])
    branch_pool = _dot(jnp.concatenate(groups, axis=-1).astype(BF16), w_pp_ref[...])
    half = n_pieces // 2
    gcols = branch_conv.shape[-1] // half
    merged = jnp.concatenate(
        [gate_piece(p) * branch_conv[:, p * gcols:(p + 1) * gcols]
         + gate_piece(half + p) * branch_pool[:, p * gcols:(p + 1) * gcols] for p in range(half)], axis=-1)
    merged = merged.astype(BF16)
    n = h.shape[0]
    rows = min(n, MERGE_ROWS)
    assert n % rows == 0
    out = []
    for r in range(0, n, rows):
        mixed = _dot(merged[r:r + rows], w_out_ref[...])
        out.append(_layer_norm(alpha * h[r:r + rows] + mixed, l2g_ref[...], l2b_ref[...]))
    return jnp.concatenate(out, axis=0)


def _mixer_prompt_kernel(h_ref, w_in_ref, w_gate_ref, b_gate_ref, cw_ref, cb_ref, clg_ref, clb_ref, w_cp_ref,
                         w_pg_ref, psc_ref, w_pp_ref, w_out_ref, l2g_ref, l2b_ref,
                         h2_ref, nc_ref, np_ref, cbuf, pext, wbc, dconv, *, alpha):
    t_idx = pl.program_id(1)
    L, d_model = h_ref.shape
    d_conv = wbc.shape[-1]
    d_pool = pext.shape[-1]
    hist_rows = cbuf.shape[1] - (d_conv // V7X_LANES) * L

    @pl.when(t_idx == 0)
    def _():
        cbuf[:, :hist_rows, :] = jnp.zeros((1, hist_rows, V7X_LANES), F32)
        pext[:, :POOL_PAD, :] = jnp.zeros((1, POOL_PAD, d_pool), F32)
        _broadcast_taps(wbc, cw_ref)

    @pl.when(t_idx > 0)
    def _():
        cbuf[:, :hist_rows, :] = cbuf[:, cbuf.shape[1] - hist_rows:, :]
        pext[:, :POOL_PAD, :] = pext[:, L:L + POOL_PAD, :]

    h = h_ref[...]
    hb = h.astype(BF16)
    u = _dot(hb, w_in_ref[...])
    gates = jax.nn.sigmoid(_dot(hb, w_gate_ref[...]) + b_gate_ref[...])
    glu = u[:, :d_conv] * jax.nn.sigmoid(u[:, d_conv:2 * d_conv])
    _store_conv_input(cbuf, glu.reshape(1, L, d_conv), 0)
    pext[:, POOL_PAD:, :] = u[:, 2 * d_conv:].reshape(1, L, d_pool)
    row = lax.broadcasted_iota(jnp.int32, (L, d_pool // len(POOL_WINDOWS)), 0)
    pooled = _pooled(pext, L, row + 1 + t_idx * L).reshape(L, d_pool).astype(BF16)

    def conv_trip(i, carry):
        r0 = pl.multiple_of(i * CONV_ROWS, CONV_ROWS)
        _conv_chunk(cbuf, wbc, cb_ref, dconv, 0, r0, r0, CONV_ROWS)
        return carry

    lax.fori_loop(0, L // CONV_ROWS, conv_trip, 0)

    n_pieces = 2 * d_model // V7X_MXU_COLS
    gate_piece = lambda p: gates[:, p * V7X_MXU_COLS:(p + 1) * V7X_MXU_COLS]
    h2_ref[0] = _merge(h, dconv[...], pooled, gate_piece, n_pieces, clg_ref, clb_ref, w_cp_ref, w_pg_ref, psc_ref,
                       w_pp_ref, w_out_ref, l2g_ref, l2b_ref, alpha)

    @pl.when(t_idx == pl.num_programs(1) - 1)
    def _():
        nc_ref[0] = glu[L - CONV_HIST:, :].reshape(1, CONV_HIST, d_conv)
        np_ref[0] = pext[:, POOL_PAD + L - POOL_HIST:POOL_PAD + L, :]


def _mixer_prompt(h_all, bsz, lseq, weights, alpha):
    (w_in, w_gate, b_gate, conv_w, conv_b, cln_g, cln_b, w_cp, w_pg, pscale, w_pp, w_out, l2g, l2b) = weights
    d = h_all.shape[1]
    d_conv = conv_w.shape[1]
    d_pool = w_pp.shape[0]
    L = TOKEN_TILE
    assert lseq % L == 0 and L % CONV_ROWS == 0 and L >= CONV_HIST and L >= POOL_HIST
    hblock = (1, L, d)
    hmap = lambda b, t: (b, t, 0)
    state_map = lambda b, t: (0, b, 0, 0)
    scratch = [((1, (d_conv // V7X_LANES) * (CONV_PAD + L), V7X_LANES), F32), ((1, POOL_PAD + L, d_pool), F32),
               ((CONV_WIDTH, V7X_SUBLANES, d_conv), F32), ((L, d_conv), F32)]
    w_list = [w_in, w_gate, b_gate, conv_w, conv_b, cln_g, cln_b, w_cp, w_pg, pscale, w_pp, w_out, l2g, l2b]
    vmem = (sum(_nbytes(w.shape, w.dtype) for w in w_list)
            + 2 * 2 * _nbytes(hblock, F32)
            + 2 * 2 * (_nbytes((1, CONV_HIST, d_conv), F32) + _nbytes((1, POOL_HIST, d_pool), F32))
            + sum(_nbytes(shape, dtype) for shape, dtype in scratch)
            + _nbytes((L, w_in.shape[1]), F32) + _nbytes((L, 2 * d), F32)
            + 8 * _nbytes((L, d), F32))
    kern = functools.partial(_mixer_prompt_kernel, alpha=alpha)
    return pl.pallas_call(
        kern,
        grid=(bsz, lseq // L),
        in_specs=([pl.BlockSpec((L, d), lambda b, t: (b * (lseq // L) + t, 0))]
                  + [_resident(w.shape) for w in w_list]),
        out_specs=[pl.BlockSpec(hblock, hmap), pl.BlockSpec((1, 1, CONV_HIST, d_conv), state_map),
                   pl.BlockSpec((1, 1, POOL_HIST, d_pool), state_map)],
        out_shape=[jax.ShapeDtypeStruct((bsz, lseq, d), F32), jax.ShapeDtypeStruct((1, bsz, CONV_HIST, d_conv), F32),
                   jax.ShapeDtypeStruct((1, bsz, POOL_HIST, d_pool), F32)],
        scratch_shapes=[pltpu.VMEM(shape, dtype) for shape, dtype in scratch],
        compiler_params=pltpu.CompilerParams(dimension_semantics=("arbitrary", "arbitrary"),
                                             vmem_limit_bytes=_vmem_limit(vmem)),
        name="mixer_prompt",
    )(h_all, *w_list)


def _mixer_state_kernel(h_ref, cst_ref, pst_ref, w_in_ref, w_gate_ref, b_gate_ref, cw_ref, cb_ref, clg_ref, clb_ref,
                        w_cp_ref, w_pg_ref, psc_ref, w_pp_ref, w_out_ref, l2g_ref, l2b_ref,
                        h2_ref, nc_ref, np_ref, cbuf, pext, wbc, dconv, *, pos0, alpha):
    S, L, d_model = h2_ref.shape
    N = S * L
    d_conv = wbc.shape[-1]
    d_pool = pext.shape[-1]

    h = h_ref[...]
    hb = h.astype(BF16)
    u = _dot(hb, w_in_ref[...])
    gates = jax.nn.sigmoid(_dot(hb, w_gate_ref[...]) + b_gate_ref[...])
    glu = u[:, :d_conv] * jax.nn.sigmoid(u[:, d_conv:2 * d_conv])

    _store_conv_input(cbuf, cst_ref[0], -CONV_HIST)
    pext[:, :POOL_PAD - POOL_HIST, :] = jnp.zeros((S, POOL_PAD - POOL_HIST, d_pool), F32)
    pext[:, POOL_PAD - POOL_HIST:POOL_PAD, :] = pst_ref[0]
    _broadcast_taps(wbc, cw_ref)
    glu3 = glu.reshape(S, L, d_conv)
    _store_conv_input(cbuf, glu3, 0)
    pext[:, POOL_PAD:, :] = u[:, 2 * d_conv:].reshape(S, L, d_pool)
    row = lax.broadcasted_iota(jnp.int32, (L, d_pool // len(POOL_WINDOWS)), 0)
    pooled = _pooled(pext, L, row + (pos0 + 1)).reshape(N, d_pool).astype(BF16)

    def conv_trip(s, carry):
        _conv_chunk(cbuf, wbc, cb_ref, dconv, s, 0, pl.multiple_of(s * L, L), L)
        return carry

    lax.fori_loop(0, S, conv_trip, 0)

    n_pieces = 2 * d_model // V7X_MXU_COLS
    gate_piece = lambda p: gates[:, p * V7X_MXU_COLS:(p + 1) * V7X_MXU_COLS]
    h2 = _merge(h, dconv[...], pooled, gate_piece, n_pieces, clg_ref, clb_ref, w_cp_ref, w_pg_ref, psc_ref, w_pp_ref,
                w_out_ref, l2g_ref, l2b_ref, alpha)
    h2_ref[...] = h2.reshape(S, L, d_model)
    nc_ref[0] = glu3[:, L - CONV_HIST:, :]
    np_ref[0] = pext[:, POOL_PAD + L - POOL_HIST:POOL_PAD + L, :]


def _mixer_state(h_all, first_row, bsz, L, conv_state, pool_state, weights, pos0, alpha):
    (w_in, w_gate, b_gate, conv_w, conv_b, cln_g, cln_b, w_cp, w_pg, pscale, w_pp, w_out, l2g, l2b) = weights
    d = h_all.shape[1]
    d_conv = conv_w.shape[1]
    d_pool = w_pp.shape[0]
    S = STATE_SEQS_PER_STEP
    N = S * L
    assert bsz % S == 0 and L % V7X_SUBLANES == 0 and L >= CONV_HIST and L >= POOL_HIST and first_row % N == 0

    hblock = (S, L, d)
    hmap = lambda i: (i, 0, 0)
    cstate_spec = pl.BlockSpec((1, S, CONV_HIST, d_conv), lambda i: (0, i, 0, 0))
    pstate_spec = pl.BlockSpec((1, S, POOL_HIST, d_pool), lambda i: (0, i, 0, 0))
    scratch = [((S, (d_conv // V7X_LANES) * (CONV_PAD + L), V7X_LANES), F32), ((S, POOL_PAD + L, d_pool), F32),
               ((CONV_WIDTH, V7X_SUBLANES, d_conv), F32), ((N, d_conv), F32)]
    w_list = [w_in, w_gate, b_gate, conv_w, conv_b, cln_g, cln_b, w_cp, w_pg, pscale, w_pp, w_out, l2g, l2b]
    vmem = (sum(_nbytes(w.shape, w.dtype) for w in w_list)
            + 2 * 2 * _nbytes(hblock, F32)
            + 2 * 2 * (_nbytes((S, CONV_HIST, d_conv), F32) + _nbytes((S, POOL_HIST, d_pool), F32))
            + sum(_nbytes(shape, dtype) for shape, dtype in scratch)
            + _nbytes((N, w_in.shape[1]), F32) + _nbytes((N, 2 * d), F32)
            + 8 * _nbytes((N, d), F32))
    kern = functools.partial(_mixer_state_kernel, pos0=pos0, alpha=alpha)
    return pl.pallas_call(
        kern,
        grid=(bsz // S,),
        in_specs=([pl.BlockSpec((N, d), lambda i: (first_row // N + i, 0)), cstate_spec, pstate_spec]
                  + [_resident(w.shape) for w in w_list]),
        out_specs=[pl.BlockSpec(hblock, hmap), cstate_spec, pstate_spec],
        out_shape=[jax.ShapeDtypeStruct((bsz, L, d), F32), jax.ShapeDtypeStruct(conv_state.shape, F32),
                   jax.ShapeDtypeStruct(pool_state.shape, F32)],
        scratch_shapes=[pltpu.VMEM(shape, dtype) for shape, dtype in scratch],
        compiler_params=pltpu.CompilerParams(dimension_semantics=("arbitrary",), vmem_limit_bytes=_vmem_limit(vmem)),
        name="mixer_state",
    )(h_all, conv_state, pool_state, *w_list)


def kernel(x_prompt, x_sample, state_conv, state_pool, w_ffn1_gate, w_ffn1_up, w_ffn1_down, ln1_g, ln1_b, w_in, w_gate, b_gate, conv_w, conv_b, conv_ln_g, conv_ln_b, w_conv_proj, w_pool_group, pool_scale, w_pool_proj, w_out, ln2_g, ln2_b, w_ffn2_gate, w_ffn2_up, w_ffn2_down, ln3_g, ln3_b):
    depth = w_in.shape[0]
    bsz, seq, d = x_prompt.shape
    dbsz, dseq, _ = x_sample.shape
    alpha = (2.0 * depth) ** 0.25
    row = lambda v: v.reshape(1, -1)

    hp = x_prompt.reshape(bsz * seq, d)
    hs = x_sample.reshape(dbsz * dseq, d)
    conv_p, conv_s, pool_p, pool_s = [], [], [], []
    for l in range(depth):
        h_all = _ffn_ln(hp, hs, w_ffn1_gate[l], w_ffn1_up[l], w_ffn1_down[l], row(ln1_g[l]), row(ln1_b[l]), alpha,
                        joined=True)
        mix_w = (w_in[l].astype(BF16), w_gate[l].astype(BF16), row(b_gate[l]), conv_w[l], row(conv_b[l]),
                 row(conv_ln_g[l]), row(conv_ln_b[l]), w_conv_proj[l].astype(BF16), w_pool_group[l].astype(BF16),
                 row(pool_scale[l]), w_pool_proj[l].astype(BF16), w_out[l].astype(BF16), row(ln2_g[l]), row(ln2_b[l]))
        hp3, cp, pp = _mixer_prompt(h_all, bsz, seq, mix_w, alpha)
        hs3, cs, ps = _mixer_state(h_all, bsz * seq, dbsz, dseq, state_conv[l:l + 1], state_pool[l:l + 1], mix_w,
                                   PAST_LEN, alpha)
        hp, hs = _ffn_ln(hp3.reshape(bsz * seq, d), hs3.reshape(dbsz * dseq, d), w_ffn2_gate[l], w_ffn2_up[l],
                         w_ffn2_down[l], row(ln3_g[l]), row(ln3_b[l]), alpha, joined=False)
        conv_p.append(cp)
        conv_s.append(cs)
        pool_p.append(pp)
        pool_s.append(ps)
    cat = lambda xs: xs[0] if len(xs) == 1 else jnp.concatenate(xs, axis=0)
    return (hp.reshape(bsz, seq, d), hs.reshape(dbsz, dseq, d), cat(conv_p), cat(conv_s), cat(pool_p), cat(pool_s))
```

```python
import functools

import jax
import jax.numpy as jnp
from jax import lax
from jax.experimental import pallas as pl
from jax.experimental.pallas import tpu as pltpu

CONV_WIDTH = 31
CONV_HIST = CONV_WIDTH - 1
POOL_WINDOWS = (2, 4, 8, 16)
POOL_HIST = max(POOL_WINDOWS) - 1
LN_EPS = 1e-5
PAST_LEN = 4096

V7X_SUBLANES = 8
V7X_LANES = 128
V7X_MXU_COLS = 256
V7X_VMEM_BYTES = 64 * 1024 * 1024

TOKEN_TILE = 512
FFN_BLOCKS = (256, 256)
MERGE_ROWS = 256
STATE_SEQS_PER_STEP = 8
CONV_ROWS = 128
CONV_PAD = 32
POOL_PAD = 16
WEIGHT_CHUNK = 128
STAGE_SLOTS = 4

BF16 = jnp.bfloat16
F32 = jnp.float32


def _layer_norm(z, g, b):
    mu = jnp.mean(z, axis=-1, keepdims=True)
    zc = z - mu
    var = jnp.mean(zc * zc, axis=-1, keepdims=True)
    return zc * lax.rsqrt(var + LN_EPS) * g + b


def _dot(a, b):
    return jnp.dot(a, b, preferred_element_type=F32)


def _resident(shape):
    zeros = (0,) * len(shape)
    return pl.BlockSpec(shape, lambda *_: zeros, pipeline_mode=pl.Buffered(1))


def _nbytes(shape, dtype):
    n = 1
    for s in shape:
        n *= s
    return n * jnp.dtype(dtype).itemsize


def _vmem_limit(estimate_bytes):
    assert estimate_bytes <= V7X_VMEM_BYTES, estimate_bytes
    return int(estimate_bytes)


def _stage_shape(w_shape):
    return (STAGE_SLOTS, WEIGHT_CHUNK, w_shape[1])


def _cast_weight(w_hbm, w_bf16, stage, sem):
    assert w_hbm.shape[0] % WEIGHT_CHUNK == 0 and stage.shape == _stage_shape(w_hbm.shape)
    n = w_hbm.shape[0] // WEIGHT_CHUNK
    piece = lambda i: pl.ds(i * WEIGHT_CHUNK, WEIGHT_CHUNK)

    def copy(i):
        return pltpu.make_async_copy(w_hbm.at[piece(i), :], stage.at[i % STAGE_SLOTS], sem.at[i % STAGE_SLOTS])

    ahead = STAGE_SLOTS - 1
    for i in range(min(ahead, n)):
        copy(i).start()
    for i in range(n):
        if i + ahead < n:
            copy(i + ahead).start()
        copy(i).wait()
        w_bf16[piece(i), :] = stage[i % STAGE_SLOTS].astype(BF16)


def _ffn_ln_kernel(xp_ref, xs_ref, wg_hbm, wu_hbm, wd_hbm, g_ref, b_ref, op_ref, os_ref,
                   wg_ref, wu_ref, wd_ref, stage_wide, stage_narrow, sem, *, n_prompt_tiles, alpha):
    i = pl.program_id(0)

    @pl.when(i == 0)
    def _():
        _cast_weight(wg_hbm, wg_ref, stage_wide, sem)
        _cast_weight(wu_hbm, wu_ref, stage_wide, sem)
        _cast_weight(wd_hbm, wd_ref, stage_narrow, sem)

    is_prompt = i < n_prompt_tiles
    assert sum(FFN_BLOCKS) == xp_ref.shape[0]
    ys = []
    r = 0
    for rows in FFN_BLOCKS:
        x = jnp.where(is_prompt, xp_ref[r:r + rows, :], xs_ref[r:r + rows, :])
        r += rows
        xb = x.astype(BF16)
        gate = _dot(xb, wg_ref[...])
        up = _dot(xb, wu_ref[...])
        act = (gate * jax.nn.sigmoid(gate) * up).astype(BF16)
        ffn = _dot(act, wd_ref[...])
        ys.append(_layer_norm(alpha * x + 0.5 * ffn, g_ref[...], b_ref[...]))
    y = jnp.concatenate(ys, axis=0)

    @pl.when(is_prompt)
    def _():
        op_ref[...] = y

    @pl.when(jnp.logical_not(is_prompt))
    def _():
        os_ref[...] = y


def _ffn_ln(xp, xs, wg, wu, wd, g, b, alpha):
    n_p, d = xp.shape
    n_s = xs.shape[0]
    d_ff = wg.shape[1]
    t = TOKEN_TILE
    assert n_p % t == 0 and n_s % t == 0
    npt, nst = n_p // t, n_s // t
    tile = (t, d)
    scratch = [(wg.shape, BF16), (wu.shape, BF16), (wd.shape, BF16), (_stage_shape(wg.shape), F32),
               (_stage_shape(wd.shape), F32)]
    vmem = (sum(_nbytes(shape, dtype) for shape, dtype in scratch)
            + 4 * 2 * _nbytes(tile, F32)
            + 2 * _nbytes((t, d_ff), F32) + _nbytes((t, d_ff), BF16)
            + 4 * _nbytes(tile, F32))
    kern = functools.partial(_ffn_ln_kernel, n_prompt_tiles=npt, alpha=alpha)
    return pl.pallas_call(
        kern,
        grid=(npt + nst,),
        in_specs=[
            pl.BlockSpec(tile, lambda i: (jnp.minimum(i, npt - 1), 0)),
            pl.BlockSpec(tile, lambda i: (jnp.maximum(i - npt, 0), 0)),
            pl.BlockSpec(memory_space=pl.ANY), pl.BlockSpec(memory_space=pl.ANY), pl.BlockSpec(memory_space=pl.ANY),
            _resident(g.shape), _resident(b.shape),
        ],
        out_specs=[
            pl.BlockSpec(tile, lambda i: (jnp.minimum(i, npt - 1), 0)),
            pl.BlockSpec(tile, lambda i: (jnp.maximum(i - npt, 0), 0)),
        ],
        out_shape=[jax.ShapeDtypeStruct(xp.shape, F32), jax.ShapeDtypeStruct(xs.shape, F32)],
        scratch_shapes=([pltpu.VMEM(shape, dtype) for shape, dtype in scratch]
                        + [pltpu.SemaphoreType.DMA((STAGE_SLOTS,))]),
        compiler_params=pltpu.CompilerParams(dimension_semantics=("arbitrary",), vmem_limit_bytes=_vmem_limit(vmem)),
        name="ffn_ln",
    )(xp, xs, wg, wu, wd, g, b)


def _store_conv_input(cbuf, glu3, first_frame):
    S, n, d_conv = glu3.shape
    tiles = d_conv // V7X_LANES
    for s in range(S):
        for c in range(tiles):
            rows = pl.ds(tiles * (CONV_PAD + first_frame) + c, n, stride=tiles)
            cbuf[s, rows, :] = glu3[s, :, c * V7X_LANES:(c + 1) * V7X_LANES]


def _broadcast_taps(wbc, cw_ref):
    for k in range(CONV_WIDTH):
        wbc[k] = jnp.broadcast_to(cw_ref[k:k + 1, :], wbc.shape[1:])


def _conv_chunk(cbuf, wbc, cb_ref, dconv, s, r0, out_row0, conv_rows):
    groups = conv_rows // V7X_SUBLANES
    tiles = wbc.shape[-1] // V7X_LANES
    for c in range(tiles):
        lanes = slice(c * V7X_LANES, (c + 1) * V7X_LANES)
        acc = [jnp.broadcast_to(cb_ref[:, lanes], (V7X_SUBLANES, V7X_LANES))] * groups
        for b in range(V7X_SUBLANES):
            a_max = (CONV_HIST - b) // V7X_SUBLANES
            rows = {g: cbuf[s, pl.ds(tiles * (r0 + CONV_PAD + V7X_SUBLANES * g - b) + c, V7X_SUBLANES, stride=tiles), :]
                    for g in range(-a_max, groups)}
            for a in range(a_max + 1):
                w = wbc[CONV_HIST - (V7X_SUBLANES * a + b), :, lanes]
                for m in range(groups):
                    acc[m] = acc[m] + rows[m - a] * w
        dconv[pl.ds(out_row0, conv_rows), lanes] = jnp.concatenate(acc, axis=0)


def _pooled(pext, L, pos1):
    pool_group = pext.shape[-1] // len(POOL_WINDOWS)
    out = []
    for gi, w in enumerate(POOL_WINDOWS):
        lanes = slice(gi * pool_group, (gi + 1) * pool_group)
        assert w & (w - 1) == 0 and w - 1 <= POOL_PAD
        total = pext[:, :, lanes]
        k = 1
        while k < w:
            total = total + jnp.concatenate([total[:, :k], total[:, :-k]], axis=1)
            k *= 2
        frame = pext[:, POOL_PAD:POOL_PAD + L, lanes]
        cnt = jnp.minimum(pos1, w).astype(F32)
        out.append(total[:, POOL_PAD:POOL_PAD + L] / cnt[None] - frame)
    return jnp.concatenate(out, axis=-1)


def _merge(h, dconv, pooled_bf16, gate_piece, n_pieces, clg_ref, clb_ref, w_cp_ref, w_pg_ref, psc_ref, w_pp_ref,
           w_out_ref, l2g_ref, l2b_ref, alpha):
    conv_act = _layer_norm(dconv, clg_ref[...], clb_ref[...])
    conv_act = conv_act * jax.nn.sigmoid(conv_act)
    branch_conv = _dot(conv_act.astype(BF16), w_cp_ref[...])
    pool_group = w_pg_ref.shape[-1]
    groups = []
    for gi in range(len(POOL_WINDOWS)):
        lanes = slice(gi * pool_group, (gi + 1) * pool_group)
        groups.append(_dot(pooled_bf16[:, lanes], w_pg_ref[gi]) * psc_ref[:, lanes])
    branch_pool = _dot(jnp.concatenate(groups, axis=-1).astype(BF16), w_pp_ref[...])
    half = n_pieces // 2
    gcols = branch_conv.shape[-1] // half
    merged = jnp.concatenate(
        [gate_piece(p) * branch_conv[:, p * gcols:(p + 1) * gcols]
         + gate_piece(half + p) * branch_pool[:, p * gcols:(p + 1) * gcols] for p in range(half)], axis=-1)
    merged = merged.astype(BF16)
    n = h.shape[0]
    rows = min(n, MERGE_ROWS)
    assert n % rows == 0
    out = []
    for r in range(0, n, rows):
        mixed = _dot(merged[r:r + rows], w_out_ref[...])
        out.append(_layer_norm(alpha * h[r:r + rows] + mixed, l2g_ref[...], l2b_ref[...]))
    return jnp.concatenate(out, axis=0)


def _mixer_prompt_kernel(h_ref, w_in_ref, w_gate_ref, b_gate_ref, cw_ref, cb_ref, clg_ref, clb_ref, w_cp_ref,
                         w_pg_ref, psc_ref, w_pp_ref, w_out_ref, l2g_ref, l2b_ref,
                         h2_ref, nc_ref, np_ref, cbuf, pext, wbc, dconv, *, alpha):
    t_idx = pl.program_id(1)
    L, d_model = h_ref.shape[1:]
    d_conv = wbc.shape[-1]
    d_pool = pext.shape[-1]
    hist_rows = cbuf.shape[1] - (d_conv // V7X_LANES) * L

    @pl.when(t_idx == 0)
    def _():
        cbuf[:, :hist_rows, :] = jnp.zeros((1, hist_rows, V7X_LANES), F32)
        pext[:, :POOL_PAD, :] = jnp.zeros((1, POOL_PAD, d_pool), F32)
        _broadcast_taps(wbc, cw_ref)

    @pl.when(t_idx > 0)
    def _():
        cbuf[:, :hist_rows, :] = cbuf[:, cbuf.shape[1] - hist_rows:, :]
        pext[:, :POOL_PAD, :] = pext[:, L:L + POOL_PAD, :]

    h = h_ref[0]
    hb = h.astype(BF16)
    u = _dot(hb, w_in_ref[...])
    glu = u[:, :d_conv] * jax.nn.sigmoid(u[:, d_conv:2 * d_conv])
    _store_conv_input(cbuf, glu.reshape(1, L, d_conv), 0)
    pext[:, POOL_PAD:, :] = u[:, 2 * d_conv:].reshape(1, L, d_pool)
    row = lax.broadcasted_iota(jnp.int32, (L, d_pool // len(POOL_WINDOWS)), 0)
    pooled = _pooled(pext, L, row + 1 + t_idx * L).reshape(L, d_pool).astype(BF16)

    def conv_trip(i, carry):
        r0 = pl.multiple_of(i * CONV_ROWS, CONV_ROWS)
        _conv_chunk(cbuf, wbc, cb_ref, dconv, 0, r0, r0, CONV_ROWS)
        return carry

    lax.fori_loop(0, L // CONV_ROWS, conv_trip, 0)

    gates = jax.nn.sigmoid(_dot(h_ref[0].astype(BF16), w_gate_ref[...]) + b_gate_ref[...])
    n_pieces = 2 * d_model // V7X_MXU_COLS
    gate_piece = lambda p: gates[:, p * V7X_MXU_COLS:(p + 1) * V7X_MXU_COLS]
    h2_ref[0] = _merge(h, dconv[...], pooled, gate_piece, n_pieces, clg_ref, clb_ref, w_cp_ref, w_pg_ref, psc_ref,
                       w_pp_ref, w_out_ref, l2g_ref, l2b_ref, alpha)

    @pl.when(t_idx == pl.num_programs(1) - 1)
    def _():
        nc_ref[0] = glu[L - CONV_HIST:, :].reshape(1, CONV_HIST, d_conv)
        np_ref[0] = pext[:, POOL_PAD + L - POOL_HIST:POOL_PAD + L, :]


def _mixer_prompt(h, weights, alpha):
    (w_in, w_gate, b_gate, conv_w, conv_b, cln_g, cln_b, w_cp, w_pg, pscale, w_pp, w_out, l2g, l2b) = weights
    bsz, lseq, d = h.shape
    d_conv = conv_w.shape[1]
    d_pool = w_pp.shape[0]
    L = TOKEN_TILE
    assert lseq % L == 0 and L % CONV_ROWS == 0 and L >= CONV_HIST and L >= POOL_HIST
    hblock = (1, L, d)
    hmap = lambda b, t: (b, t, 0)
    state_map = lambda b, t: (0, b, 0, 0)
    scratch = [((1, (d_conv // V7X_LANES) * (CONV_PAD + L), V7X_LANES), F32), ((1, POOL_PAD + L, d_pool), F32),
               ((CONV_WIDTH, V7X_SUBLANES, d_conv), F32), ((L, d_conv), F32)]
    w_list = [w_in, w_gate, b_gate, conv_w, conv_b, cln_g, cln_b, w_cp, w_pg, pscale, w_pp, w_out, l2g, l2b]
    vmem = (sum(_nbytes(w.shape, w.dtype) for w in w_list)
            + 2 * 2 * _nbytes(hblock, F32)
            + 2 * 2 * (_nbytes((1, CONV_HIST, d_conv), F32) + _nbytes((1, POOL_HIST, d_pool), F32))
            + sum(_nbytes(shape, dtype) for shape, dtype in scratch)
            + _nbytes((L, w_in.shape[1]), F32) + _nbytes((L, 2 * d), F32)
            + 8 * _nbytes((L, d), F32))
    kern = functools.partial(_mixer_prompt_kernel, alpha=alpha)
    return pl.pallas_call(
        kern,
        grid=(bsz, lseq // L),
        in_specs=[pl.BlockSpec(hblock, hmap)] + [_resident(w.shape) for w in w_list],
        out_specs=[pl.BlockSpec(hblock, hmap), pl.BlockSpec((1, 1, CONV_HIST, d_conv), state_map),
                   pl.BlockSpec((1, 1, POOL_HIST, d_pool), state_map)],
        out_shape=[jax.ShapeDtypeStruct(h.shape, F32), jax.ShapeDtypeStruct((1, bsz, CONV_HIST, d_conv), F32),
                   jax.ShapeDtypeStruct((1, bsz, POOL_HIST, d_pool), F32)],
        scratch_shapes=[pltpu.VMEM(shape, dtype) for shape, dtype in scratch],
        compiler_params=pltpu.CompilerParams(dimension_semantics=("arbitrary", "arbitrary"),
                                             vmem_limit_bytes=_vmem_limit(vmem)),
        name="mixer_prompt",
    )(h, *w_list)


def _mixer_state_kernel(h_ref, cst_ref, pst_ref, w_in_ref, w_gate_ref, b_gate_ref, cw_ref, cb_ref, clg_ref, clb_ref,
                        w_cp_ref, w_pg_ref, psc_ref, w_pp_ref, w_out_ref, l2g_ref, l2b_ref,
                        h2_ref, nc_ref, np_ref, cbuf, pext, wbc, dconv, *, pos0, alpha):
    S, L, d_model = h_ref.shape
    N = S * L
    d_conv = wbc.shape[-1]
    d_pool = pext.shape[-1]

    h = h_ref[...].reshape(N, d_model)
    hb = h.astype(BF16)
    u = _dot(hb, w_in_ref[...])
    gates = jax.nn.sigmoid(_dot(hb, w_gate_ref[...]) + b_gate_ref[...])
    glu = u[:, :d_conv] * jax.nn.sigmoid(u[:, d_conv:2 * d_conv])

    _store_conv_input(cbuf, cst_ref[0], -CONV_HIST)
    pext[:, :POOL_PAD - POOL_HIST, :] = jnp.zeros((S, POOL_PAD - POOL_HIST, d_pool), F32)
    pext[:, POOL_PAD - POOL_HIST:POOL_PAD, :] = pst_ref[0]
    _broadcast_taps(wbc, cw_ref)
    glu3 = glu.reshape(S, L, d_conv)
    _store_conv_input(cbuf, glu3, 0)
    pext[:, POOL_PAD:, :] = u[:, 2 * d_conv:].reshape(S, L, d_pool)
    row = lax.broadcasted_iota(jnp.int32, (L, d_pool // len(POOL_WINDOWS)), 0)
    pooled = _pooled(pext, L, row + (pos0 + 1)).reshape(N, d_pool).astype(BF16)

    def conv_trip(s, carry):
        _conv_chunk(cbuf, wbc, cb_ref, dconv, s, 0, pl.multiple_of(s * L, L), L)
        return carry

    lax.fori_loop(0, S, conv_trip, 0)

    n_pieces = 2 * d_model // V7X_MXU_COLS
    gate_piece = lambda p: gates[:, p * V7X_MXU_COLS:(p + 1) * V7X_MXU_COLS]
    h2 = _merge(h, dconv[...], pooled, gate_piece, n_pieces, clg_ref, clb_ref, w_cp_ref, w_pg_ref, psc_ref, w_pp_ref,
                w_out_ref, l2g_ref, l2b_ref, alpha)
    h2_ref[...] = h2.reshape(S, L, d_model)
    nc_ref[0] = glu3[:, L - CONV_HIST:, :]
    np_ref[0] = pext[:, POOL_PAD + L - POOL_HIST:POOL_PAD + L, :]


def _mixer_state(h, conv_state, pool_state, weights, pos0, alpha):
    (w_in, w_gate, b_gate, conv_w, conv_b, cln_g, cln_b, w_cp, w_pg, pscale, w_pp, w_out, l2g, l2b) = weights
    bsz, L, d = h.shape
    d_conv = conv_w.shape[1]
    d_pool = w_pp.shape[0]
    S = STATE_SEQS_PER_STEP
    N = S * L
    assert bsz % S == 0 and L % V7X_SUBLANES == 0 and L >= CONV_HIST and L >= POOL_HIST

    hblock = (S, L, d)
    hmap = lambda i: (i, 0, 0)
    cstate_spec = pl.BlockSpec((1, S, CONV_HIST, d_conv), lambda i: (0, i, 0, 0))
    pstate_spec = pl.BlockSpec((1, S, POOL_HIST, d_pool), lambda i: (0, i, 0, 0))
    scratch = [((S, (d_conv // V7X_LANES) * (CONV_PAD + L), V7X_LANES), F32), ((S, POOL_PAD + L, d_pool), F32),
               ((CONV_WIDTH, V7X_SUBLANES, d_conv), F32), ((N, d_conv), F32)]
    w_list = [w_in, w_gate, b_gate, conv_w, conv_b, cln_g, cln_b, w_cp, w_pg, pscale, w_pp, w_out, l2g, l2b]
    vmem = (sum(_nbytes(w.shape, w.dtype) for w in w_list)
            + 2 * 2 * _nbytes(hblock, F32)
            + 2 * 2 * (_nbytes((S, CONV_HIST, d_conv), F32) + _nbytes((S, POOL_HIST, d_pool), F32))
            + sum(_nbytes(shape, dtype) for shape, dtype in scratch)
            + _nbytes((N, w_in.shape[1]), F32) + _nbytes((N, 2 * d), F32)
            + 8 * _nbytes((N, d), F32))
    kern = functools.partial(_mixer_state_kernel, pos0=pos0, alpha=alpha)
    return pl.pallas_call(
        kern,
        grid=(bsz // S,),
        in_specs=[pl.BlockSpec(hblock, hmap), cstate_spec, pstate_spec] + [_resident(w.shape) for w in w_list],
        out_specs=[pl.BlockSpec(hblock, hmap), cstate_spec, pstate_spec],
        out_shape=[jax.ShapeDtypeStruct(h.shape, F32), jax.ShapeDtypeStruct(conv_state.shape, F32),
                   jax.ShapeDtypeStruct(pool_state.shape, F32)],
        scratch_shapes=[pltpu.VMEM(shape, dtype) for shape, dtype in scratch],
        compiler_params=pltpu.CompilerParams(dimension_semantics=("arbitrary",), vmem_limit_bytes=_vmem_limit(vmem)),
        name="mixer_state",
    )(h, conv_state, pool_state, *w_list)


def kernel(x_prompt, x_sample, state_conv, state_pool, w_ffn1_gate, w_ffn1_up, w_ffn1_down, ln1_g, ln1_b, w_in, w_gate, b_gate, conv_w, conv_b, conv_ln_g, conv_ln_b, w_conv_proj, w_pool_group, pool_scale, w_pool_proj, w_out, ln2_g, ln2_b, w_ffn2_gate, w_ffn2_up, w_ffn2_down, ln3_g, ln3_b):
    depth = w_in.shape[0]
    bsz, seq, d = x_prompt.shape
    dbsz, dseq, _ = x_sample.shape
    alpha = (2.0 * depth) ** 0.25
    row = lambda v: v.reshape(1, -1)

    hp = x_prompt.reshape(bsz * seq, d)
    hs = x_sample.reshape(dbsz * dseq, d)
    conv_p, conv_s, pool_p, pool_s = [], [], [], []
    for l in range(depth):
        hp, hs = _ffn_ln(hp, hs, w_ffn1_gate[l], w_ffn1_up[l], w_ffn1_down[l], row(ln1_g[l]), row(ln1_b[l]), alpha)
        mix_w = (w_in[l].astype(BF16), w_gate[l].astype(BF16), row(b_gate[l]), conv_w[l], row(conv_b[l]),
                 row(conv_ln_g[l]), row(conv_ln_b[l]), w_conv_proj[l].astype(BF16), w_pool_group[l].astype(BF16),
                 row(pool_scale[l]), w_pool_proj[l].astype(BF16), w_out[l].astype(BF16), row(ln2_g[l]), row(ln2_b[l]))
        hp3, cp, pp = _mixer_prompt(hp.reshape(bsz, seq, d), mix_w, alpha)
        hs3, cs, ps = _mixer_state(hs.reshape(dbsz, dseq, d), state_conv[l:l + 1], state_pool[l:l + 1], mix_w, PAST_LEN, alpha)
        hp, hs = _ffn_ln(hp3.reshape(bsz * seq, d), hs3.reshape(dbsz * dseq, d), w_ffn2_gate[l], w_ffn2_up[l],
                         w_ffn2_down[l], row(ln3_g[l]), row(ln3_b[l]), alpha)
        conv_p.append(cp)
        conv_s.append(cs)
        pool_p.append(pp)
        pool_s.append(ps)
    cat = lambda xs: xs[0] if len(xs) == 1 else jnp.concatenate(xs, axis=0)
    return (hp.reshape(bsz, seq, d), hs.reshape(dbsz, dseq, d), cat(conv_p), cat(conv_s), cat(pool_p), cat(pool_s))
```

```python
import functools

import jax
import jax.numpy as jnp
from jax import lax
from jax.experimental import pallas as pl
from jax.experimental.pallas import tpu as pltpu

CONV_WIDTH = 31
CONV_HIST = CONV_WIDTH - 1
POOL_WINDOWS = (2, 4, 8, 16)
POOL_HIST = max(POOL_WINDOWS) - 1
LN_EPS = 1e-5
PAST_LEN = 4096

V7X_SUBLANES = 8
V7X_LANES = 128
V7X_MXU_COLS = 256
V7X_VMEM_BYTES = 64 * 1024 * 1024

TOKEN_TILE = 512
FFN_BLOCKS = (256, 256)
MERGE_ROWS = 256
STATE_SEQS_PER_STEP = 8
CONV_ROWS = 128
CONV_PAD = 32
POOL_PAD = 16
WEIGHT_CHUNK = 128
STAGE_SLOTS = 4

BF16 = jnp.bfloat16
F32 = jnp.float32


def _layer_norm(z, g, b):
    mu = jnp.mean(z, axis=-1, keepdims=True)
    zc = z - mu
    var = jnp.mean(zc * zc, axis=-1, keepdims=True)
    return zc * lax.rsqrt(var + LN_EPS) * g + b


def _dot(a, b):
    return jnp.dot(a, b, preferred_element_type=F32)


def _resident(shape):
    zeros = (0,) * len(shape)
    return pl.BlockSpec(shape, lambda *_: zeros, pipeline_mode=pl.Buffered(1))


def _nbytes(shape, dtype):
    n = 1
    for s in shape:
        n *= s
    return n * jnp.dtype(dtype).itemsize


def _vmem_limit(estimate_bytes):
    assert estimate_bytes <= V7X_VMEM_BYTES, estimate_bytes
    return int(estimate_bytes)


def _stage_shape(w_shape):
    return (STAGE_SLOTS, WEIGHT_CHUNK, w_shape[1])


def _cast_weight(w_hbm, w_bf16, stage, sem):
    assert w_hbm.shape[0] % WEIGHT_CHUNK == 0 and stage.shape == _stage_shape(w_hbm.shape)
    n = w_hbm.shape[0] // WEIGHT_CHUNK
    piece = lambda i: pl.ds(i * WEIGHT_CHUNK, WEIGHT_CHUNK)

    def copy(i):
        return pltpu.make_async_copy(w_hbm.at[piece(i), :], stage.at[i % STAGE_SLOTS], sem.at[i % STAGE_SLOTS])

    ahead = STAGE_SLOTS - 1
    for i in range(min(ahead, n)):
        copy(i).start(priority=i % 2)
    for i in range(n):
        if i + ahead < n:
            copy(i + ahead).start(priority=(i + ahead) % 2)
        copy(i).wait()
        w_bf16[piece(i), :] = stage[i % STAGE_SLOTS].astype(BF16)


def _ffn_ln_kernel(xp_ref, xs_ref, wg_hbm, wu_hbm, wd_hbm, g_ref, b_ref, op_ref, os_ref,
                   wg_ref, wu_ref, wd_ref, stage_wide, stage_narrow, sem, *, n_prompt_tiles, alpha):
    i = pl.program_id(0)

    @pl.when(i == 0)
    def _():
        _cast_weight(wg_hbm, wg_ref, stage_wide, sem)
        _cast_weight(wu_hbm, wu_ref, stage_wide, sem)
        _cast_weight(wd_hbm, wd_ref, stage_narrow, sem)

    is_prompt = i < n_prompt_tiles
    assert sum(FFN_BLOCKS) == xp_ref.shape[0]
    ys = []
    r = 0
    for rows in FFN_BLOCKS:
        x = jnp.where(is_prompt, xp_ref[r:r + rows, :], xs_ref[r:r + rows, :])
        r += rows
        xb = x.astype(BF16)
        gate = _dot(xb, wg_ref[...])
        up = _dot(xb, wu_ref[...])
        act = (gate * jax.nn.sigmoid(gate) * up).astype(BF16)
        ffn = _dot(act, wd_ref[...])
        ys.append(_layer_norm(alpha * x + 0.5 * ffn, g_ref[...], b_ref[...]))
    y = jnp.concatenate(ys, axis=0)

    @pl.when(is_prompt)
    def _():
        op_ref[...] = y

    @pl.when(jnp.logical_not(is_prompt))
    def _():
        os_ref[...] = y


def _ffn_ln(xp, xs, wg, wu, wd, g, b, alpha):
    n_p, d = xp.shape
    n_s = xs.shape[0]
    d_ff = wg.shape[1]
    t = TOKEN_TILE
    assert n_p % t == 0 and n_s % t == 0
    npt, nst = n_p // t, n_s // t
    tile = (t, d)
    scratch = [(wg.shape, BF16), (wu.shape, BF16), (wd.shape, BF16), (_stage_shape(wg.shape), F32),
               (_stage_shape(wd.shape), F32)]
    vmem = (sum(_nbytes(shape, dtype) for shape, dtype in scratch)
            + 4 * 2 * _nbytes(tile, F32)
            + 2 * _nbytes((t, d_ff), F32) + _nbytes((t, d_ff), BF16)
            + 4 * _nbytes(tile, F32))
    kern = functools.partial(_ffn_ln_kernel, n_prompt_tiles=npt, alpha=alpha)
    return pl.pallas_call(
        kern,
        grid=(npt + nst,),
        in_specs=[
            pl.BlockSpec(tile, lambda i: (jnp.minimum(i, npt - 1), 0)),
            pl.BlockSpec(tile, lambda i: (jnp.maximum(i - npt, 0), 0)),
            pl.BlockSpec(memory_space=pl.ANY), pl.BlockSpec(memory_space=pl.ANY), pl.BlockSpec(memory_space=pl.ANY),
            _resident(g.shape), _resident(b.shape),
        ],
        out_specs=[
            pl.BlockSpec(tile, lambda i: (jnp.minimum(i, npt - 1), 0)),
            pl.BlockSpec(tile, lambda i: (jnp.maximum(i - npt, 0), 0)),
        ],
        out_shape=[jax.ShapeDtypeStruct(xp.shape, F32), jax.ShapeDtypeStruct(xs.shape, F32)],
        scratch_shapes=([pltpu.VMEM(shape, dtype) for shape, dtype in scratch]
                        + [pltpu.SemaphoreType.DMA((STAGE_SLOTS,))]),
        compiler_params=pltpu.CompilerParams(dimension_semantics=("arbitrary",), vmem_limit_bytes=_vmem_limit(vmem)),
        name="ffn_ln",
    )(xp, xs, wg, wu, wd, g, b)


def _store_conv_input(cbuf, glu3, first_frame):
    S, n, d_conv = glu3.shape
    tiles = d_conv // V7X_LANES
    for s in range(S):
        for c in range(tiles):
            rows = pl.ds(tiles * (CONV_PAD + first_frame) + c, n, stride=tiles)
            cbuf[s, rows, :] = glu3[s, :, c * V7X_LANES:(c + 1) * V7X_LANES]


def _broadcast_taps(wbc, cw_ref):
    for k in range(CONV_WIDTH):
        wbc[k] = jnp.broadcast_to(cw_ref[k:k + 1, :], wbc.shape[1:])


def _conv_chunk(cbuf, wbc, cb_ref, dconv, s, r0, out_row0, conv_rows):
    groups = conv_rows // V7X_SUBLANES
    tiles = wbc.shape[-1] // V7X_LANES
    for c in range(tiles):
        lanes = slice(c * V7X_LANES, (c + 1) * V7X_LANES)
        acc = [jnp.broadcast_to(cb_ref[:, lanes], (V7X_SUBLANES, V7X_LANES))] * groups
        for b in range(V7X_SUBLANES):
            a_max = (CONV_HIST - b) // V7X_SUBLANES
            rows = {g: cbuf[s, pl.ds(tiles * (r0 + CONV_PAD + V7X_SUBLANES * g - b) + c, V7X_SUBLANES, stride=tiles), :]
                    for g in range(-a_max, groups)}
            for a in range(a_max + 1):
                w = wbc[CONV_HIST - (V7X_SUBLANES * a + b), :, lanes]
                for m in range(groups):
                    acc[m] = acc[m] + rows[m - a] * w
        dconv[pl.ds(out_row0, conv_rows), lanes] = jnp.concatenate(acc, axis=0)


def _pooled(pext, L, pos1):
    pool_group = pext.shape[-1] // len(POOL_WINDOWS)
    out = []
    for gi, w in enumerate(POOL_WINDOWS):
        lanes = slice(gi * pool_group, (gi + 1) * pool_group)
        assert w & (w - 1) == 0 and w - 1 <= POOL_PAD
        total = pext[:, :, lanes]
        k = 1
        while k < w:
            total = total + jnp.concatenate([total[:, :k], total[:, :-k]], axis=1)
            k *= 2
        frame = pext[:, POOL_PAD:POOL_PAD + L, lanes]
        cnt = jnp.minimum(pos1, w).astype(F32)
        out.append(total[:, POOL_PAD:POOL_PAD + L] / cnt[None] - frame)
    return jnp.concatenate(out, axis=-1)


def _merge(h, dconv, pooled_bf16, gate_piece, n_pieces, clg_ref, clb_ref, w_cp_ref, w_pg_ref, psc_ref, w_pp_ref,
           w_out_ref, l2g_ref, l2b_ref, alpha):
    conv_act = _layer_norm(dconv, clg_ref[...], clb_ref[...])
    conv_act = conv_act * jax.nn.sigmoid(conv_act)
    branch_conv = _dot(conv_act.astype(BF16), w_cp_ref[...])
    pool_group = w_pg_ref.shape[-1]
    groups = []
    for gi in range(len(POOL_WINDOWS)):
        lanes = slice(gi * pool_group, (gi + 1) * pool_group)
        groups.append(_dot(pooled_bf16[:, lanes], w_pg_ref[gi]) * psc_ref[:, lanes])
    branch_pool = _dot(jnp.concatenate(groups, axis=-1).astype(BF16), w_pp_ref[...])
    half = n_pieces // 2
    gcols = branch_conv.shape[-1] // half
    merged = jnp.concatenate(
        [gate_piece(p) * branch_conv[:, p * gcols:(p + 1) * gcols]
         + gate_piece(half + p) * branch_pool[:, p * gcols:(p + 1) * gcols] for p in range(half)], axis=-1)
    merged = merged.astype(BF16)
    n = h.shape[0]
    rows = min(n, MERGE_ROWS)
    assert n % rows == 0
    out = []
    for r in range(0, n, rows):
        mixed = _dot(merged[r:r + rows], w_out_ref[...])
        out.append(_layer_norm(alpha * h[r:r + rows] + mixed, l2g_ref[...], l2b_ref[...]))
    return jnp.concatenate(out, axis=0)


def _mixer_prompt_kernel(h_ref, w_in_ref, w_gate_ref, b_gate_ref, cw_ref, cb_ref, clg_ref, clb_ref, w_cp_ref,
                         w_pg_ref, psc_ref, w_pp_ref, w_out_ref, l2g_ref, l2b_ref,
                         h2_ref, nc_ref, np_ref, cbuf, pext, wbc, dconv, *, alpha):
    t_idx = pl.program_id(1)
    L, d_model = h_ref.shape[1:]
    d_conv = wbc.shape[-1]
    d_pool = pext.shape[-1]
    hist_rows = cbuf.shape[1] - (d_conv // V7X_LANES) * L

    @pl.when(t_idx == 0)
    def _():
        cbuf[:, :hist_rows, :] = jnp.zeros((1, hist_rows, V7X_LANES), F32)
        pext[:, :POOL_PAD, :] = jnp.zeros((1, POOL_PAD, d_pool), F32)
        _broadcast_taps(wbc, cw_ref)

    @pl.when(t_idx > 0)
    def _():
        cbuf[:, :hist_rows, :] = cbuf[:, cbuf.shape[1] - hist_rows:, :]
        pext[:, :POOL_PAD, :] = pext[:, L:L + POOL_PAD, :]

    h = h_ref[0]
    hb = h.astype(BF16)
    u = _dot(hb, w_in_ref[...])
    glu = u[:, :d_conv] * jax.nn.sigmoid(u[:, d_conv:2 * d_conv])
    _store_conv_input(cbuf, glu.reshape(1, L, d_conv), 0)
    pext[:, POOL_PAD:, :] = u[:, 2 * d_conv:].reshape(1, L, d_pool)
    row = lax.broadcasted_iota(jnp.int32, (L, d_pool // len(POOL_WINDOWS)), 0)
    pooled = _pooled(pext, L, row + 1 + t_idx * L).reshape(L, d_pool).astype(BF16)

    def conv_trip(i, carry):
        r0 = pl.multiple_of(i * CONV_ROWS, CONV_ROWS)
        _conv_chunk(cbuf, wbc, cb_ref, dconv, 0, r0, r0, CONV_ROWS)
        return carry

    lax.fori_loop(0, L // CONV_ROWS, conv_trip, 0)

    gates = jax.nn.sigmoid(_dot(h_ref[0].astype(BF16), w_gate_ref[...]) + b_gate_ref[...])
    n_pieces = 2 * d_model // V7X_MXU_COLS
    gate_piece = lambda p: gates[:, p * V7X_MXU_COLS:(p + 1) * V7X_MXU_COLS]
    h2_ref[0] = _merge(h, dconv[...], pooled, gate_piece, n_pieces, clg_ref, clb_ref, w_cp_ref, w_pg_ref, psc_ref,
                       w_pp_ref, w_out_ref, l2g_ref, l2b_ref, alpha)

    @pl.when(t_idx == pl.num_programs(1) - 1)
    def _():
        nc_ref[0] = glu[L - CONV_HIST:, :].reshape(1, CONV_HIST, d_conv)
        np_ref[0] = pext[:, POOL_PAD + L - POOL_HIST:POOL_PAD + L, :]


def _mixer_prompt(h, weights, alpha):
    (w_in, w_gate, b_gate, conv_w, conv_b, cln_g, cln_b, w_cp, w_pg, pscale, w_pp, w_out, l2g, l2b) = weights
    bsz, lseq, d = h.shape
    d_conv = conv_w.shape[1]
    d_pool = w_pp.shape[0]
    L = TOKEN_TILE
    assert lseq % L == 0 and L % CONV_ROWS == 0 and L >= CONV_HIST and L >= POOL_HIST
    hblock = (1, L, d)
    hmap = lambda b, t: (b, t, 0)
    state_map = lambda b, t: (0, b, 0, 0)
    scratch = [((1, (d_conv // V7X_LANES) * (CONV_PAD + L), V7X_LANES), F32), ((1, POOL_PAD + L, d_pool), F32),
               ((CONV_WIDTH, V7X_SUBLANES, d_conv), F32), ((L, d_conv), F32)]
    w_list = [w_in, w_gate, b_gate, conv_w, conv_b, cln_g, cln_b, w_cp, w_pg, pscale, w_pp, w_out, l2g, l2b]
    vmem = (sum(_nbytes(w.shape, w.dtype) for w in w_list)
            + 2 * 2 * _nbytes(hblock, F32)
            + 2 * 2 * (_nbytes((1, CONV_HIST, d_conv), F32) + _nbytes((1, POOL_HIST, d_pool), F32))
            + sum(_nbytes(shape, dtype) for shape, dtype in scratch)
            + _nbytes((L, w_in.shape[1]), F32) + _nbytes((L, 2 * d), F32)
            + 8 * _nbytes((L, d), F32))
    kern = functools.partial(_mixer_prompt_kernel, alpha=alpha)
    return pl.pallas_call(
        kern,
        grid=(bsz, lseq // L),
        in_specs=[pl.BlockSpec(hblock, hmap)] + [_resident(w.shape) for w in w_list],
        out_specs=[pl.BlockSpec(hblock, hmap), pl.BlockSpec((1, 1, CONV_HIST, d_conv), state_map),
                   pl.BlockSpec((1, 1, POOL_HIST, d_pool), state_map)],
        out_shape=[jax.ShapeDtypeStruct(h.shape, F32), jax.ShapeDtypeStruct((1, bsz, CONV_HIST, d_conv), F32),
                   jax.ShapeDtypeStruct((1, bsz, POOL_HIST, d_pool), F32)],
        scratch_shapes=[pltpu.VMEM(shape, dtype) for shape, dtype in scratch],
        compiler_params=pltpu.CompilerParams(dimension_semantics=("arbitrary", "arbitrary"),
                                             vmem_limit_bytes=_vmem_limit(vmem)),
        name="mixer_prompt",
    )(h, *w_list)


def _mixer_state_kernel(h_ref, cst_ref, pst_ref, w_in_ref, w_gate_ref, b_gate_ref, cw_ref, cb_ref, clg_ref, clb_ref,
                        w_cp_ref, w_pg_ref, psc_ref, w_pp_ref, w_out_ref, l2g_ref, l2b_ref,
                        h2_ref, nc_ref, np_ref, cbuf, pext, wbc, dconv, *, pos0, alpha):
    S, L, d_model = h_ref.shape
    N = S * L
    d_conv = wbc.shape[-1]
    d_pool = pext.shape[-1]

    h = h_ref[...].reshape(N, d_model)
    hb = h.astype(BF16)
    u = _dot(hb, w_in_ref[...])
    glu = u[:, :d_conv] * jax.nn.sigmoid(u[:, d_conv:2 * d_conv])

    _store_conv_input(cbuf, cst_ref[0], -CONV_HIST)
    pext[:, :POOL_PAD - POOL_HIST, :] = jnp.zeros((S, POOL_PAD - POOL_HIST, d_pool), F32)
    pext[:, POOL_PAD - POOL_HIST:POOL_PAD, :] = pst_ref[0]
    _broadcast_taps(wbc, cw_ref)
    glu3 = glu.reshape(S, L, d_conv)
    _store_conv_input(cbuf, glu3, 0)
    pext[:, POOL_PAD:, :] = u[:, 2 * d_conv:].reshape(S, L, d_pool)
    row = lax.broadcasted_iota(jnp.int32, (L, d_pool // len(POOL_WINDOWS)), 0)
    pooled = _pooled(pext, L, row + (pos0 + 1)).reshape(N, d_pool).astype(BF16)

    def conv_trip(s, carry):
        _conv_chunk(cbuf, wbc, cb_ref, dconv, s, 0, pl.multiple_of(s * L, L), L)
        return carry

    lax.fori_loop(0, S, conv_trip, 0)

    gates = jax.nn.sigmoid(_dot(h_ref[...].reshape(N, d_model).astype(BF16), w_gate_ref[...]) + b_gate_ref[...])
    n_pieces = 2 * d_model // V7X_MXU_COLS
    gate_piece = lambda p: gates[:, p * V7X_MXU_COLS:(p + 1) * V7X_MXU_COLS]
    h2 = _merge(h, dconv[...], pooled, gate_piece, n_pieces, clg_ref, clb_ref, w_cp_ref, w_pg_ref, psc_ref, w_pp_ref,
                w_out_ref, l2g_ref, l2b_ref, alpha)
    h2_ref[...] = h2.reshape(S, L, d_model)
    nc_ref[0] = glu3[:, L - CONV_HIST:, :]
    np_ref[0] = pext[:, POOL_PAD + L - POOL_HIST:POOL_PAD + L, :]


def _mixer_state(h, conv_state, pool_state, weights, pos0, alpha):
    (w_in, w_gate, b_gate, conv_w, conv_b, cln_g, cln_b, w_cp, w_pg, pscale, w_pp, w_out, l2g, l2b) = weights
    bsz, L, d = h.shape
    d_conv = conv_w.shape[1]
    d_pool = w_pp.shape[0]
    S = STATE_SEQS_PER_STEP
    N = S * L
    assert bsz % S == 0 and L % V7X_SUBLANES == 0 and L >= CONV_HIST and L >= POOL_HIST

    hblock = (S, L, d)
    hmap = lambda i: (i, 0, 0)
    cstate_spec = pl.BlockSpec((1, S, CONV_HIST, d_conv), lambda i: (0, i, 0, 0))
    pstate_spec = pl.BlockSpec((1, S, POOL_HIST, d_pool), lambda i: (0, i, 0, 0))
    scratch = [((S, (d_conv // V7X_LANES) * (CONV_PAD + L), V7X_LANES), F32), ((S, POOL_PAD + L, d_pool), F32),
               ((CONV_WIDTH, V7X_SUBLANES, d_conv), F32), ((N, d_conv), F32)]
    w_list = [w_in, w_gate, b_gate, conv_w, conv_b, cln_g, cln_b, w_cp, w_pg, pscale, w_pp, w_out, l2g, l2b]
    vmem = (sum(_nbytes(w.shape, w.dtype) for w in w_list)
            + 2 * 2 * _nbytes(hblock, F32)
            + 2 * 2 * (_nbytes((S, CONV_HIST, d_conv), F32) + _nbytes((S, POOL_HIST, d_pool), F32))
            + sum(_nbytes(shape, dtype) for shape, dtype in scratch)
            + _nbytes((N, w_in.shape[1]), F32) + _nbytes((N, 2 * d), F32)
            + 8 * _nbytes((N, d), F32))
    kern = functools.partial(_mixer_state_kernel, pos0=pos0, alpha=alpha)
    return pl.pallas_call(
        kern,
        grid=(bsz // S,),
        in_specs=[pl.BlockSpec(hblock, hmap), cstate_spec, pstate_spec] + [_resident(w.shape) for w in w_list],
        out_specs=[pl.BlockSpec(hblock, hmap), cstate_spec, pstate_spec],
        out_shape=[jax.ShapeDtypeStruct(h.shape, F32), jax.ShapeDtypeStruct(conv_state.shape, F32),
                   jax.ShapeDtypeStruct(pool_state.shape, F32)],
        scratch_shapes=[pltpu.VMEM(shape, dtype) for shape, dtype in scratch],
        compiler_params=pltpu.CompilerParams(dimension_semantics=("arbitrary",), vmem_limit_bytes=_vmem_limit(vmem)),
        name="mixer_state",
    )(h, conv_state, pool_state, *w_list)


def kernel(x_prompt, x_sample, state_conv, state_pool, w_ffn1_gate, w_ffn1_up, w_ffn1_down, ln1_g, ln1_b, w_in, w_gate, b_gate, conv_w, conv_b, conv_ln_g, conv_ln_b, w_conv_proj, w_pool_group, pool_scale, w_pool_proj, w_out, ln2_g, ln2_b, w_ffn2_gate, w_ffn2_up, w_ffn2_down, ln3_g, ln3_b):
    depth = w_in.shape[0]
    bsz, seq, d = x_prompt.shape
    dbsz, dseq, _ = x_sample.shape
    alpha = (2.0 * depth) ** 0.25
    row = lambda v: v.reshape(1, -1)

    hp = x_prompt.reshape(bsz * seq, d)
    hs = x_sample.reshape(dbsz * dseq, d)
    conv_p, conv_s, pool_p, pool_s = [], [], [], []
    for l in range(depth):
        hp, hs = _ffn_ln(hp, hs, w_ffn1_gate[l], w_ffn1_up[l], w_ffn1_down[l], row(ln1_g[l]), row(ln1_b[l]), alpha)
        mix_w = (w_in[l].astype(BF16), w_gate[l].astype(BF16), row(b_gate[l]), conv_w[l], row(conv_b[l]),
                 row(conv_ln_g[l]), row(conv_ln_b[l]), w_conv_proj[l].astype(BF16), w_pool_group[l].astype(BF16),
                 row(pool_scale[l]), w_pool_proj[l].astype(BF16), w_out[l].astype(BF16), row(ln2_g[l]), row(ln2_b[l]))
        hp3, cp, pp = _mixer_prompt(hp.reshape(bsz, seq, d), mix_w, alpha)
        hs3, cs, ps = _mixer_state(hs.reshape(dbsz, dseq, d), state_conv[l:l + 1], state_pool[l:l + 1], mix_w, PAST_LEN, alpha)
        hp, hs = _ffn_ln(hp3.reshape(bsz * seq, d), hs3.reshape(dbsz * dseq, d), w_ffn2_gate[l], w_ffn2_up[l],
                         w_ffn2_down[l], row(ln3_g[l]), row(ln3_b[l]), alpha)
        conv_p.append(cp)
        conv_s.append(cs)
        pool_p.append(pp)
        pool_s.append(ps)
    cat = lambda xs: xs[0] if len(xs) == 1 else jnp.concatenate(xs, axis=0)
    return (hp.reshape(bsz, seq, d), hs.reshape(dbsz, dseq, d), cat(conv_p), cat(conv_s), cat(pool_p), cat(pool_s))
```

```python
import functools

import jax
import jax.numpy as jnp
from jax import lax
from jax.experimental import pallas as pl
from jax.experimental.pallas import tpu as pltpu

CONV_WIDTH = 31
CONV_HIST = CONV_WIDTH - 1
POOL_WINDOWS = (2, 4, 8, 16)
POOL_HIST = max(POOL_WINDOWS) - 1
LN_EPS = 1e-5
PAST_LEN = 4096

V7X_SUBLANES = 8
V7X_LANES = 128
V7X_MXU_COLS = 256
V7X_VMEM_BYTES = 64 * 1024 * 1024

TOKEN_TILE = 512
FFN_BLOCKS = (256, 256)
MERGE_ROWS = 256
STATE_SEQS_PER_STEP = 8
CONV_ROWS = 128
CONV_PAD = 32
POOL_PAD = 16
WEIGHT_CHUNK = 128
STAGE_SLOTS = 4

BF16 = jnp.bfloat16
F32 = jnp.float32


def _layer_norm(z, g, b):
    mu = jnp.mean(z, axis=-1, keepdims=True)
    zc = z - mu
    var = jnp.mean(zc * zc, axis=-1, keepdims=True)
    return zc * lax.rsqrt(var + LN_EPS) * g + b


def _dot(a, b):
    return jnp.dot(a, b, preferred_element_type=F32)


def _resident(shape):
    zeros = (0,) * len(shape)
    return pl.BlockSpec(shape, lambda *_: zeros, pipeline_mode=pl.Buffered(1))


def _nbytes(shape, dtype):
    n = 1
    for s in shape:
        n *= s
    return n * jnp.dtype(dtype).itemsize


def _vmem_limit(estimate_bytes):
    assert estimate_bytes <= V7X_VMEM_BYTES, estimate_bytes
    return int(estimate_bytes)


def _stage_shape(w_shape):
    return (STAGE_SLOTS, WEIGHT_CHUNK, w_shape[1])


def _cast_weight(w_hbm, w_bf16, stage, sem):
    assert w_hbm.shape[0] % WEIGHT_CHUNK == 0 and stage.shape == _stage_shape(w_hbm.shape)
    n = w_hbm.shape[0] // WEIGHT_CHUNK
    piece = lambda i: pl.ds(i * WEIGHT_CHUNK, WEIGHT_CHUNK)

    def copy(i):
        return pltpu.make_async_copy(w_hbm.at[piece(i), :], stage.at[i % STAGE_SLOTS], sem.at[i % STAGE_SLOTS])

    ahead = STAGE_SLOTS - 1
    for i in range(min(ahead, n)):
        copy(i).start(priority=i % 2)
    for i in range(n):
        if i + ahead < n:
            copy(i + ahead).start(priority=(i + ahead) % 2)
        copy(i).wait()
        w_bf16[piece(i), :] = stage[i % STAGE_SLOTS].astype(BF16)


def _ffn_ln_kernel(xp_ref, xs_ref, wg_hbm, wu_hbm, wd_hbm, g_ref, b_ref, op_ref, os_ref,
                   wg_ref, wu_ref, wd_ref, stage_wide, stage_narrow, sem, *, n_prompt_tiles, alpha):
    i = pl.program_id(0)

    @pl.when(i == 0)
    def _():
        _cast_weight(wg_hbm, wg_ref, stage_wide, sem)
        _cast_weight(wu_hbm, wu_ref, stage_wide, sem)
        _cast_weight(wd_hbm, wd_ref, stage_narrow, sem)

    is_prompt = i < n_prompt_tiles
    assert sum(FFN_BLOCKS) == xp_ref.shape[0]
    xs_, gates, ups = [], [], []
    r = 0
    for rows in FFN_BLOCKS:
        x = jnp.where(is_prompt, xp_ref[r:r + rows, :], xs_ref[r:r + rows, :])
        r += rows
        xb = x.astype(BF16)
        xs_.append(x)
        gates.append(_dot(xb, wg_ref[...]))
        ups.append(_dot(xb, wu_ref[...]))
    ys = []
    for x, gate, up in zip(xs_, gates, ups):
        act = (gate * jax.nn.sigmoid(gate) * up).astype(BF16)
        ffn = _dot(act, wd_ref[...])
        ys.append(_layer_norm(alpha * x + 0.5 * ffn, g_ref[...], b_ref[...]))
    y = jnp.concatenate(ys, axis=0)

    @pl.when(is_prompt)
    def _():
        op_ref[...] = y

    @pl.when(jnp.logical_not(is_prompt))
    def _():
        os_ref[...] = y


def _ffn_ln(xp, xs, wg, wu, wd, g, b, alpha):
    n_p, d = xp.shape
    n_s = xs.shape[0]
    d_ff = wg.shape[1]
    t = TOKEN_TILE
    assert n_p % t == 0 and n_s % t == 0
    npt, nst = n_p // t, n_s // t
    tile = (t, d)
    scratch = [(wg.shape, BF16), (wu.shape, BF16), (wd.shape, BF16), (_stage_shape(wg.shape), F32),
               (_stage_shape(wd.shape), F32)]
    vmem = (sum(_nbytes(shape, dtype) for shape, dtype in scratch)
            + 4 * 2 * _nbytes(tile, F32)
            + 2 * _nbytes((t, d_ff), F32) + _nbytes((t, d_ff), BF16)
            + 4 * _nbytes(tile, F32))
    kern = functools.partial(_ffn_ln_kernel, n_prompt_tiles=npt, alpha=alpha)
    return pl.pallas_call(
        kern,
        grid=(npt + nst,),
        in_specs=[
            pl.BlockSpec(tile, lambda i: (jnp.minimum(i, npt - 1), 0)),
            pl.BlockSpec(tile, lambda i: (jnp.maximum(i - npt, 0), 0)),
            pl.BlockSpec(memory_space=pl.ANY), pl.BlockSpec(memory_space=pl.ANY), pl.BlockSpec(memory_space=pl.ANY),
            _resident(g.shape), _resident(b.shape),
        ],
        out_specs=[
            pl.BlockSpec(tile, lambda i: (jnp.minimum(i, npt - 1), 0)),
            pl.BlockSpec(tile, lambda i: (jnp.maximum(i - npt, 0), 0)),
        ],
        out_shape=[jax.ShapeDtypeStruct(xp.shape, F32), jax.ShapeDtypeStruct(xs.shape, F32)],
        scratch_shapes=([pltpu.VMEM(shape, dtype) for shape, dtype in scratch]
                        + [pltpu.SemaphoreType.DMA((STAGE_SLOTS,))]),
        compiler_params=pltpu.CompilerParams(dimension_semantics=("arbitrary",), vmem_limit_bytes=_vmem_limit(vmem)),
        name="ffn_ln",
    )(xp, xs, wg, wu, wd, g, b)


def _store_conv_input(cbuf, glu3, first_frame):
    S, n, d_conv = glu3.shape
    tiles = d_conv // V7X_LANES
    for s in range(S):
        for c in range(tiles):
            rows = pl.ds(tiles * (CONV_PAD + first_frame) + c, n, stride=tiles)
            cbuf[s, rows, :] = glu3[s, :, c * V7X_LANES:(c + 1) * V7X_LANES]


def _broadcast_taps(wbc, cw_ref):
    for k in range(CONV_WIDTH):
        wbc[k] = jnp.broadcast_to(cw_ref[k:k + 1, :], wbc.shape[1:])


def _conv_chunk(cbuf, wbc, cb_ref, dconv, s, r0, out_row0, conv_rows):
    groups = conv_rows // V7X_SUBLANES
    tiles = wbc.shape[-1] // V7X_LANES
    for c in range(tiles):
        lanes = slice(c * V7X_LANES, (c + 1) * V7X_LANES)
        acc = [jnp.broadcast_to(cb_ref[:, lanes], (V7X_SUBLANES, V7X_LANES))] * groups
        for b in range(V7X_SUBLANES):
            a_max = (CONV_HIST - b) // V7X_SUBLANES
            rows = {g: cbuf[s, pl.ds(tiles * (r0 + CONV_PAD + V7X_SUBLANES * g - b) + c, V7X_SUBLANES, stride=tiles), :]
                    for g in range(-a_max, groups)}
            for a in range(a_max + 1):
                w = wbc[CONV_HIST - (V7X_SUBLANES * a + b), :, lanes]
                for m in range(groups):
                    acc[m] = acc[m] + rows[m - a] * w
        dconv[pl.ds(out_row0, conv_rows), lanes] = jnp.concatenate(acc, axis=0)


def _pooled(pext, L, pos1):
    pool_group = pext.shape[-1] // len(POOL_WINDOWS)
    out = []
    for gi, w in enumerate(POOL_WINDOWS):
        lanes = slice(gi * pool_group, (gi + 1) * pool_group)
        assert w & (w - 1) == 0 and w - 1 <= POOL_PAD
        total = pext[:, :, lanes]
        k = 1
        while k < w:
            total = total + jnp.concatenate([total[:, :k], total[:, :-k]], axis=1)
            k *= 2
        frame = pext[:, POOL_PAD:POOL_PAD + L, lanes]
        cnt = jnp.minimum(pos1, w).astype(F32)
        out.append(total[:, POOL_PAD:POOL_PAD + L] / cnt[None] - frame)
    return jnp.concatenate(out, axis=-1)


def _merge(h, dconv, pooled_bf16, gate_piece, n_pieces, clg_ref, clb_ref, w_cp_ref, w_pg_ref, psc_ref, w_pp_ref,
           w_out_ref, l2g_ref, l2b_ref, alpha):
    conv_act = _layer_norm(dconv, clg_ref[...], clb_ref[...])
    conv_act = conv_act * jax.nn.sigmoid(conv_act)
    branch_conv = _dot(conv_act.astype(BF16), w_cp_ref[...])
    pool_group = w_pg_ref.shape[-1]
    groups = []
    for gi in range(len(POOL_WINDOWS)):
        lanes = slice(gi * pool_group, (gi + 1) * pool_group)
        groups.append(_dot(pooled_bf16[:, lanes], w_pg_ref[gi]) * psc_ref[:, lanes])
    branch_pool = _dot(jnp.concatenate(groups, axis=-1).astype(BF16), w_pp_ref[...])
    half = n_pieces // 2
    gcols = branch_conv.shape[-1] // half
    merged = jnp.concatenate(
        [gate_piece(p) * branch_conv[:, p * gcols:(p + 1) * gcols]
         + gate_piece(half + p) * branch_pool[:, p * gcols:(p + 1) * gcols] for p in range(half)], axis=-1)
    merged = merged.astype(BF16)
    n = h.shape[0]
    rows = min(n, MERGE_ROWS)
    assert n % rows == 0
    out = []
    for r in range(0, n, rows):
        mixed = _dot(merged[r:r + rows], w_out_ref[...])
        out.append(_layer_norm(alpha * h[r:r + rows] + mixed, l2g_ref[...], l2b_ref[...]))
    return jnp.concatenate(out, axis=0)


def _mixer_prompt_kernel(h_ref, w_in_ref, w_gate_ref, b_gate_ref, cw_ref, cb_ref, clg_ref, clb_ref, w_cp_ref,
                         w_pg_ref, psc_ref, w_pp_ref, w_out_ref, l2g_ref, l2b_ref,
                         h2_ref, nc_ref, np_ref, cbuf, pext, wbc, dconv, *, alpha):
    t_idx = pl.program_id(1)
    L, d_model = h_ref.shape[1:]
    d_conv = wbc.shape[-1]
    d_pool = pext.shape[-1]
    hist_rows = cbuf.shape[1] - (d_conv // V7X_LANES) * L

    @pl.when(t_idx == 0)
    def _():
        cbuf[:, :hist_rows, :] = jnp.zeros((1, hist_rows, V7X_LANES), F32)
        pext[:, :POOL_PAD, :] = jnp.zeros((1, POOL_PAD, d_pool), F32)
        _broadcast_taps(wbc, cw_ref)

    @pl.when(t_idx > 0)
    def _():
        cbuf[:, :hist_rows, :] = cbuf[:, cbuf.shape[1] - hist_rows:, :]
        pext[:, :POOL_PAD, :] = pext[:, L:L + POOL_PAD, :]

    h = h_ref[0]
    hb = h.astype(BF16)
    u = _dot(hb, w_in_ref[...])
    glu = u[:, :d_conv] * jax.nn.sigmoid(u[:, d_conv:2 * d_conv])
    _store_conv_input(cbuf, glu.reshape(1, L, d_conv), 0)
    pext[:, POOL_PAD:, :] = u[:, 2 * d_conv:].reshape(1, L, d_pool)
    row = lax.broadcasted_iota(jnp.int32, (L, d_pool // len(POOL_WINDOWS)), 0)
    pooled = _pooled(pext, L, row + 1 + t_idx * L).reshape(L, d_pool).astype(BF16)

    def conv_trip(i, carry):
        r0 = pl.multiple_of(i * CONV_ROWS, CONV_ROWS)
        _conv_chunk(cbuf, wbc, cb_ref, dconv, 0, r0, r0, CONV_ROWS)
        return carry

    lax.fori_loop(0, L // CONV_ROWS, conv_trip, 0)

    gates = jax.nn.sigmoid(_dot(h_ref[0].astype(BF16), w_gate_ref[...]) + b_gate_ref[...])
    n_pieces = 2 * d_model // V7X_MXU_COLS
    gate_piece = lambda p: gates[:, p * V7X_MXU_COLS:(p + 1) * V7X_MXU_COLS]
    h2_ref[0] = _merge(h, dconv[...], pooled, gate_piece, n_pieces, clg_ref, clb_ref, w_cp_ref, w_pg_ref, psc_ref,
                       w_pp_ref, w_out_ref, l2g_ref, l2b_ref, alpha)

    @pl.when(t_idx == pl.num_programs(1) - 1)
    def _():
        nc_ref[0] = glu[L - CONV_HIST:, :].reshape(1, CONV_HIST, d_conv)
        np_ref[0] = pext[:, POOL_PAD + L - POOL_HIST:POOL_PAD + L, :]


def _mixer_prompt(h, weights, alpha):
    (w_in, w_gate, b_gate, conv_w, conv_b, cln_g, cln_b, w_cp, w_pg, pscale, w_pp, w_out, l2g, l2b) = weights
    bsz, lseq, d = h.shape
    d_conv = conv_w.shape[1]
    d_pool = w_pp.shape[0]
    L = TOKEN_TILE
    assert lseq % L == 0 and L % CONV_ROWS == 0 and L >= CONV_HIST and L >= POOL_HIST
    hblock = (1, L, d)
    hmap = lambda b, t: (b, t, 0)
    state_map = lambda b, t: (0, b, 0, 0)
    scratch = [((1, (d_conv // V7X_LANES) * (CONV_PAD + L), V7X_LANES), F32), ((1, POOL_PAD + L, d_pool), F32),
               ((CONV_WIDTH, V7X_SUBLANES, d_conv), F32), ((L, d_conv), F32)]
    w_list = [w_in, w_gate, b_gate, conv_w, conv_b, cln_g, cln_b, w_cp, w_pg, pscale, w_pp, w_out, l2g, l2b]
    vmem = (sum(_nbytes(w.shape, w.dtype) for w in w_list)
            + 2 * 2 * _nbytes(hblock, F32)
            + 2 * 2 * (_nbytes((1, CONV_HIST, d_conv), F32) + _nbytes((1, POOL_HIST, d_pool), F32))
            + sum(_nbytes(shape, dtype) for shape, dtype in scratch)
            + _nbytes((L, w_in.shape[1]), F32) + _nbytes((L, 2 * d), F32)
            + 8 * _nbytes((L, d), F32))
    kern = functools.partial(_mixer_prompt_kernel, alpha=alpha)
    return pl.pallas_call(
        kern,
        grid=(bsz, lseq // L),
        in_specs=[pl.BlockSpec(hblock, hmap)] + [_resident(w.shape) for w in w_list],
        out_specs=[pl.BlockSpec(hblock, hmap), pl.BlockSpec((1, 1, CONV_HIST, d_conv), state_map),
                   pl.BlockSpec((1, 1, POOL_HIST, d_pool), state_map)],
        out_shape=[jax.ShapeDtypeStruct(h.shape, F32), jax.ShapeDtypeStruct((1, bsz, CONV_HIST, d_conv), F32),
                   jax.ShapeDtypeStruct((1, bsz, POOL_HIST, d_pool), F32)],
        scratch_shapes=[pltpu.VMEM(shape, dtype) for shape, dtype in scratch],
        compiler_params=pltpu.CompilerParams(dimension_semantics=("arbitrary", "arbitrary"),
                                             vmem_limit_bytes=_vmem_limit(vmem)),
        name="mixer_prompt",
    )(h, *w_list)


def _mixer_state_kernel(h_ref, cst_ref, pst_ref, w_in_ref, w_gate_ref, b_gate_ref, cw_ref, cb_ref, clg_ref, clb_ref,
                        w_cp_ref, w_pg_ref, psc_ref, w_pp_ref, w_out_ref, l2g_ref, l2b_ref,
                        h2_ref, nc_ref, np_ref, cbuf, pext, wbc, dconv, *, pos0, alpha):
    S, L, d_model = h_ref.shape
    N = S * L
    d_conv = wbc.shape[-1]
    d_pool = pext.shape[-1]

    h = h_ref[...].reshape(N, d_model)
    hb = h.astype(BF16)
    u = _dot(hb, w_in_ref[...])
    glu = u[:, :d_conv] * jax.nn.sigmoid(u[:, d_conv:2 * d_conv])

    _store_conv_input(cbuf, cst_ref[0], -CONV_HIST)
    pext[:, :POOL_PAD - POOL_HIST, :] = jnp.zeros((S, POOL_PAD - POOL_HIST, d_pool), F32)
    pext[:, POOL_PAD - POOL_HIST:POOL_PAD, :] = pst_ref[0]
    _broadcast_taps(wbc, cw_ref)
    glu3 = glu.reshape(S, L, d_conv)
    _store_conv_input(cbuf, glu3, 0)
    pext[:, POOL_PAD:, :] = u[:, 2 * d_conv:].reshape(S, L, d_pool)
    row = lax.broadcasted_iota(jnp.int32, (L, d_pool // len(POOL_WINDOWS)), 0)
    pooled = _pooled(pext, L, row + (pos0 + 1)).reshape(N, d_pool).astype(BF16)

    def conv_trip(s, carry):
        _conv_chunk(cbuf, wbc, cb_ref, dconv, s, 0, pl.multiple_of(s * L, L), L)
        return carry

    lax.fori_loop(0, S, conv_trip, 0)

    gates = jax.nn.sigmoid(_dot(h_ref[...].reshape(N, d_model).astype(BF16), w_gate_ref[...]) + b_gate_ref[...])
    n_pieces = 2 * d_model // V7X_MXU_COLS
    gate_piece = lambda p: gates[:, p * V7X_MXU_COLS:(p + 1) * V7X_MXU_COLS]
    h2 = _merge(h, dconv[...], pooled, gate_piece, n_pieces, clg_ref, clb_ref, w_cp_ref, w_pg_ref, psc_ref, w_pp_ref,
                w_out_ref, l2g_ref, l2b_ref, alpha)
    h2_ref[...] = h2.reshape(S, L, d_model)
    nc_ref[0] = glu3[:, L - CONV_HIST:, :]
    np_ref[0] = pext[:, POOL_PAD + L - POOL_HIST:POOL_PAD + L, :]


def _mixer_state(h, conv_state, pool_state, weights, pos0, alpha):
    (w_in, w_gate, b_gate, conv_w, conv_b, cln_g, cln_b, w_cp, w_pg, pscale, w_pp, w_out, l2g, l2b) = weights
    bsz, L, d = h.shape
    d_conv = conv_w.shape[1]
    d_pool = w_pp.shape[0]
    S = STATE_SEQS_PER_STEP
    N = S * L
    assert bsz % S == 0 and L % V7X_SUBLANES == 0 and L >= CONV_HIST and L >= POOL_HIST

    hblock = (S, L, d)
    hmap = lambda i: (i, 0, 0)
    cstate_spec = pl.BlockSpec((1, S, CONV_HIST, d_conv), lambda i: (0, i, 0, 0))
    pstate_spec = pl.BlockSpec((1, S, POOL_HIST, d_pool), lambda i: (0, i, 0, 0))
    scratch = [((S, (d_conv // V7X_LANES) * (CONV_PAD + L), V7X_LANES), F32), ((S, POOL_PAD + L, d_pool), F32),
               ((CONV_WIDTH, V7X_SUBLANES, d_conv), F32), ((N, d_conv), F32)]
    w_list = [w_in, w_gate, b_gate, conv_w, conv_b, cln_g, cln_b, w_cp, w_pg, pscale, w_pp, w_out, l2g, l2b]
    vmem = (sum(_nbytes(w.shape, w.dtype) for w in w_list)
            + 2 * 2 * _nbytes(hblock, F32)
            + 2 * 2 * (_nbytes((S, CONV_HIST, d_conv), F32) + _nbytes((S, POOL_HIST, d_pool), F32))
            + sum(_nbytes(shape, dtype) for shape, dtype in scratch)
            + _nbytes((N, w_in.shape[1]), F32) + _nbytes((N, 2 * d), F32)
            + 8 * _nbytes((N, d), F32))
    kern = functools.partial(_mixer_state_kernel, pos0=pos0, alpha=alpha)
    return pl.pallas_call(
        kern,
        grid=(bsz // S,),
        in_specs=[pl.BlockSpec(hblock, hmap), cstate_spec, pstate_spec] + [_resident(w.shape) for w in w_list],
        out_specs=[pl.BlockSpec(hblock, hmap), cstate_spec, pstate_spec],
        out_shape=[jax.ShapeDtypeStruct(h.shape, F32), jax.ShapeDtypeStruct(conv_state.shape, F32),
                   jax.ShapeDtypeStruct(pool_state.shape, F32)],
        scratch_shapes=[pltpu.VMEM(shape, dtype) for shape, dtype in scratch],
        compiler_params=pltpu.CompilerParams(dimension_semantics=("arbitrary",), vmem_limit_bytes=_vmem_limit(vmem)),
        name="mixer_state",
    )(h, conv_state, pool_state, *w_list)


def kernel(x_prompt, x_sample, state_conv, state_pool, w_ffn1_gate, w_ffn1_up, w_ffn1_down, ln1_g, ln1_b, w_in, w_gate, b_gate, conv_w, conv_b, conv_ln_g, conv_ln_b, w_conv_proj, w_pool_group, pool_scale, w_pool_proj, w_out, ln2_g, ln2_b, w_ffn2_gate, w_ffn2_up, w_ffn2_down, ln3_g, ln3_b):
    depth = w_in.shape[0]
    bsz, seq, d = x_prompt.shape
    dbsz, dseq, _ = x_sample.shape
    alpha = (2.0 * depth) ** 0.25
    row = lambda v: v.reshape(1, -1)

    hp = x_prompt.reshape(bsz * seq, d)
    hs = x_sample.reshape(dbsz * dseq, d)
    conv_p, conv_s, pool_p, pool_s = [], [], [], []
    for l in range(depth):
        hp, hs = _ffn_ln(hp, hs, w_ffn1_gate[l], w_ffn1_up[l], w_ffn1_down[l], row(ln1_g[l]), row(ln1_b[l]), alpha)
        mix_w = (w_in[l].astype(BF16), w_gate[l].astype(BF16), row(b_gate[l]), conv_w[l], row(conv_b[l]),
                 row(conv_ln_g[l]), row(conv_ln_b[l]), w_conv_proj[l].astype(BF16), w_pool_group[l].astype(BF16),
                 row(pool_scale[l]), w_pool_proj[l].astype(BF16), w_out[l].astype(BF16), row(ln2_g[l]), row(ln2_b[l]))
        hp3, cp, pp = _mixer_prompt(hp.reshape(bsz, seq, d), mix_w, alpha)
        hs3, cs, ps = _mixer_state(hs.reshape(dbsz, dseq, d), state_conv[l:l + 1], state_pool[l:l + 1], mix_w, PAST_LEN, alpha)
        hp, hs = _ffn_ln(hp3.reshape(bsz * seq, d), hs3.reshape(dbsz * dseq, d), w_ffn2_gate[l], w_ffn2_up[l],
                         w_ffn2_down[l], row(ln3_g[l]), row(ln3_b[l]), alpha)
        conv_p.append(cp)
        conv_s.append(cs)
        pool_p.append(pp)
        pool_s.append(ps)
    cat = lambda xs: xs[0] if len(xs) == 1 else jnp.concatenate(xs, axis=0)
    return (hp.reshape(bsz, seq, d), hs.reshape(dbsz, dseq, d), cat(conv_p), cat(conv_s), cat(pool_p), cat(pool_s))
```

```python
import functools

import jax
import jax.numpy as jnp
from jax import lax
from jax.experimental import pallas as pl
from jax.experimental.pallas import tpu as pltpu

CONV_WIDTH = 31
CONV_HIST = CONV_WIDTH - 1
POOL_WINDOWS = (2, 4, 8, 16)
POOL_HIST = max(POOL_WINDOWS) - 1
LN_EPS = 1e-5
PAST_LEN = 4096

V7X_SUBLANES = 8
V7X_LANES = 128
V7X_MXU_COLS = 256
V7X_VMEM_BYTES = 64 * 1024 * 1024

TOKEN_TILE = 512
FFN_BLOCKS = (256, 256)
MERGE_ROWS = 256
STATE_SEQS_PER_STEP = 8
CONV_ROWS = 128
CONV_PAD = 32
POOL_PAD = 16
WEIGHT_CHUNK = 128
STAGE_SLOTS = 4

BF16 = jnp.bfloat16
F32 = jnp.float32


def _layer_norm(z, g, b):
    mu = jnp.mean(z, axis=-1, keepdims=True)
    zc = z - mu
    var = jnp.mean(zc * zc, axis=-1, keepdims=True)
    return zc * lax.rsqrt(var + LN_EPS) * g + b


def _dot(a, b):
    return jnp.dot(a, b, preferred_element_type=F32)


def _resident(shape):
    zeros = (0,) * len(shape)
    return pl.BlockSpec(shape, lambda *_: zeros, pipeline_mode=pl.Buffered(1))


def _nbytes(shape, dtype):
    n = 1
    for s in shape:
        n *= s
    return n * jnp.dtype(dtype).itemsize


def _vmem_limit(estimate_bytes):
    assert estimate_bytes <= V7X_VMEM_BYTES, estimate_bytes
    return int(estimate_bytes)


def _stage_shape(w_shape):
    return (STAGE_SLOTS, WEIGHT_CHUNK, w_shape[1])


def _cast_weight(w_hbm, w_bf16, stage, sem):
    assert w_hbm.shape[0] % WEIGHT_CHUNK == 0 and stage.shape == _stage_shape(w_hbm.shape)
    n = w_hbm.shape[0] // WEIGHT_CHUNK
    piece = lambda i: pl.ds(i * WEIGHT_CHUNK, WEIGHT_CHUNK)

    def copy(i):
        return pltpu.make_async_copy(w_hbm.at[piece(i), :], stage.at[i % STAGE_SLOTS], sem.at[i % STAGE_SLOTS])

    ahead = STAGE_SLOTS - 1
    for i in range(min(ahead, n)):
        copy(i).start(priority=i % 2)
    for i in range(n):
        if i + ahead < n:
            copy(i + ahead).start(priority=(i + ahead) % 2)
        copy(i).wait()
        w_bf16[piece(i), :] = stage[i % STAGE_SLOTS].astype(BF16)


def _ffn_ln_kernel(xp_ref, xs_ref, wg_hbm, wu_hbm, wd_hbm, g_ref, b_ref, op_ref, os_ref,
                   wg_ref, wu_ref, wd_ref, stage_wide, stage_narrow, sem, *, n_prompt_tiles, alpha):
    i = pl.program_id(0)

    @pl.when(i == 0)
    def _():
        _cast_weight(wg_hbm, wg_ref, stage_wide, sem)
        _cast_weight(wu_hbm, wu_ref, stage_wide, sem)
        _cast_weight(wd_hbm, wd_ref, stage_narrow, sem)

    is_prompt = i < n_prompt_tiles
    assert sum(FFN_BLOCKS) == xp_ref.shape[0]
    xs_, gates, ups = [], [], []
    r = 0
    for rows in FFN_BLOCKS:
        x = jnp.where(is_prompt, xp_ref[r:r + rows, :], xs_ref[r:r + rows, :])
        r += rows
        xb = x.astype(BF16)
        xs_.append(x)
        gates.append(_dot(xb, wg_ref[...]))
        ups.append(_dot(xb, wu_ref[...]))
    ys = []
    for x, gate, up in zip(xs_, gates, ups):
        act = (gate * jax.nn.sigmoid(gate) * up).astype(BF16)
        ffn = _dot(act, wd_ref[...])
        ys.append(_layer_norm(alpha * x + 0.5 * ffn, g_ref[...], b_ref[...]))
    y = jnp.concatenate(ys, axis=0)

    @pl.when(is_prompt)
    def _():
        op_ref[...] = y

    @pl.when(jnp.logical_not(is_prompt))
    def _():
        os_ref[...] = y


def _ffn_ln(xp, xs, wg, wu, wd, g, b, alpha):
    n_p, d = xp.shape
    n_s = xs.shape[0]
    d_ff = wg.shape[1]
    t = TOKEN_TILE
    assert n_p % t == 0 and n_s % t == 0
    npt, nst = n_p // t, n_s // t
    tile = (t, d)
    scratch = [(wg.shape, BF16), (wu.shape, BF16), (wd.shape, BF16), (_stage_shape(wg.shape), F32),
               (_stage_shape(wd.shape), F32)]
    vmem = (sum(_nbytes(shape, dtype) for shape, dtype in scratch)
            + 4 * 2 * _nbytes(tile, F32)
            + 2 * _nbytes((t, d_ff), F32) + _nbytes((t, d_ff), BF16)
            + 4 * _nbytes(tile, F32))
    kern = functools.partial(_ffn_ln_kernel, n_prompt_tiles=npt, alpha=alpha)
    return pl.pallas_call(
        kern,
        grid=(npt + nst,),
        in_specs=[
            pl.BlockSpec(tile, lambda i: (jnp.minimum(i, npt - 1), 0)),
            pl.BlockSpec(tile, lambda i: (jnp.maximum(i - npt, 0), 0)),
            pl.BlockSpec(memory_space=pl.ANY), pl.BlockSpec(memory_space=pl.ANY), pl.BlockSpec(memory_space=pl.ANY),
            _resident(g.shape), _resident(b.shape),
        ],
        out_specs=[
            pl.BlockSpec(tile, lambda i: (jnp.minimum(i, npt - 1), 0)),
            pl.BlockSpec(tile, lambda i: (jnp.maximum(i - npt, 0), 0)),
        ],
        out_shape=[jax.ShapeDtypeStruct(xp.shape, F32), jax.ShapeDtypeStruct(xs.shape, F32)],
        scratch_shapes=([pltpu.VMEM(shape, dtype) for shape, dtype in scratch]
                        + [pltpu.SemaphoreType.DMA((STAGE_SLOTS,))]),
        compiler_params=pltpu.CompilerParams(dimension_semantics=("arbitrary",), vmem_limit_bytes=_vmem_limit(vmem)),
        name="ffn_ln",
    )(xp, xs, wg, wu, wd, g, b)


def _store_conv_input(cbuf, glu3, first_frame):
    S, n, d_conv = glu3.shape
    tiles = d_conv // V7X_LANES
    for s in range(S):
        for c in range(tiles):
            rows = pl.ds(tiles * (CONV_PAD + first_frame) + c, n, stride=tiles)
            cbuf[s, rows, :] = glu3[s, :, c * V7X_LANES:(c + 1) * V7X_LANES]


def _broadcast_taps(wbc, cw_ref):
    for k in range(CONV_WIDTH):
        wbc[k] = jnp.broadcast_to(cw_ref[k:k + 1, :], wbc.shape[1:])


def _conv_chunk(cbuf, wbc, cb_ref, dconv, s, r0, out_row0, conv_rows):
    groups = conv_rows // V7X_SUBLANES
    tiles = wbc.shape[-1] // V7X_LANES
    for c in range(tiles):
        lanes = slice(c * V7X_LANES, (c + 1) * V7X_LANES)
        acc = [jnp.broadcast_to(cb_ref[:, lanes], (V7X_SUBLANES, V7X_LANES))] * groups
        for b in range(V7X_SUBLANES):
            a_max = (CONV_HIST - b) // V7X_SUBLANES
            rows = {g: cbuf[s, pl.ds(tiles * (r0 + CONV_PAD + V7X_SUBLANES * g - b) + c, V7X_SUBLANES, stride=tiles), :]
                    for g in range(-a_max, groups)}
            for a in range(a_max + 1):
                w = wbc[CONV_HIST - (V7X_SUBLANES * a + b), :, lanes]
                for m in range(groups):
                    acc[m] = acc[m] + rows[m - a] * w
        dconv[pl.ds(out_row0, conv_rows), lanes] = jnp.concatenate(acc, axis=0)


def _pooled(pext, L, pos1):
    pool_group = pext.shape[-1] // len(POOL_WINDOWS)
    out = []
    for gi, w in enumerate(POOL_WINDOWS):
        lanes = slice(gi * pool_group, (gi + 1) * pool_group)
        assert w & (w - 1) == 0 and w - 1 <= POOL_PAD
        total = pext[:, :, lanes]
        k = 1
        while k < w:
            total = total + jnp.concatenate([total[:, :k], total[:, :-k]], axis=1)
            k *= 2
        frame = pext[:, POOL_PAD:POOL_PAD + L, lanes]
        cnt = jnp.minimum(pos1, w).astype(F32)
        out.append(total[:, POOL_PAD:POOL_PAD + L] / cnt[None] - frame)
    return jnp.concatenate(out, axis=-1)


def _merge(h, dconv, pooled_bf16, gate_piece, n_pieces, clg_ref, clb_ref, w_cp_ref, w_pg_ref, psc_ref, w_pp_ref,
           w_out_ref, l2g_ref, l2b_ref, alpha):
    conv_act = _layer_norm(dconv, clg_ref[...], clb_ref[...])
    conv_act = conv_act * jax.nn.sigmoid(conv_act)
    branch_conv = _dot(conv_act.astype(BF16), w_cp_ref[...])
    pool_group = w_pg_ref.shape[-1]
    groups = []
    for gi in range(len(POOL_WINDOWS)):
        lanes = slice(gi * pool_group, (gi + 1) * pool_group)
        groups.append(_dot(pooled_bf16[:, lanes], w_pg_ref[gi]) * psc_ref[:, lanes])
    branch_pool = _dot(jnp.concatenate(groups, axis=-1).astype(BF16), w_pp_ref[...])
    half = n_pieces // 2
    gcols = branch_conv.shape[-1] // half
    merged = jnp.concatenate(
        [gate_piece(p) * branch_conv[:, p * gcols:(p + 1) * gcols]
         + gate_piece(half + p) * branch_pool[:, p * gcols:(p + 1) * gcols] for p in range(half)], axis=-1)
    merged = merged.astype(BF16)
    n = h.shape[0]
    rows = min(n, MERGE_ROWS)
    assert n % rows == 0
    out = []
    for r in range(0, n, rows):
        mixed = _dot(merged[r:r + rows], w_out_ref[...])
        out.append(_layer_norm(alpha * h[r:r + rows] + mixed, l2g_ref[...], l2b_ref[...]))
    return jnp.concatenate(out, axis=0)


def _mixer_prompt_kernel(h_ref, w_in_ref, w_gate_ref, b_gate_ref, cw_ref, cb_ref, clg_ref, clb_ref, w_cp_ref,
                         w_pg_ref, psc_ref, w_pp_ref, w_out_ref, l2g_ref, l2b_ref,
                         h2_ref, nc_ref, np_ref, cbuf, pext, wbc, dconv, *, alpha):
    t_idx = pl.program_id(1)
    L, d_model = h_ref.shape[1:]
    d_conv = wbc.shape[-1]
    d_pool = pext.shape[-1]
    hist_rows = cbuf.shape[1] - (d_conv // V7X_LANES) * L

    @pl.when(t_idx == 0)
    def _():
        cbuf[:, :hist_rows, :] = jnp.zeros((1, hist_rows, V7X_LANES), F32)
        pext[:, :POOL_PAD, :] = jnp.zeros((1, POOL_PAD, d_pool), F32)
        _broadcast_taps(wbc, cw_ref)

    @pl.when(t_idx > 0)
    def _():
        cbuf[:, :hist_rows, :] = cbuf[:, cbuf.shape[1] - hist_rows:, :]
        pext[:, :POOL_PAD, :] = pext[:, L:L + POOL_PAD, :]

    h = h_ref[0]
    hb = h.astype(BF16)
    u = _dot(hb, w_in_ref[...])
    gates_pool = jax.nn.sigmoid(_dot(hb, w_gate_ref[:, d_model:]) + b_gate_ref[:, d_model:])
    glu = u[:, :d_conv] * jax.nn.sigmoid(u[:, d_conv:2 * d_conv])
    _store_conv_input(cbuf, glu.reshape(1, L, d_conv), 0)
    pext[:, POOL_PAD:, :] = u[:, 2 * d_conv:].reshape(1, L, d_pool)
    row = lax.broadcasted_iota(jnp.int32, (L, d_pool // len(POOL_WINDOWS)), 0)
    pooled = _pooled(pext, L, row + 1 + t_idx * L).reshape(L, d_pool).astype(BF16)

    def conv_trip(i, carry):
        r0 = pl.multiple_of(i * CONV_ROWS, CONV_ROWS)
        _conv_chunk(cbuf, wbc, cb_ref, dconv, 0, r0, r0, CONV_ROWS)
        return carry

    lax.fori_loop(0, L // CONV_ROWS, conv_trip, 0)

    gates_conv = jax.nn.sigmoid(_dot(h_ref[0].astype(BF16), w_gate_ref[:, :d_model]) + b_gate_ref[:, :d_model])
    n_pieces = 2 * d_model // V7X_MXU_COLS
    half_pieces = n_pieces // 2
    gate_piece = lambda p: (gates_conv if p < half_pieces else gates_pool)[
        :, (p % half_pieces) * V7X_MXU_COLS:(p % half_pieces + 1) * V7X_MXU_COLS]
    h2_ref[0] = _merge(h, dconv[...], pooled, gate_piece, n_pieces, clg_ref, clb_ref, w_cp_ref, w_pg_ref, psc_ref,
                       w_pp_ref, w_out_ref, l2g_ref, l2b_ref, alpha)

    @pl.when(t_idx == pl.num_programs(1) - 1)
    def _():
        nc_ref[0] = glu[L - CONV_HIST:, :].reshape(1, CONV_HIST, d_conv)
        np_ref[0] = pext[:, POOL_PAD + L - POOL_HIST:POOL_PAD + L, :]


def _mixer_prompt(h, weights, alpha):
    (w_in, w_gate, b_gate, conv_w, conv_b, cln_g, cln_b, w_cp, w_pg, pscale, w_pp, w_out, l2g, l2b) = weights
    bsz, lseq, d = h.shape
    d_conv = conv_w.shape[1]
    d_pool = w_pp.shape[0]
    L = TOKEN_TILE
    assert lseq % L == 0 and L % CONV_ROWS == 0 and L >= CONV_HIST and L >= POOL_HIST
    hblock = (1, L, d)
    hmap = lambda b, t: (b, t, 0)
    state_map = lambda b, t: (0, b, 0, 0)
    scratch = [((1, (d_conv // V7X_LANES) * (CONV_PAD + L), V7X_LANES), F32), ((1, POOL_PAD + L, d_pool), F32),
               ((CONV_WIDTH, V7X_SUBLANES, d_conv), F32), ((L, d_conv), F32)]
    w_list = [w_in, w_gate, b_gate, conv_w, conv_b, cln_g, cln_b, w_cp, w_pg, pscale, w_pp, w_out, l2g, l2b]
    vmem = (sum(_nbytes(w.shape, w.dtype) for w in w_list)
            + 2 * 2 * _nbytes(hblock, F32)
            + 2 * 2 * (_nbytes((1, CONV_HIST, d_conv), F32) + _nbytes((1, POOL_HIST, d_pool), F32))
            + sum(_nbytes(shape, dtype) for shape, dtype in scratch)
            + _nbytes((L, w_in.shape[1]), F32) + _nbytes((L, 2 * d), F32)
            + 8 * _nbytes((L, d), F32))
    kern = functools.partial(_mixer_prompt_kernel, alpha=alpha)
    return pl.pallas_call(
        kern,
        grid=(bsz, lseq // L),
        in_specs=[pl.BlockSpec(hblock, hmap)] + [_resident(w.shape) for w in w_list],
        out_specs=[pl.BlockSpec(hblock, hmap), pl.BlockSpec((1, 1, CONV_HIST, d_conv), state_map),
                   pl.BlockSpec((1, 1, POOL_HIST, d_pool), state_map)],
        out_shape=[jax.ShapeDtypeStruct(h.shape, F32), jax.ShapeDtypeStruct((1, bsz, CONV_HIST, d_conv), F32),
                   jax.ShapeDtypeStruct((1, bsz, POOL_HIST, d_pool), F32)],
        scratch_shapes=[pltpu.VMEM(shape, dtype) for shape, dtype in scratch],
        compiler_params=pltpu.CompilerParams(dimension_semantics=("arbitrary", "arbitrary"),
                                             vmem_limit_bytes=_vmem_limit(vmem)),
        name="mixer_prompt",
    )(h, *w_list)


def _mixer_state_kernel(h_ref, cst_ref, pst_ref, w_in_ref, w_gate_ref, b_gate_ref, cw_ref, cb_ref, clg_ref, clb_ref,
                        w_cp_ref, w_pg_ref, psc_ref, w_pp_ref, w_out_ref, l2g_ref, l2b_ref,
                        h2_ref, nc_ref, np_ref, cbuf, pext, wbc, dconv, *, pos0, alpha):
    S, L, d_model = h_ref.shape
    N = S * L
    d_conv = wbc.shape[-1]
    d_pool = pext.shape[-1]

    h = h_ref[...].reshape(N, d_model)
    hb = h.astype(BF16)
    u = _dot(hb, w_in_ref[...])
    glu = u[:, :d_conv] * jax.nn.sigmoid(u[:, d_conv:2 * d_conv])

    _store_conv_input(cbuf, cst_ref[0], -CONV_HIST)
    pext[:, :POOL_PAD - POOL_HIST, :] = jnp.zeros((S, POOL_PAD - POOL_HIST, d_pool), F32)
    pext[:, POOL_PAD - POOL_HIST:POOL_PAD, :] = pst_ref[0]
    _broadcast_taps(wbc, cw_ref)
    glu3 = glu.reshape(S, L, d_conv)
    _store_conv_input(cbuf, glu3, 0)
    pext[:, POOL_PAD:, :] = u[:, 2 * d_conv:].reshape(S, L, d_pool)
    row = lax.broadcasted_iota(jnp.int32, (L, d_pool // len(POOL_WINDOWS)), 0)
    pooled = _pooled(pext, L, row + (pos0 + 1)).reshape(N, d_pool).astype(BF16)

    def conv_trip(s, carry):
        _conv_chunk(cbuf, wbc, cb_ref, dconv, s, 0, pl.multiple_of(s * L, L), L)
        return carry

    lax.fori_loop(0, S, conv_trip, 0)

    gates = jax.nn.sigmoid(_dot(h_ref[...].reshape(N, d_model).astype(BF16), w_gate_ref[...]) + b_gate_ref[...])
    n_pieces = 2 * d_model // V7X_MXU_COLS
    gate_piece = lambda p: gates[:, p * V7X_MXU_COLS:(p + 1) * V7X_MXU_COLS]
    h2 = _merge(h, dconv[...], pooled, gate_piece, n_pieces, clg_ref, clb_ref, w_cp_ref, w_pg_ref, psc_ref, w_pp_ref,
                w_out_ref, l2g_ref, l2b_ref, alpha)
    h2_ref[...] = h2.reshape(S, L, d_model)
    nc_ref[0] = glu3[:, L - CONV_HIST:, :]
    np_ref[0] = pext[:, POOL_PAD + L - POOL_HIST:POOL_PAD + L, :]


def _mixer_state(h, conv_state, pool_state, weights, pos0, alpha):
    (w_in, w_gate, b_gate, conv_w, conv_b, cln_g, cln_b, w_cp, w_pg, pscale, w_pp, w_out, l2g, l2b) = weights
    bsz, L, d = h.shape
    d_conv = conv_w.shape[1]
    d_pool = w_pp.shape[0]
    S = STATE_SEQS_PER_STEP
    N = S * L
    assert bsz % S == 0 and L % V7X_SUBLANES == 0 and L >= CONV_HIST and L >= POOL_HIST

    hblock = (S, L, d)
    hmap = lambda i: (i, 0, 0)
    cstate_spec = pl.BlockSpec((1, S, CONV_HIST, d_conv), lambda i: (0, i, 0, 0))
    pstate_spec = pl.BlockSpec((1, S, POOL_HIST, d_pool), lambda i: (0, i, 0, 0))
    scratch = [((S, (d_conv // V7X_LANES) * (CONV_PAD + L), V7X_LANES), F32), ((S, POOL_PAD + L, d_pool), F32),
               ((CONV_WIDTH, V7X_SUBLANES, d_conv), F32), ((N, d_conv), F32)]
    w_list = [w_in, w_gate, b_gate, conv_w, conv_b, cln_g, cln_b, w_cp, w_pg, pscale, w_pp, w_out, l2g, l2b]
    vmem = (sum(_nbytes(w.shape, w.dtype) for w in w_list)
            + 2 * 2 * _nbytes(hblock, F32)
            + 2 * 2 * (_nbytes((S, CONV_HIST, d_conv), F32) + _nbytes((S, POOL_HIST, d_pool), F32))
            + sum(_nbytes(shape, dtype) for shape, dtype in scratch)
            + _nbytes((N, w_in.shape[1]), F32) + _nbytes((N, 2 * d), F32)
            + 8 * _nbytes((N, d), F32))
    kern = functools.partial(_mixer_state_kernel, pos0=pos0, alpha=alpha)
    return pl.pallas_call(
        kern,
        grid=(bsz // S,),
        in_specs=[pl.BlockSpec(hblock, hmap), cstate_spec, pstate_spec] + [_resident(w.shape) for w in w_list],
        out_specs=[pl.BlockSpec(hblock, hmap), cstate_spec, pstate_spec],
        out_shape=[jax.ShapeDtypeStruct(h.shape, F32), jax.ShapeDtypeStruct(conv_state.shape, F32),
                   jax.ShapeDtypeStruct(pool_state.shape, F32)],
        scratch_shapes=[pltpu.VMEM(shape, dtype) for shape, dtype in scratch],
        compiler_params=pltpu.CompilerParams(dimension_semantics=("arbitrary",), vmem_limit_bytes=_vmem_limit(vmem)),
        name="mixer_state",
    )(h, conv_state, pool_state, *w_list)


def kernel(x_prompt, x_sample, state_conv, state_pool, w_ffn1_gate, w_ffn1_up, w_ffn1_down, ln1_g, ln1_b, w_in, w_gate, b_gate, conv_w, conv_b, conv_ln_g, conv_ln_b, w_conv_proj, w_pool_group, pool_scale, w_pool_proj, w_out, ln2_g, ln2_b, w_ffn2_gate, w_ffn2_up, w_ffn2_down, ln3_g, ln3_b):
    depth = w_in.shape[0]
    bsz, seq, d = x_prompt.shape
    dbsz, dseq, _ = x_sample.shape
    alpha = (2.0 * depth) ** 0.25
    row = lambda v: v.reshape(1, -1)

    hp = x_prompt.reshape(bsz * seq, d)
    hs = x_sample.reshape(dbsz * dseq, d)
    conv_p, conv_s, pool_p, pool_s = [], [], [], []
    for l in range(depth):
        hp, hs = _ffn_ln(hp, hs, w_ffn1_gate[l], w_ffn1_up[l], w_ffn1_down[l], row(ln1_g[l]), row(ln1_b[l]), alpha)
        mix_w = (w_in[l].astype(BF16), w_gate[l].astype(BF16), row(b_gate[l]), conv_w[l], row(conv_b[l]),
                 row(conv_ln_g[l]), row(conv_ln_b[l]), w_conv_proj[l].astype(BF16), w_pool_group[l].astype(BF16),
                 row(pool_scale[l]), w_pool_proj[l].astype(BF16), w_out[l].astype(BF16), row(ln2_g[l]), row(ln2_b[l]))
        hp3, cp, pp = _mixer_prompt(hp.reshape(bsz, seq, d), mix_w, alpha)
        hs3, cs, ps = _mixer_state(hs.reshape(dbsz, dseq, d), state_conv[l:l + 1], state_pool[l:l + 1], mix_w, PAST_LEN, alpha)
        hp, hs = _ffn_ln(hp3.reshape(bsz * seq, d), hs3.reshape(dbsz * dseq, d), w_ffn2_gate[l], w_ffn2_up[l],
                         w_ffn2_down[l], row(ln3_g[l]), row(ln3_b[l]), alpha)
        conv_p.append(cp)
        conv_s.append(cs)
        pool_p.append(pp)
        pool_s.append(ps)
    cat = lambda xs: xs[0] if len(xs) == 1 else jnp.concatenate(xs, axis=0)
    return (hp.reshape(bsz, seq, d), hs.reshape(dbsz, dseq, d), cat(conv_p), cat(conv_s), cat(pool_p), cat(pool_s))
```
